```python
import jax, jax.numpy as jnp
from jax import lax
import numpy as np

D_MODEL = 1024
BATCH = 2
SEQ = 8192
DEPTH = 4

MLA_HEADS = 8
MLA_NOPE = 64
MLA_ROPE = 32
MLA_V = 64
Q_LORA = 384
KV_LORA = 256
MLA_WIDTH = MLA_HEADS * MLA_V

DIL_PAIRS = ((128, 1), (512, 4), (2048, 16))
DIL_GROUPS = 3
DIL_HEADS = 8
DIL_HD = 64
DIL_WIDTH = DIL_HEADS * DIL_HD
ROT_DIM = DIL_HD // 4

MIX_WIDTH = MLA_WIDTH + DIL_WIDTH
ROPE_THETA = 500000.0
Q_BLOCK = 128
EPS = 1e-6

IN_SPLITS = (Q_LORA, KV_LORA, MLA_ROPE, MLA_WIDTH, 3 * DIL_GROUPS * DIL_WIDTH, DIL_WIDTH)
IN_WIDTH = Q_LORA + KV_LORA + MLA_ROPE + MLA_WIDTH + 3 * DIL_GROUPS * DIL_WIDTH + DIL_WIDTH

kernel_name = "hymba_mla_dilated_window_encoder"


def rms_norm(x, g):
    xf = x.astype(jnp.float32)
    y = xf * lax.rsqrt(jnp.mean(xf * xf, axis=-1, keepdims=True) + EPS)
    return (y * g.astype(jnp.float32)).astype(x.dtype)


def rope_tables(seq, dim):
    inv = 1.0 / (ROPE_THETA ** (jnp.arange(0, dim, 2, dtype=jnp.float32) / dim))
    ang = jnp.arange(seq, dtype=jnp.float32)[:, None] * inv[None, :]
    return jnp.cos(ang), jnp.sin(ang)


def apply_rope(x, cos, sin):
    xf = x.astype(jnp.float32)
    x1, x2 = jnp.split(xf, 2, axis=-1)
    c = cos[:, None, :]
    s = sin[:, None, :]
    return jnp.concatenate([x1 * c - x2 * s, x1 * s + x2 * c], axis=-1).astype(x.dtype)


def partial_rope(x, cos, sin):
    return jnp.concatenate([apply_rope(x[..., :ROT_DIM], cos, sin), x[..., ROT_DIM:]], axis=-1)


def mla_attention(c_q, c_kv, k_r, q_norm_g, kv_norm_g, w_uq, w_ukv, cos, sin):
    B, S, _ = c_q.shape
    q = (rms_norm(c_q, q_norm_g) @ w_uq).reshape(B, S, MLA_HEADS, MLA_NOPE + MLA_ROPE)
    q_nope = q[..., :MLA_NOPE]
    q_rope = apply_rope(q[..., MLA_NOPE:], cos, sin)
    kv = (rms_norm(c_kv, kv_norm_g) @ w_ukv).reshape(B, S, MLA_HEADS, MLA_NOPE + MLA_V)
    k_nope = kv[..., :MLA_NOPE]
    v = kv[..., MLA_NOPE:]
    k_rope = apply_rope(k_r[:, :, None, :], cos, sin)[:, :, 0, :]
    scale = (MLA_NOPE + MLA_ROPE) ** -0.5
    nb = S // Q_BLOCK
    qn_b = q_nope.reshape(B, nb, Q_BLOCK, MLA_HEADS, MLA_NOPE).swapaxes(0, 1)
    qr_b = q_rope.reshape(B, nb, Q_BLOCK, MLA_HEADS, MLA_ROPE).swapaxes(0, 1)

    def block(args):
        qn, qr = args
        s = (jnp.einsum('bqhd,bkhd->bhqk', qn, k_nope).astype(jnp.float32)
             + jnp.einsum('bqhr,bkr->bhqk', qr, k_rope).astype(jnp.float32)) * scale
        p = jax.nn.softmax(s, axis=-1)
        return jnp.einsum('bhqk,bkhd->bqhd', p.astype(v.dtype), v)

    o = lax.map(block, (qn_b, qr_b))
    return o.swapaxes(0, 1).reshape(B, S, MLA_WIDTH)


def to_strided(t, d):
    B, S = t.shape[:2]
    rest = t.shape[2:]
    return t.reshape(B, S // d, d, *rest).swapaxes(1, 2).reshape(B * d, S // d, *rest)


def from_strided(t, B, d):
    L = t.shape[1]
    rest = t.shape[2:]
    return t.reshape(B, d, L, *rest).swapaxes(1, 2).reshape(B, L * d, *rest)


def banded_attention(q, k, v, half):
    N, L, H, Dh = q.shape
    nb = -(-L // half)
    Lp = nb * half
    pad = Lp - L
    qp = jnp.pad(q, ((0, 0), (0, pad), (0, 0), (0, 0))).reshape(N, nb, half, H, Dh)

    def key_windows(t):
        tp = jnp.pad(t, ((0, 0), (half, pad + half), (0, 0), (0, 0))).reshape(N, nb + 2, half, H, Dh)
        return jnp.concatenate([tp[:, :-2], tp[:, 1:-1], tp[:, 2:]], axis=2)

    kw = key_windows(k)
    vw = key_windows(v)
    qpos = jnp.arange(Lp).reshape(nb, half)
    kpos = (jnp.arange(nb)[:, None] - 1) * half + jnp.arange(3 * half)[None, :]
    valid = ((jnp.abs(qpos[:, :, None] - kpos[:, None, :]) <= half)
             & (kpos[:, None, :] >= 0) & (kpos[:, None, :] < L))
    s = jnp.einsum('nbqhd,nbkhd->nbhqk', qp, kw).astype(jnp.float32) * (Dh ** -0.5)
    s = jnp.where(valid[None, :, None], s, -jnp.inf)
    m = jnp.max(s, axis=-1, keepdims=True)
    e = jnp.exp(s - m)
    l = jnp.sum(e, axis=-1, keepdims=True)
    o = jnp.einsum('nbhqk,nbkhd->nbqhd', (e / l).astype(v.dtype), vw)
    lse = (m + jnp.log(l))[..., 0]
    o = o.reshape(N, Lp, H, Dh)[:, :L]
    lse = lse.swapaxes(2, 3).reshape(N, Lp, H)[:, :L]
    return o, lse


def dilated_attention(dil_qkv, cos, sin):
    B, S, _ = dil_qkv.shape
    qkv = dil_qkv.reshape(B, S, DIL_GROUPS, 3, DIL_HEADS, DIL_HD)
    outs, lses = [], []
    for g, (window, dil) in enumerate(DIL_PAIRS):
        q = partial_rope(qkv[:, :, g, 0], cos, sin)
        k = partial_rope(qkv[:, :, g, 1], cos, sin)
        v = qkv[:, :, g, 2]
        o, lse = banded_attention(to_strided(q, dil), to_strided(k, dil), to_strided(v, dil),
                                  window // (2 * dil))
        outs.append(from_strided(o, B, dil))
        lses.append(from_strided(lse, B, dil))
    alpha = jax.nn.softmax(jnp.stack(lses, axis=0), axis=0)
    out = jnp.einsum('gbsh,gbshd->bshd', alpha, jnp.stack(outs, axis=0).astype(jnp.float32))
    return out.astype(dil_qkv.dtype).reshape(B, S, DIL_WIDTH)


def setup_inputs(seed: int = 0) -> dict:
    key = jax.random.key(seed)
    ks = jax.random.split(key, 10)
    f32 = jnp.float32
    x = jax.random.normal(ks[0], (BATCH, SEQ, D_MODEL), f32)
    norm_g = 1.0 + 0.02 * jax.random.normal(ks[1], (DEPTH, D_MODEL), f32)
    w_in = jax.random.normal(ks[2], (DEPTH, D_MODEL, IN_WIDTH), f32) * D_MODEL ** -0.5
    q_norm_g = 1.0 + 0.02 * jax.random.normal(ks[3], (DEPTH, Q_LORA), f32)
    kv_norm_g = 1.0 + 0.02 * jax.random.normal(ks[4], (DEPTH, KV_LORA), f32)
    w_uq = jax.random.normal(ks[5], (DEPTH, Q_LORA, MLA_HEADS * (MLA_NOPE + MLA_ROPE)), f32) * Q_LORA ** -0.5
    w_ukv = jax.random.normal(ks[6], (DEPTH, KV_LORA, MLA_HEADS * (MLA_NOPE + MLA_V)), f32) * KV_LORA ** -0.5
    w_out = jax.random.normal(ks[7], (DEPTH, MIX_WIDTH, D_MODEL), f32) * MIX_WIDTH ** -0.5
    final_g = 1.0 + 0.02 * jax.random.normal(ks[8], (D_MODEL,), f32)
    return {"x": x, "norm_g": norm_g, "w_in": w_in, "q_norm_g": q_norm_g, "kv_norm_g": kv_norm_g,
            "w_uq": w_uq, "w_ukv": w_ukv, "w_out": w_out, "final_g": final_g}


def reference(x, norm_g, w_in, q_norm_g, kv_norm_g, w_uq, w_ukv, w_out, final_g):
    S = x.shape[1]
    cos_m, sin_m = rope_tables(S, MLA_ROPE)
    cos_d, sin_d = rope_tables(S, ROT_DIM)
    split_points = [sum(IN_SPLITS[:i + 1]) for i in range(len(IN_SPLITS) - 1)]
    for layer in range(DEPTH):
        h = rms_norm(x, norm_g[layer])
        p = h @ w_in[layer]
        c_q, c_kv, k_r, gate_a, dil_qkv, gate_b = jnp.split(p, split_points, axis=-1)
        a = mla_attention(c_q, c_kv, k_r, q_norm_g[layer], kv_norm_g[layer],
                          w_uq[layer], w_ukv[layer], cos_m, sin_m) * jax.nn.silu(gate_a)
        b = dilated_attention(dil_qkv, cos_d, sin_d) * jax.nn.silu(gate_b)
        x = x + jnp.concatenate([a, b], axis=-1) @ w_out[layer]
    return rms_norm(x, final_g)
```

```python
import functools
import math

import jax
import jax.numpy as jnp
from jax import lax
from jax.experimental import pallas as pl
from jax.experimental.pallas import tpu as pltpu

F32 = jnp.float32
BF16 = jnp.bfloat16

D_MODEL = 1024
DEPTH = 4
MLA_HEADS = 8
MLA_NOPE = 64
MLA_ROPE = 32
MLA_V = 64
Q_LORA = 384
KV_LORA = 256
MLA_WIDTH = MLA_HEADS * MLA_V
DIL_PAIRS = ((128, 1), (512, 4), (2048, 16))
DIL_GROUPS = 3
DIL_HEADS = 8
DIL_HD = 64
DIL_WIDTH = DIL_HEADS * DIL_HD
ROT_DIM = DIL_HD // 4
ROPE_THETA = 500000.0
EPS = 1e-6

LANES = 128
HEAD_PAD = 128
LAT_WIDTH = Q_LORA + KV_LORA + HEAD_PAD
LOG2E = math.log2(math.e)
NEG_BIG = -1e30
VMEM_LIMIT = 48 * 1024 * 1024


def _rope_tables(seq, dim):
    inv = 1.0 / (ROPE_THETA ** (jnp.arange(0, dim, 2, dtype=F32) / dim))
    ang = jnp.arange(seq, dtype=F32)[:, None] * inv[None, :]
    return jnp.cos(ang), jnp.sin(ang)


def _mla_tables(seq, scale):
    cos, sin = _rope_tables(seq, MLA_ROPE)
    half = MLA_ROPE // 2
    ones = jnp.ones((seq, MLA_NOPE), F32)
    z = lambda n: jnp.zeros((seq, n), F32)
    c = jnp.concatenate([ones, cos, cos, z(HEAD_PAD - MLA_NOPE - MLA_ROPE)], axis=1)
    a = jnp.concatenate([z(MLA_NOPE), -sin, z(half), z(HEAD_PAD - MLA_NOPE - MLA_ROPE)], axis=1)
    b = jnp.concatenate([z(MLA_NOPE), z(half), sin, z(HEAD_PAD - MLA_NOPE - MLA_ROPE)], axis=1)
    return jnp.stack([c, a, b]) * scale


def _dil_tables(seq, scale):
    cos, sin = _rope_tables(seq, ROT_DIM)
    half = ROT_DIM // 2
    rest = DIL_HD - ROT_DIM
    z = lambda n: jnp.zeros((seq, n), F32)
    c = jnp.concatenate([cos, cos, jnp.ones((seq, rest), F32)], axis=1)
    a = jnp.concatenate([-sin, z(half), z(rest)], axis=1)
    b = jnp.concatenate([z(half), sin, z(rest)], axis=1)
    rep = LANES // DIL_HD
    return jnp.stack([jnp.tile(c, (1, rep)), jnp.tile(a, (1, rep)), jnp.tile(b, (1, rep))]) * scale


def _rotate(blk, tab_c, tab_a, tab_b, shift):
    return (blk * tab_c
            + pltpu.roll(blk, LANES - shift, 1) * tab_a
            + pltpu.roll(blk, shift, 1) * tab_b)


def _rms_mm_kernel(x_ref, g_ref, w_ref, *rest, rope):
    if rope:
        tq_ref, tk_ref, o_ref, h_scr = rest
    else:
        o_ref, h_scr = rest
    j = pl.program_id(1)

    @pl.when(j == 0)
    def _():
        x = x_ref[...]
        ms = jnp.mean(x * x, axis=-1, keepdims=True)
        h_scr[...] = (x * lax.rsqrt(ms + EPS) * g_ref[...]).astype(BF16)

    y = jnp.dot(h_scr[...], w_ref[...], preferred_element_type=F32)
    if not rope:
        o_ref[...] = y
        return

    kind = j % 3

    def roped(tab_ref):
        tc, ta, tb = tab_ref[0], tab_ref[1], tab_ref[2]
        for c in range(y.shape[1] // LANES):
            sl = slice(c * LANES, (c + 1) * LANES)
            o_ref[:, sl] = _rotate(y[:, sl], tc, ta, tb, ROT_DIM // 2)

    @pl.when(kind == 0)
    def _():
        roped(tq_ref)

    @pl.when(kind == 1)
    def _():
        roped(tk_ref)

    @pl.when(kind == 2)
    def _():
        o_ref[...] = y


def _rms_matmul(x2d, g, w, *, tm, tn, seq, tabs=None, name):
    m, k = x2d.shape
    n = w.shape[1]
    rope = tabs is not None
    in_specs = [
        pl.BlockSpec((tm, k), lambda i, j: (i, 0)),
        pl.BlockSpec((1, k), lambda i, j: (0, 0)),
        pl.BlockSpec((k, tn), lambda i, j: (0, j)),
    ]
    args = [x2d, g.reshape(1, k), w]
    if rope:
        nblk = seq // tm
        tab_spec = pl.BlockSpec((3, tm, LANES), lambda i, j: (0, i % nblk, 0))
        in_specs += [tab_spec, tab_spec]
        args += list(tabs)
    return pl.pallas_call(
        functools.partial(_rms_mm_kernel, rope=rope),
        grid=(m // tm, n // tn),
        in_specs=in_specs,
        out_specs=pl.BlockSpec((tm, tn), lambda i, j: (i, j)),
        out_shape=jax.ShapeDtypeStruct((m, n), F32),
        scratch_shapes=[pltpu.VMEM((tm, k), BF16)],
        compiler_params=pltpu.CompilerParams(
            dimension_semantics=("arbitrary", "arbitrary"), vmem_limit_bytes=VMEM_LIMIT),
        name=name,
    )(*args)


def _mla_prep_kernel(lat_ref, qg_ref, kvg_ref, wq_ref, wkt_ref, wv_ref, tq_ref, tk_ref,
                     q_out, kt_out, v_out):
    lat = lat_ref[0]
    ts = lat.shape[0]

    def norm(v, g):
        ms = jnp.mean(v * v, axis=-1, keepdims=True)
        return (v * lax.rsqrt(ms + EPS) * g).astype(BF16)

    cqn = norm(lat[:, :Q_LORA], qg_ref[...])
    ckvn = norm(lat[:, Q_LORA:Q_LORA + KV_LORA], kvg_ref[...])
    kr = lat[:, Q_LORA + KV_LORA:]

    q = jnp.dot(cqn, wq_ref[...], preferred_element_type=F32)
    qc, qa, qb = tq_ref[0], tq_ref[1], tq_ref[2]
    for h in range(MLA_HEADS):
        blk = q[:, h * HEAD_PAD:(h + 1) * HEAD_PAD]
        q_out[0, h] = _rotate(blk, qc, qa, qb, MLA_ROPE // 2).astype(BF16)

    krp = _rotate(kr, tk_ref[0], tk_ref[1], tk_ref[2], MLA_ROPE // 2)
    krt = krp.T
    kt = lax.dot_general(wkt_ref[...], ckvn, (((1,), (1,)), ((), ())),
                         preferred_element_type=F32)
    for h in range(MLA_HEADS):
        kt_out[0, h] = (kt[h * HEAD_PAD:(h + 1) * HEAD_PAD, :] + krt).astype(BF16)

    v = jnp.dot(ckvn, wv_ref[...], preferred_element_type=F32)
    lane = lax.broadcasted_iota(jnp.int32, (ts, HEAD_PAD), 1)
    ones_col = (lane == MLA_V).astype(F32)
    for h in range(MLA_HEADS):
        v_out[0, h] = (v[:, h * HEAD_PAD:(h + 1) * HEAD_PAD] + ones_col).astype(BF16)


def _mla_prep(lat, qg, kvg, wq, wkt, wv, tabs_q, tabs_k, *, ts):
    b, s, _ = lat.shape
    hp = MLA_HEADS * HEAD_PAD
    const = lambda *shape: pl.BlockSpec(shape, lambda bi, i: (0,) * len(shape))
    tab_spec = pl.BlockSpec((3, ts, LANES), lambda bi, i: (0, i, 0))
    return pl.pallas_call(
        _mla_prep_kernel,
        grid=(b, s // ts),
        in_specs=[
            pl.BlockSpec((1, ts, LAT_WIDTH), lambda bi, i: (bi, i, 0)),
            const(1, Q_LORA), const(1, KV_LORA),
            const(Q_LORA, hp), const(hp, KV_LORA), const(KV_LORA, hp),
            tab_spec, tab_spec,
        ],
        out_specs=[
            pl.BlockSpec((1, MLA_HEADS, ts, HEAD_PAD), lambda bi, i: (bi, 0, i, 0)),
            pl.BlockSpec((1, MLA_HEADS, HEAD_PAD, ts), lambda bi, i: (bi, 0, 0, i)),
            pl.BlockSpec((1, MLA_HEADS, ts, HEAD_PAD), lambda bi, i: (bi, 0, i, 0)),
        ],
        out_shape=[
            jax.ShapeDtypeStruct((b, MLA_HEADS, s, HEAD_PAD), BF16),
            jax.ShapeDtypeStruct((b, MLA_HEADS, HEAD_PAD, s), BF16),
            jax.ShapeDtypeStruct((b, MLA_HEADS, s, HEAD_PAD), BF16),
        ],
        compiler_params=pltpu.CompilerParams(
            dimension_semantics=("arbitrary", "arbitrary"), vmem_limit_bytes=VMEM_LIMIT),
        name="mla_prep",
    )(lat, qg.reshape(1, -1), kvg.reshape(1, -1), wq, wkt, wv, tabs_q, tabs_k)


def _flash_kernel(q_ref, kt_ref, v_ref, o_ref, *, tk):
    tq = q_ref.shape[2]
    s_len = kt_ref.shape[3]
    outs = []
    for hh in range(2):
        q = q_ref[0, hh]

        def body(c, carry):
            m, acc = carry
            start = pl.multiple_of(c * tk, tk)
            s = jnp.dot(q, kt_ref[0, hh, :, pl.ds(start, tk)], preferred_element_type=F32)
            m_new = jnp.maximum(m, jnp.max(s, axis=-1, keepdims=True))
            alpha = jnp.exp2(m - m_new)
            p = jnp.exp2(s - m_new).astype(BF16)
            pv = jnp.dot(p, v_ref[0, hh, pl.ds(start, tk), :], preferred_element_type=F32)
            return m_new, alpha * acc + pv

        m0 = jnp.full((tq, 1), NEG_BIG, F32)
        acc0 = jnp.zeros((tq, HEAD_PAD), F32)
        _, acc = lax.fori_loop(0, s_len // tk, body, (m0, acc0))
        outs.append(acc[:, :MLA_V] / acc[:, MLA_V:MLA_V + 1])
    o_ref[0] = jnp.concatenate(outs, axis=-1)


def _mla_flash(q, kt, v, *, tq, tk):
    b, h, s, _ = q.shape
    return pl.pallas_call(
        functools.partial(_flash_kernel, tk=tk),
        grid=(b, h // 2, s // tq),
        in_specs=[
            pl.BlockSpec((1, 2, tq, HEAD_PAD), lambda bi, hp, i: (bi, hp, i, 0)),
            pl.BlockSpec((1, 2, HEAD_PAD, s), lambda bi, hp, i: (bi, hp, 0, 0)),
            pl.BlockSpec((1, 2, s, HEAD_PAD), lambda bi, hp, i: (bi, hp, 0, 0)),
        ],
        out_specs=pl.BlockSpec((1, tq, 2 * MLA_V), lambda bi, hp, i: (bi, i, hp)),
        out_shape=jax.ShapeDtypeStruct((b, s, MLA_WIDTH), F32),
        compiler_params=pltpu.CompilerParams(
            dimension_semantics=("arbitrary", "arbitrary", "arbitrary"),
            vmem_limit_bytes=VMEM_LIMIT),
        name="mla_flash",
    )(q, kt, v)


def _band_kernel(q_ref, kp_ref, km_ref, kn_ref, vp_ref, vm_ref, vn_ref, o_ref, lse_ref,
                 kwin, vwin, *, half, length, sub):
    tq = q_ref.shape[1]
    i = pl.program_id(2)
    kwin[0:half] = kp_ref[0].astype(BF16)
    kwin[half:half + tq] = km_ref[0].astype(BF16)
    kwin[half + tq:] = kn_ref[0].astype(BF16)
    vwin[0:half] = vp_ref[0].astype(BF16)
    vwin[half:half + tq] = vm_ref[0].astype(BF16)
    vwin[half + tq:] = vn_ref[0].astype(BF16)

    wk = sub + 2 * half
    row = lax.broadcasted_iota(jnp.int32, (sub, wk), 0)
    col = lax.broadcasted_iota(jnp.int32, (sub, wk), 1)
    rel = col - half - row
    band = (rel >= -half) & (rel <= half)

    def step(t, carry):
        r0 = pl.multiple_of(t * sub, sub)
        kpos = i * tq + r0 - half + col
        mask = band & (kpos >= 0) & (kpos < length)
        for h in range(DIL_HEADS):
            hs = slice(h * DIL_HD, (h + 1) * DIL_HD)
            q = q_ref[0, pl.ds(r0, sub), hs].astype(BF16)
            k = kwin[pl.ds(r0, wk), hs]
            v = vwin[pl.ds(r0, wk), hs]
            s = lax.dot_general(q, k, (((1,), (1,)), ((), ())), preferred_element_type=F32)
            s = jnp.where(mask, s, NEG_BIG)
            m = jnp.max(s, axis=-1, keepdims=True)
            p = jnp.exp2(s - m)
            l = jnp.sum(p, axis=-1, keepdims=True)
            o = jnp.dot(p.astype(BF16), v, preferred_element_type=F32) / l
            o_ref[0, pl.ds(r0, sub), hs] = o
            lse_ref[0, pl.ds(r0, sub), hs] = jnp.broadcast_to(m + jnp.log2(l), (sub, DIL_HD))
        return carry

    lax.fori_loop(0, tq // sub, step, 0)


def _band_attention(dil, g, *, tq, sub):
    window, d = DIL_PAIRS[g]
    half = window // (2 * d)
    b, s, width = dil.shape
    length = s // d
    tq = min(tq, length)
    ncol = width // DIL_WIDTH
    view = dil.reshape(b, length, d * width)
    nhalf = length // half
    per = tq // half

    def col(c):
        return lambda bi, r, i: (bi, i, r * ncol + 3 * g + c)

    def prev(c):
        return lambda bi, r, i: (bi, jnp.maximum(i * per - 1, 0), r * ncol + 3 * g + c)

    def nxt(c):
        return lambda bi, r, i: (bi, jnp.minimum((i + 1) * per, nhalf - 1), r * ncol + 3 * g + c)

    main = lambda c: pl.BlockSpec((1, tq, DIL_WIDTH), col(c))
    edge = lambda f, c: pl.BlockSpec((1, half, DIL_WIDTH), f(c))
    out_spec = pl.BlockSpec((1, tq, DIL_WIDTH), lambda bi, r, i: (bi, i, r))
    o, lse = pl.pallas_call(
        functools.partial(_band_kernel, half=half, length=length, sub=sub),
        grid=(b, d, length // tq),
        in_specs=[main(0), edge(prev, 1), main(1), edge(nxt, 1), edge(prev, 2), main(2), edge(nxt, 2)],
        out_specs=[out_spec, out_spec],
        out_shape=[jax.ShapeDtypeStruct((b, length, d * DIL_WIDTH), F32)] * 2,
        scratch_shapes=[pltpu.VMEM((tq + 2 * half, DIL_WIDTH), BF16)] * 2,
        compiler_params=pltpu.CompilerParams(
            dimension_semantics=("arbitrary", "arbitrary", "arbitrary"),
            vmem_limit_bytes=VMEM_LIMIT),
        name=f"band_attn_d{d}",
    )(view, view, view, view, view, view, view)
    return o.reshape(b, s, DIL_WIDTH), lse.reshape(b, s, DIL_WIDTH)


def _silu(g):
    return g / (1.0 + jnp.exp(-g))


def _out_kernel(x_ref, a_ref, gate_ref, o1, l1, o2, l2, o3, l3, w_ref, fg_ref, y_ref, *, final):
    la, lb, lc = l1[...], l2[...], l3[...]
    mx = jnp.maximum(jnp.maximum(la, lb), lc)
    ea, eb, ec = jnp.exp2(la - mx), jnp.exp2(lb - mx), jnp.exp2(lc - mx)
    bmix = (ea * o1[...] + eb * o2[...] + ec * o3[...]) / (ea + eb + ec)
    gates = gate_ref[...]
    mix = jnp.concatenate([a_ref[...] * _silu(gates[:, :MLA_WIDTH]),
                           bmix * _silu(gates[:, MLA_WIDTH:])], axis=-1).astype(BF16)
    y = x_ref[...] + jnp.dot(mix, w_ref[...], preferred_element_type=F32)
    if final:
        ms = jnp.mean(y * y, axis=-1, keepdims=True)
        y = y * lax.rsqrt(ms + EPS) * fg_ref[...]
    y_ref[...] = y


def _out_proj(x2d, a2d, gates, groups, w_out, final_g, *, tm, final):
    m, dm = x2d.shape
    row = lambda n: pl.BlockSpec((tm, n), lambda i: (i, 0))
    args = [x2d, a2d, gates]
    specs = [row(dm), row(MLA_WIDTH), row(2 * MLA_WIDTH)]
    for o, lse in groups:
        args += [o, lse]
        specs += [row(DIL_WIDTH), row(DIL_WIDTH)]
    args += [w_out, final_g.reshape(1, dm)]
    specs += [pl.BlockSpec((dm, dm), lambda i: (0, 0)), pl.BlockSpec((1, dm), lambda i: (0, 0))]
    return pl.pallas_call(
        functools.partial(_out_kernel, final=final),
        grid=(m // tm,),
        in_specs=specs,
        out_specs=row(dm),
        out_shape=jax.ShapeDtypeStruct((m, dm), F32),
        compiler_params=pltpu.CompilerParams(
            dimension_semantics=("arbitrary",), vmem_limit_bytes=VMEM_LIMIT),
        name="merge_out_proj",
    )(*args)


def _split_w_in(w):
    o_kr = Q_LORA + KV_LORA
    o_ga = o_kr + MLA_ROPE
    o_dil = o_ga + MLA_WIDTH
    o_gb = o_dil + 3 * DIL_GROUPS * DIL_WIDTH
    k = w.shape[0]
    w_dil = w[:, o_dil:o_gb]
    w_gate = jnp.concatenate([w[:, o_ga:o_dil], w[:, o_gb:]], axis=1)
    w_lat = jnp.concatenate([w[:, :o_kr], jnp.zeros((k, MLA_NOPE), w.dtype), w[:, o_kr:o_ga],
                             jnp.zeros((k, HEAD_PAD - MLA_NOPE - MLA_ROPE), w.dtype)], axis=1)
    return w_dil.astype(BF16), w_gate.astype(BF16), w_lat.astype(BF16)


def _pad_heads(w, per_head, keep):
    k = w.shape[0]
    wh = w.reshape(k, MLA_HEADS, per_head)[:, :, keep]
    wh = jnp.pad(wh, ((0, 0), (0, 0), (0, HEAD_PAD - wh.shape[2])))
    return wh.reshape(k, MLA_HEADS * HEAD_PAD)


def kernel(x, norm_g, w_in, q_norm_g, kv_norm_g, w_uq, w_ukv, w_out, final_g):
    b, s, dm = x.shape
    m = b * s
    mla_scale = (MLA_NOPE + MLA_ROPE) ** -0.5 * LOG2E
    dil_scale = DIL_HD ** -0.5 * LOG2E
    tabs_mq = _mla_tables(s, mla_scale)
    tabs_mk = _mla_tables(s, 1.0)
    tabs_dq = _dil_tables(s, dil_scale)
    tabs_dk = _dil_tables(s, 1.0)

    x2d = x.reshape(m, dm)
    for layer in range(DEPTH):
        w_dil, w_gate, w_lat = _split_w_in(w_in[layer])
        wq = _pad_heads(w_uq[layer], MLA_NOPE + MLA_ROPE, slice(None)).astype(BF16)
        wkt = _pad_heads(w_ukv[layer], MLA_NOPE + MLA_V, slice(0, MLA_NOPE)).astype(BF16).T
        wv = _pad_heads(w_ukv[layer], MLA_NOPE + MLA_V, slice(MLA_NOPE, None)).astype(BF16)
        g = norm_g[layer]

        dil = _rms_matmul(x2d, g, w_dil, tm=1024, tn=DIL_WIDTH, seq=s,
                          tabs=(tabs_dq, tabs_dk), name="in_proj_dil")
        gates = _rms_matmul(x2d, g, w_gate, tm=1024, tn=512, seq=s, name="in_proj_gate")
        lat = _rms_matmul(x2d, g, w_lat, tm=1024, tn=LAT_WIDTH, seq=s, name="in_proj_lat")

        q, kt, v = _mla_prep(lat.reshape(b, s, LAT_WIDTH), q_norm_g[layer], kv_norm_g[layer],
                             wq, wkt, wv, tabs_mq, tabs_mk, ts=512)
        a = _mla_flash(q, kt, v, tq=512, tk=1024)

        dil3 = dil.reshape(b, s, 3 * DIL_GROUPS * DIL_WIDTH)
        groups = []
        for gi in range(DIL_GROUPS):
            o, lse = _band_attention(dil3, gi, tq=512, sub=128)
            groups.append((o.reshape(m, DIL_WIDTH), lse.reshape(m, DIL_WIDTH)))

        x2d = _out_proj(x2d, a.reshape(m, MLA_WIDTH), gates, groups, w_out[layer].astype(BF16),
                        final_g, tm=512, final=(layer == DEPTH - 1))
    return x2d.reshape(b, s, dm)
```

```python
import functools
import math

import jax
import jax.numpy as jnp
from jax import lax
from jax.experimental import pallas as pl
from jax.experimental.pallas import tpu as pltpu

F32 = jnp.float32
BF16 = jnp.bfloat16

D_MODEL = 1024
DEPTH = 4
MLA_HEADS = 8
MLA_NOPE = 64
MLA_ROPE = 32
MLA_V = 64
Q_LORA = 384
KV_LORA = 256
MLA_WIDTH = MLA_HEADS * MLA_V
DIL_PAIRS = ((128, 1), (512, 4), (2048, 16))
DIL_GROUPS = 3
DIL_HEADS = 8
DIL_HD = 64
DIL_WIDTH = DIL_HEADS * DIL_HD
ROT_DIM = DIL_HD // 4
ROPE_THETA = 500000.0
EPS = 1e-6

LANES = 128
HEAD_PAD = 128
LAT_WIDTH = Q_LORA + KV_LORA + HEAD_PAD
LOG2E = math.log2(math.e)
NEG_BIG = -1e30
VMEM_LIMIT = 48 * 1024 * 1024


def _rope_tables(seq, dim):
    inv = 1.0 / (ROPE_THETA ** (jnp.arange(0, dim, 2, dtype=F32) / dim))
    ang = jnp.arange(seq, dtype=F32)[:, None] * inv[None, :]
    return jnp.cos(ang), jnp.sin(ang)


def _mla_tables(seq, scale):
    cos, sin = _rope_tables(seq, MLA_ROPE)
    half = MLA_ROPE // 2
    ones = jnp.ones((seq, MLA_NOPE), F32)
    z = lambda n: jnp.zeros((seq, n), F32)
    c = jnp.concatenate([ones, cos, cos, z(HEAD_PAD - MLA_NOPE - MLA_ROPE)], axis=1)
    a = jnp.concatenate([z(MLA_NOPE), -sin, z(half), z(HEAD_PAD - MLA_NOPE - MLA_ROPE)], axis=1)
    b = jnp.concatenate([z(MLA_NOPE), z(half), sin, z(HEAD_PAD - MLA_NOPE - MLA_ROPE)], axis=1)
    return jnp.stack([c, a, b]) * scale


def _dil_tables(seq, scale):
    cos, sin = _rope_tables(seq, ROT_DIM)
    half = ROT_DIM // 2
    rest = DIL_HD - ROT_DIM
    z = lambda n: jnp.zeros((seq, n), F32)
    c = jnp.concatenate([cos, cos, jnp.ones((seq, rest), F32)], axis=1)
    a = jnp.concatenate([-sin, z(half), z(rest)], axis=1)
    b = jnp.concatenate([z(half), sin, z(rest)], axis=1)
    rep = LANES // DIL_HD
    return jnp.stack([jnp.tile(c, (1, rep)), jnp.tile(a, (1, rep)), jnp.tile(b, (1, rep))]) * scale


def _rotate(blk, tab_c, tab_a, tab_b, shift):
    return (blk * tab_c
            + pltpu.roll(blk, LANES - shift, 1) * tab_a
            + pltpu.roll(blk, shift, 1) * tab_b)


def _rms_to_bf16(x, g):
    ms = jnp.mean(x * x, axis=-1, keepdims=True)
    return (x * lax.rsqrt(ms + EPS) * g).astype(BF16)


def _rms_mm_kernel(x_ref, g_ref, w_ref, o_ref, h_scr):
    @pl.when(pl.program_id(1) == 0)
    def _():
        h_scr[...] = _rms_to_bf16(x_ref[...], g_ref[...])

    o_ref[...] = jnp.dot(h_scr[...], w_ref[...], preferred_element_type=F32)


def _rms_matmul(x2d, g, w, *, tm, tn, name):
    m, k = x2d.shape
    n = w.shape[1]
    return pl.pallas_call(
        _rms_mm_kernel,
        grid=(m // tm, n // tn),
        in_specs=[
            pl.BlockSpec((tm, k), lambda i, j: (i, 0)),
            pl.BlockSpec((1, k), lambda i, j: (0, 0)),
            pl.BlockSpec((k, tn), lambda i, j: (0, j)),
        ],
        out_specs=pl.BlockSpec((tm, tn), lambda i, j: (i, j)),
        out_shape=jax.ShapeDtypeStruct((m, n), F32),
        scratch_shapes=[pltpu.VMEM((tm, k), BF16)],
        compiler_params=pltpu.CompilerParams(
            dimension_semantics=("arbitrary", "arbitrary"), vmem_limit_bytes=VMEM_LIMIT),
        name=name,
    )(x2d, g.reshape(1, k), w)


def _in_proj_dil_kernel(x_ref, g_ref, w_ref, tq_ref, tk_ref, *rest):
    out_refs = rest[:DIL_GROUPS]
    h_scr, y_scr = rest[DIL_GROUPS:]
    tm = x_ref.shape[0]
    j = pl.program_id(1)

    @pl.when(j == 0)
    def _():
        h_scr[...] = _rms_to_bf16(x_ref[...], g_ref[...])

    y = jnp.dot(h_scr[...], w_ref[...], preferred_element_type=F32)
    kind = j % 3
    grp = j // 3

    ntile = y.shape[1] // LANES
    tiles = [slice(c * LANES, (c + 1) * LANES) for c in range(ntile)]

    def roped(tab_ref):
        tc, ta, tb = tab_ref[0], tab_ref[1], tab_ref[2]
        for c, sl in enumerate(tiles):
            y_scr[c] = _rotate(y[:, sl], tc, ta, tb, ROT_DIM // 2)

    @pl.when(kind == 0)
    def _():
        roped(tq_ref)

    @pl.when(kind == 1)
    def _():
        roped(tk_ref)

    @pl.when(kind == 2)
    def _():
        for c, sl in enumerate(tiles):
            y_scr[c] = y[:, sl]

    for gi, (_, d) in enumerate(DIL_PAIRS):
        @pl.when(grp == gi)
        def _(gi=gi, d=d):
            rows = tm // d
            for r in range(d):
                for c, sl in enumerate(tiles):
                    src = y_scr[c] if d == 1 else y_scr[c, pl.ds(r, rows, stride=d), :]
                    out_refs[gi][0, r, :, sl] = src.astype(BF16)


def _in_proj_dil(x2d, g, w, tabs_q, tabs_k, *, tm, batch, seq):
    m, k = x2d.shape
    tn = DIL_WIDTH
    nblk = seq // tm
    tab_spec = pl.BlockSpec((3, tm, LANES), lambda i, j: (0, i % nblk, 0))

    def out_map(gi):
        return lambda i, j: (i // nblk, 0, i % nblk, jnp.clip(j - 3 * gi, 0, 2))

    out_specs, out_shape = [], []
    for gi, (_, d) in enumerate(DIL_PAIRS):
        out_specs.append(pl.BlockSpec((1, d, tm // d, tn), out_map(gi)))
        out_shape.append(jax.ShapeDtypeStruct((batch, d, seq // d, 3 * tn), BF16))
    return pl.pallas_call(
        _in_proj_dil_kernel,
        grid=(m // tm, 3 * DIL_GROUPS),
        in_specs=[
            pl.BlockSpec((tm, k), lambda i, j: (i, 0)),
            pl.BlockSpec((1, k), lambda i, j: (0, 0)),
            pl.BlockSpec((k, tn), lambda i, j: (0, j)),
            tab_spec, tab_spec,
        ],
        out_specs=out_specs,
        out_shape=out_shape,
        scratch_shapes=[pltpu.VMEM((tm, k), BF16), pltpu.VMEM((tn // LANES, tm, LANES), F32)],
        compiler_params=pltpu.CompilerParams(
            dimension_semantics=("arbitrary", "arbitrary"), vmem_limit_bytes=VMEM_LIMIT),
        name="in_proj_dil",
    )(x2d, g.reshape(1, k), w, tabs_q, tabs_k)


def _mla_prep_kernel(lat_ref, qg_ref, kvg_ref, wq_ref, wkt_ref, wv_ref, tq_ref, tk_ref,
                     q_out, kt_out, v_out):
    lat = lat_ref[0]
    ts = lat.shape[0]
    cqn = _rms_to_bf16(lat[:, :Q_LORA], qg_ref[...])
    ckvn = _rms_to_bf16(lat[:, Q_LORA:Q_LORA + KV_LORA], kvg_ref[...])
    kr = lat[:, Q_LORA + KV_LORA:]

    q = jnp.dot(cqn, wq_ref[...], preferred_element_type=F32)
    qc, qa, qb = tq_ref[0], tq_ref[1], tq_ref[2]
    for h in range(MLA_HEADS):
        blk = q[:, h * HEAD_PAD:(h + 1) * HEAD_PAD]
        q_out[0, h] = _rotate(blk, qc, qa, qb, MLA_ROPE // 2).astype(BF16)

    krp = _rotate(kr, tk_ref[0], tk_ref[1], tk_ref[2], MLA_ROPE // 2)
    krt = krp.T
    kt = lax.dot_general(wkt_ref[...], ckvn, (((1,), (1,)), ((), ())),
                         preferred_element_type=F32)
    for h in range(MLA_HEADS):
        kt_out[0, h] = (kt[h * HEAD_PAD:(h + 1) * HEAD_PAD, :] + krt).astype(BF16)

    v = jnp.dot(ckvn, wv_ref[...], preferred_element_type=F32)
    lane = lax.broadcasted_iota(jnp.int32, (ts, HEAD_PAD), 1)
    ones_col = (lane == MLA_V).astype(F32)
    for h in range(MLA_HEADS):
        v_out[0, h] = (v[:, h * HEAD_PAD:(h + 1) * HEAD_PAD] + ones_col).astype(BF16)


def _mla_prep(lat, qg, kvg, wq, wkt, wv, tabs_q, tabs_k, *, ts):
    b, s, _ = lat.shape
    hp = MLA_HEADS * HEAD_PAD
    const = lambda *shape: pl.BlockSpec(shape, lambda bi, i: (0,) * len(shape))
    tab_spec = pl.BlockSpec((3, ts, LANES), lambda bi, i: (0, i, 0))
    return pl.pallas_call(
        _mla_prep_kernel,
        grid=(b, s // ts),
        in_specs=[
            pl.BlockSpec((1, ts, LAT_WIDTH), lambda bi, i: (bi, i, 0)),
            const(1, Q_LORA), const(1, KV_LORA),
            const(Q_LORA, hp), const(hp, KV_LORA), const(KV_LORA, hp),
            tab_spec, tab_spec,
        ],
        out_specs=[
            pl.BlockSpec((1, MLA_HEADS, ts, HEAD_PAD), lambda bi, i: (bi, 0, i, 0)),
            pl.BlockSpec((1, MLA_HEADS, HEAD_PAD, ts), lambda bi, i: (bi, 0, 0, i)),
            pl.BlockSpec((1, MLA_HEADS, ts, HEAD_PAD), lambda bi, i: (bi, 0, i, 0)),
        ],
        out_shape=[
            jax.ShapeDtypeStruct((b, MLA_HEADS, s, HEAD_PAD), BF16),
            jax.ShapeDtypeStruct((b, MLA_HEADS, HEAD_PAD, s), BF16),
            jax.ShapeDtypeStruct((b, MLA_HEADS, s, HEAD_PAD), BF16),
        ],
        compiler_params=pltpu.CompilerParams(
            dimension_semantics=("arbitrary", "arbitrary"), vmem_limit_bytes=VMEM_LIMIT),
        name="mla_prep",
    )(lat, qg.reshape(1, -1), kvg.reshape(1, -1), wq, wkt, wv, tabs_q, tabs_k)


def _flash_kernel(q_ref, kt_ref, v_ref, o_ref, *, tk):
    tq = q_ref.shape[2]
    s_len = kt_ref.shape[3]
    outs = []
    for hh in range(2):
        q = q_ref[0, hh]

        def body(c, carry):
            m, acc = carry
            start = pl.multiple_of(c * tk, tk)
            s = jnp.dot(q, kt_ref[0, hh, :, pl.ds(start, tk)], preferred_element_type=F32)
            m_new = jnp.maximum(m, jnp.max(s, axis=-1, keepdims=True))
            alpha = jnp.exp2(m - m_new)
            p = jnp.exp2(s - m_new).astype(BF16)
            pv = jnp.dot(p, v_ref[0, hh, pl.ds(start, tk), :], preferred_element_type=F32)
            return m_new, alpha * acc + pv

        m0 = jnp.full((tq, 1), NEG_BIG, F32)
        acc0 = jnp.zeros((tq, HEAD_PAD), F32)
        _, acc = lax.fori_loop(0, s_len // tk, body, (m0, acc0), unroll=True)
        outs.append(acc[:, :MLA_V] / acc[:, MLA_V:MLA_V + 1])
    o_ref[0] = jnp.concatenate(outs, axis=-1)


def _mla_flash(q, kt, v, *, tq, tk):
    b, h, s, _ = q.shape
    return pl.pallas_call(
        functools.partial(_flash_kernel, tk=tk),
        grid=(b, h // 2, s // tq),
        in_specs=[
            pl.BlockSpec((1, 2, tq, HEAD_PAD), lambda bi, hp, i: (bi, hp, i, 0)),
            pl.BlockSpec((1, 2, HEAD_PAD, s), lambda bi, hp, i: (bi, hp, 0, 0)),
            pl.BlockSpec((1, 2, s, HEAD_PAD), lambda bi, hp, i: (bi, hp, 0, 0)),
        ],
        out_specs=pl.BlockSpec((1, tq, 2 * MLA_V), lambda bi, hp, i: (bi, i, hp)),
        out_shape=jax.ShapeDtypeStruct((b, s, MLA_WIDTH), F32),
        compiler_params=pltpu.CompilerParams(
            dimension_semantics=("arbitrary", "arbitrary", "arbitrary"),
            vmem_limit_bytes=VMEM_LIMIT),
        name="mla_flash",
    )(q, kt, v)


def _band_kernel(q_ref, kp_ref, km_ref, kn_ref, vp_ref, vm_ref, vn_ref, o_ref, lse_ref,
                 kwin, vwin, *, half, length, sub):
    tq = q_ref.shape[2]
    i = pl.program_id(2)
    kwin[0:half] = kp_ref[0, 0]
    kwin[half:half + tq] = km_ref[0, 0]
    kwin[half + tq:] = kn_ref[0, 0]
    vwin[0:half] = vp_ref[0, 0]
    vwin[half:half + tq] = vm_ref[0, 0]
    vwin[half + tq:] = vn_ref[0, 0]

    wk = sub + 2 * half
    row = lax.broadcasted_iota(jnp.int32, (2 * sub, wk), 0) % sub
    col = lax.broadcasted_iota(jnp.int32, (2 * sub, wk), 1)
    rel = col - half - row
    band = (rel >= -half) & (rel <= half)
    lo_q = lax.broadcasted_iota(jnp.int32, (sub, LANES), 1) < DIL_HD
    ones = jnp.ones((wk, LANES), BF16)

    def step(t, carry):
        r0 = pl.multiple_of(t * sub, sub)
        kpos = i * tq + r0 - half + col
        mask = band & (kpos >= 0) & (kpos < length)
        for c in range(DIL_WIDTH // LANES):
            cs = slice(c * LANES, (c + 1) * LANES)
            qt = q_ref[0, 0, pl.ds(r0, sub), cs]
            zero = jnp.zeros_like(qt)
            q2 = jnp.concatenate([jnp.where(lo_q, qt, zero), jnp.where(lo_q, zero, qt)], axis=0)
            kt = kwin[pl.ds(r0, wk), cs]
            vt = vwin[pl.ds(r0, wk), cs]
            s = lax.dot_general(q2, kt, (((1,), (1,)), ((), ())), preferred_element_type=F32)
            s = jnp.where(mask, s, NEG_BIG)
            m = jnp.max(s, axis=-1, keepdims=True)
            p = jnp.exp2(s - m).astype(BF16)
            ol = jnp.dot(p, jnp.concatenate([vt, ones], axis=1), preferred_element_type=F32)
            o = jnp.where(lo_q, ol[:sub, :LANES], ol[sub:, :LANES])
            l = jnp.where(lo_q, ol[:sub, LANES:], ol[sub:, LANES:])
            mm = jnp.where(lo_q, m[:sub], m[sub:])
            o_ref[0, 0, pl.ds(r0, sub), cs] = o / l
            lse_ref[0, 0, pl.ds(r0, sub), cs] = mm + jnp.log2(l)
        return carry

    lax.fori_loop(0, tq // sub, step, 0)


def _band_attention(qkv, g, *, tq, sub):
    window, d = DIL_PAIRS[g]
    half = window // (2 * d)
    b, _, length, _ = qkv.shape
    tq = min(tq, length)
    nhalf = length // half
    per = tq // half

    def col(c):
        return lambda bi, r, i: (bi, r, i, c)

    def prev(c):
        return lambda bi, r, i: (bi, r, jnp.maximum(i * per - 1, 0), c)

    def nxt(c):
        return lambda bi, r, i: (bi, r, jnp.minimum((i + 1) * per, nhalf - 1), c)

    main = lambda c: pl.BlockSpec((1, 1, tq, DIL_WIDTH), col(c))
    edge = lambda f, c: pl.BlockSpec((1, 1, half, DIL_WIDTH), f(c))
    out_spec = pl.BlockSpec((1, 1, tq, DIL_WIDTH), lambda bi, r, i: (bi, r, i, 0))
    return pl.pallas_call(
        functools.partial(_band_kernel, half=half, length=length, sub=sub),
        grid=(b, d, length // tq),
        in_specs=[main(0), edge(prev, 1), main(1), edge(nxt, 1), edge(prev, 2), main(2), edge(nxt, 2)],
        out_specs=[out_spec, out_spec],
        out_shape=[jax.ShapeDtypeStruct((b, d, length, DIL_WIDTH), F32)] * 2,
        scratch_shapes=[pltpu.VMEM((tq + 2 * half, DIL_WIDTH), BF16)] * 2,
        compiler_params=pltpu.CompilerParams(
            dimension_semantics=("arbitrary", "arbitrary", "arbitrary"),
            vmem_limit_bytes=VMEM_LIMIT),
        name=f"band_attn_d{d}",
    )(qkv, qkv, qkv, qkv, qkv, qkv, qkv)


def _silu(g):
    return g / (1.0 + jnp.exp(-g))


def _out_kernel(x_ref, a_ref, gate_ref, o1, l1, o2, l2, o3, l3, w_ref, fg_ref, y_ref, nat_scr,
                *, final):
    tm = x_ref.shape[0]

    def natural(ref, slot):
        d = ref.shape[1]
        if d == 1:
            return ref[0, 0]
        ntile = ref.shape[3] // LANES
        for r in range(d):
            for c in range(ntile):
                nat_scr[slot * ntile + c, pl.ds(r, tm // d, stride=d), :] = (
                    ref[0, r, :, c * LANES:(c + 1) * LANES])
        return jnp.concatenate([nat_scr[slot * ntile + c] for c in range(ntile)], axis=1)

    la, lb, lc = l1[0, 0], natural(l2, 0), natural(l3, 1)
    mx = jnp.maximum(jnp.maximum(la, lb), lc)
    ea, eb, ec = jnp.exp2(la - mx), jnp.exp2(lb - mx), jnp.exp2(lc - mx)
    bmix = (ea * o1[0, 0] + eb * natural(o2, 2) + ec * natural(o3, 3)) / (ea + eb + ec)
    gates = gate_ref[...]
    mix = jnp.concatenate([a_ref[...] * _silu(gates[:, :MLA_WIDTH]),
                           bmix * _silu(gates[:, MLA_WIDTH:])], axis=-1).astype(BF16)
    y = x_ref[...] + jnp.dot(mix, w_ref[...], preferred_element_type=F32)
    if final:
        ms = jnp.mean(y * y, axis=-1, keepdims=True)
        y = y * lax.rsqrt(ms + EPS) * fg_ref[...]
    y_ref[...] = y


def _out_proj(x2d, a2d, gates, groups, w_out, final_g, *, tm, seq, final):
    m, dm = x2d.shape
    nblk = seq // tm
    row = lambda n: pl.BlockSpec((tm, n), lambda i: (i, 0))
    args = [x2d, a2d, gates]
    specs = [row(dm), row(MLA_WIDTH), row(2 * MLA_WIDTH)]
    for (_, d), (o, lse) in zip(DIL_PAIRS, groups):
        spec = pl.BlockSpec((1, d, tm // d, DIL_WIDTH), lambda i: (i // nblk, 0, i % nblk, 0))
        args += [o, lse]
        specs += [spec, spec]
    args += [w_out, final_g.reshape(1, dm)]
    specs += [pl.BlockSpec((dm, dm), lambda i: (0, 0)), pl.BlockSpec((1, dm), lambda i: (0, 0))]
    return pl.pallas_call(
        functools.partial(_out_kernel, final=final),
        grid=(m // tm,),
        in_specs=specs,
        out_specs=row(dm),
        out_shape=jax.ShapeDtypeStruct((m, dm), F32),
        scratch_shapes=[pltpu.VMEM((4 * DIL_WIDTH // LANES, tm, LANES), F32)],
        compiler_params=pltpu.CompilerParams(
            dimension_semantics=("arbitrary",), vmem_limit_bytes=VMEM_LIMIT),
        name="merge_out_proj",
    )(*args)


def _split_w_in(w):
    o_kr = Q_LORA + KV_LORA
    o_ga = o_kr + MLA_ROPE
    o_dil = o_ga + MLA_WIDTH
    o_gb = o_dil + 3 * DIL_GROUPS * DIL_WIDTH
    k = w.shape[0]
    w_dil = w[:, o_dil:o_gb]
    w_gate = jnp.concatenate([w[:, o_ga:o_dil], w[:, o_gb:]], axis=1)
    w_lat = jnp.concatenate([w[:, :o_kr], jnp.zeros((k, MLA_NOPE), w.dtype), w[:, o_kr:o_ga],
                             jnp.zeros((k, HEAD_PAD - MLA_NOPE - MLA_ROPE), w.dtype)], axis=1)
    return w_dil.astype(BF16), w_gate.astype(BF16), w_lat.astype(BF16)


def _pad_heads(w, per_head, keep):
    k = w.shape[0]
    wh = w.reshape(k, MLA_HEADS, per_head)[:, :, keep]
    wh = jnp.pad(wh, ((0, 0), (0, 0), (0, HEAD_PAD - wh.shape[2])))
    return wh.reshape(k, MLA_HEADS * HEAD_PAD)


def kernel(x, norm_g, w_in, q_norm_g, kv_norm_g, w_uq, w_ukv, w_out, final_g):
    b, s, dm = x.shape
    m = b * s
    mla_scale = (MLA_NOPE + MLA_ROPE) ** -0.5 * LOG2E
    dil_scale = DIL_HD ** -0.5 * LOG2E
    tabs_mq = _mla_tables(s, mla_scale)
    tabs_mk = _mla_tables(s, 1.0)
    tabs_dq = _dil_tables(s, dil_scale)
    tabs_dk = _dil_tables(s, 1.0)

    x2d = x.reshape(m, dm)
    for layer in range(DEPTH):
        w_dil, w_gate, w_lat = _split_w_in(w_in[layer])
        wq = _pad_heads(w_uq[layer], MLA_NOPE + MLA_ROPE, slice(None)).astype(BF16)
        wkt = _pad_heads(w_ukv[layer], MLA_NOPE + MLA_V, slice(0, MLA_NOPE)).astype(BF16).T
        wv = _pad_heads(w_ukv[layer], MLA_NOPE + MLA_V, slice(MLA_NOPE, None)).astype(BF16)
        g = norm_g[layer]

        dil = _in_proj_dil(x2d, g, w_dil, tabs_dq, tabs_dk, tm=1024, batch=b, seq=s)
        gates = _rms_matmul(x2d, g, w_gate, tm=1024, tn=512, name="in_proj_gate")
        lat = _rms_matmul(x2d, g, w_lat, tm=1024, tn=LAT_WIDTH, name="in_proj_lat")

        q, kt, v = _mla_prep(lat.reshape(b, s, LAT_WIDTH), q_norm_g[layer], kv_norm_g[layer],
                             wq, wkt, wv, tabs_mq, tabs_mk, ts=512)
        a = _mla_flash(q, kt, v, tq=512, tk=1024)

        groups = [_band_attention(dil[gi], gi, tq=512, sub=128) for gi in range(DIL_GROUPS)]

        x2d = _out_proj(x2d, a.reshape(m, MLA_WIDTH), gates, groups, w_out[layer].astype(BF16),
                        final_g, tm=512, seq=s, final=(layer == DEPTH - 1))
    return x2d.reshape(b, s, dm)
```

```python
import functools
import math

import numpy as np
import jax
import jax.numpy as jnp
from jax import lax
from jax.experimental import pallas as pl
from jax.experimental.pallas import tpu as pltpu

F32 = jnp.float32
BF16 = jnp.bfloat16

D_MODEL = 1024
DEPTH = 4
MLA_HEADS = 8
MLA_NOPE = 64
MLA_ROPE = 32
MLA_V = 64
Q_LORA = 384
KV_LORA = 256
MLA_WIDTH = MLA_HEADS * MLA_V
DIL_PAIRS = ((128, 1), (512, 4), (2048, 16))
DIL_GROUPS = 3
DIL_HEADS = 8
DIL_HD = 64
DIL_WIDTH = DIL_HEADS * DIL_HD
ROT_DIM = DIL_HD // 4
ROPE_THETA = 500000.0
EPS = 1e-6

LANES = 128
BF16_ROWS = 16
HEAD_PAD = 128
LAT_WIDTH = Q_LORA + KV_LORA + HEAD_PAD
VT_ROWS = -(-(MLA_V + 1) // BF16_ROWS) * BF16_ROWS
SUB_K = 256
LOG2E = math.log2(math.e)
NEG_BIG = -1e30
VMEM_LIMIT = 48 * 1024 * 1024


def _rope_tables(seq, dim):
    inv = 1.0 / (ROPE_THETA ** (jnp.arange(0, dim, 2, dtype=F32) / dim))
    ang = jnp.arange(seq, dtype=F32)[:, None] * inv[None, :]
    return jnp.cos(ang), jnp.sin(ang)


def _mla_tables(seq, scale):
    cos, sin = _rope_tables(seq, MLA_ROPE)
    half = MLA_ROPE // 2
    ones = jnp.ones((seq, MLA_NOPE), F32)
    z = lambda n: jnp.zeros((seq, n), F32)
    c = jnp.concatenate([ones, cos, cos, z(HEAD_PAD - MLA_NOPE - MLA_ROPE)], axis=1)
    a = jnp.concatenate([z(MLA_NOPE), -sin, z(half), z(HEAD_PAD - MLA_NOPE - MLA_ROPE)], axis=1)
    b = jnp.concatenate([z(MLA_NOPE), z(half), sin, z(HEAD_PAD - MLA_NOPE - MLA_ROPE)], axis=1)
    return jnp.stack([c, a, b]) * scale


_HALF_ROT = ROT_DIM // 2


def _pair_tile_source():
    src = []
    for lane in range(LANES):
        base, off = divmod(lane, DIL_HD)
        if off < _HALF_ROT:
            src.append((0, base * _HALF_ROT + off))
        elif off < ROT_DIM:
            src.append((1, base * _HALF_ROT + off - _HALF_ROT))
        else:
            src.append((base, off))
    return src


def _dil_column_perm():
    src = _pair_tile_source()
    perm = np.arange(3 * DIL_GROUPS * DIL_WIDTH)
    for g in range(DIL_GROUPS):
        for kind in range(2):
            base = (3 * g + kind) * DIL_WIDTH
            for pair in range(DIL_HEADS // 2):
                for lane, (h, dim) in enumerate(src):
                    perm[base + pair * LANES + lane] = base + (2 * pair + h) * DIL_HD + dim
    return perm


def _dil_tables(seq, scale, tm):
    cos, sin = _rope_tables(seq, ROT_DIM)
    z = lambda n: jnp.zeros((seq, n), F32)
    o = lambda n: jnp.ones((seq, n), F32)
    rest = DIL_HD - ROT_DIM
    c = jnp.concatenate([cos, cos, o(rest), cos, cos, o(rest)], axis=1)
    s = jnp.concatenate([-sin, -sin, z(rest), sin, sin, z(rest)], axis=1)
    tab = jnp.stack([c, s]) * scale
    out = []
    for _, d in DIL_PAIRS:
        t = tab.reshape(2, seq // tm, tm // d, d, LANES).transpose(0, 1, 3, 2, 4)
        out.append(t.reshape(2, seq, LANES))
    return jnp.stack(out)


def _rotate3(blk, tab_c, tab_a, tab_b, shift):
    return (blk * tab_c
            + pltpu.roll(blk, LANES - shift, 1) * tab_a
            + pltpu.roll(blk, shift, 1) * tab_b)


def _rms_f32(x, g):
    ms = jnp.mean(x * x, axis=-1, keepdims=True)
    return x * lax.rsqrt(ms + EPS) * g


def _rms_mm_kernel(x_ref, g_ref, w_ref, o_ref, h_scr):
    @pl.when(pl.program_id(1) == 0)
    def _():
        h_scr[...] = _rms_f32(x_ref[...], g_ref[...]).astype(BF16)

    o_ref[...] = jnp.dot(h_scr[...], w_ref[...], preferred_element_type=F32)


def _rms_matmul(x2d, g, w, *, tm, tn, name):
    m, k = x2d.shape
    n = w.shape[1]
    return pl.pallas_call(
        _rms_mm_kernel,
        grid=(m // tm, n // tn),
        in_specs=[
            pl.BlockSpec((tm, k), lambda i, j: (i, 0)),
            pl.BlockSpec((1, k), lambda i, j: (0, 0)),
            pl.BlockSpec((k, tn), lambda i, j: (0, j)),
        ],
        out_specs=pl.BlockSpec((tm, tn), lambda i, j: (i, j)),
        out_shape=jax.ShapeDtypeStruct((m, n), F32),
        scratch_shapes=[pltpu.VMEM((tm, k), BF16)],
        compiler_params=pltpu.CompilerParams(
            dimension_semantics=("arbitrary", "arbitrary"), vmem_limit_bytes=VMEM_LIMIT),
        name=name,
    )(x2d, g.reshape(1, k), w)


def _in_proj_dil_kernel(x_ref, g_ref, w_ref, tab_ref, *rest):
    out_refs = rest[:DIL_GROUPS]
    h_scr, xn_scr = rest[DIL_GROUPS:]
    tm, k = x_ref.shape
    j = pl.program_id(1)

    @pl.when(j == 0)
    def _():
        xn = _rms_f32(x_ref[...], g_ref[...])
        ktiles = [slice(c * LANES, (c + 1) * LANES) for c in range(k // LANES)]
        for c, sl in enumerate(ktiles):
            xn_scr[c] = xn[:, sl]
        for gi, (_, d) in enumerate(DIL_PAIRS):
            if d == 1:
                h_scr[gi] = xn.astype(BF16)
                continue
            rows = tm // d
            for r in range(d):
                for c, sl in enumerate(ktiles):
                    h_scr[gi, r * rows:(r + 1) * rows, sl] = (
                        xn_scr[c, pl.ds(r, rows, stride=d), :].astype(BF16))

    is_v = (j % 3) == 2
    grp = j // 3
    tab_c, tab_s = tab_ref[0, 0], tab_ref[0, 1]
    for gi, (_, d) in enumerate(DIL_PAIRS):
        @pl.when(grp == gi)
        def _(gi=gi, d=d):
            rows = tm // d
            y = jnp.dot(h_scr[gi], w_ref[...], preferred_element_type=F32)
            for c in range(y.shape[1] // LANES):
                sl = slice(c * LANES, (c + 1) * LANES)
                yc = y[:, sl]
                roped = yc * tab_c + pltpu.roll(yc, LANES // 2, 1) * tab_s
                val = jnp.where(is_v, yc, roped).astype(BF16)
                for r in range(d):
                    out_refs[gi][0, r, :, sl] = val[r * rows:(r + 1) * rows]


def _in_proj_dil(x2d, g, w, tabs, *, tm, batch, seq):
    m, k = x2d.shape
    tn = DIL_WIDTH
    nblk = seq // tm

    def out_map(gi):
        return lambda i, j: (i // nblk, 0, i % nblk, jnp.clip(j - 3 * gi, 0, 2))

    def tab_map(i, j):
        return ((j // 3) * 2 + jnp.minimum(j % 3, 1), 0, i % nblk, 0)

    out_specs, out_shape = [], []
    for gi, (_, d) in enumerate(DIL_PAIRS):
        out_specs.append(pl.BlockSpec((1, d, tm // d, tn), out_map(gi)))
        out_shape.append(jax.ShapeDtypeStruct((batch, d, seq // d, 3 * tn), BF16))
    return pl.pallas_call(
        _in_proj_dil_kernel,
        grid=(m // tm, 3 * DIL_GROUPS),
        in_specs=[
            pl.BlockSpec((tm, k), lambda i, j: (i, 0)),
            pl.BlockSpec((1, k), lambda i, j: (0, 0)),
            pl.BlockSpec((k, tn), lambda i, j: (0, j)),
            pl.BlockSpec((1, 2, tm, LANES), tab_map),
        ],
        out_specs=out_specs,
        out_shape=out_shape,
        scratch_shapes=[pltpu.VMEM((DIL_GROUPS, tm, k), BF16), pltpu.VMEM((k // LANES, tm, LANES), F32)],
        compiler_params=pltpu.CompilerParams(
            dimension_semantics=("arbitrary", "arbitrary"), vmem_limit_bytes=VMEM_LIMIT),
        name="in_proj_dil",
    )(x2d, g.reshape(1, k), w, tabs)


def _mla_prep_kernel(lat_ref, qg_ref, kvg_ref, wqt_ref, wk_ref, wvt_ref, tqt_ref, tk_ref,
                     qt_out, k_out, vt_out):
    lat = lat_ref[0]
    ts = lat.shape[0]
    cqn = _rms_f32(lat[:, :Q_LORA], qg_ref[...]).astype(BF16)
    ckvn = _rms_f32(lat[:, Q_LORA:Q_LORA + KV_LORA], kvg_ref[...]).astype(BF16)
    kr = lat[:, Q_LORA + KV_LORA:]
    nt = (((1,), (1,)), ((), ()))
    half = MLA_ROPE // 2

    qt = lax.dot_general(wqt_ref[...], cqn, nt, preferred_element_type=F32)
    qc, qa, qb = tqt_ref[0], tqt_ref[1], tqt_ref[2]
    for h in range(MLA_HEADS):
        blk = qt[h * HEAD_PAD:(h + 1) * HEAD_PAD, :]
        up = jnp.concatenate([blk[half:], blk[:half]], axis=0)
        down = jnp.concatenate([blk[-half:], blk[:-half]], axis=0)
        qt_out[0, h] = (blk * qc + up * qa + down * qb).astype(BF16)

    krp = _rotate3(kr, tk_ref[0], tk_ref[1], tk_ref[2], half)
    kk = jnp.dot(ckvn, wk_ref[...], preferred_element_type=F32)
    for h in range(MLA_HEADS):
        k_out[0, h] = (kk[:, h * HEAD_PAD:(h + 1) * HEAD_PAD] + krp).astype(BF16)

    vt = lax.dot_general(wvt_ref[...], ckvn, nt, preferred_element_type=F32)
    row = lax.broadcasted_iota(jnp.int32, (HEAD_PAD, ts), 0)
    ones_row = (row == MLA_V).astype(F32)
    for h in range(MLA_HEADS):
        vt_out[0, h] = (vt[h * HEAD_PAD:(h + 1) * HEAD_PAD, :] + ones_row).astype(BF16)


def _mla_prep(lat, qg, kvg, wqt, wk, wvt, tabs_qt, tabs_k, *, ts):
    b, s, _ = lat.shape
    hp = MLA_HEADS * HEAD_PAD
    const = lambda *shape: pl.BlockSpec(shape, lambda bi, i: (0,) * len(shape))
    rows = pl.BlockSpec((1, MLA_HEADS, ts, HEAD_PAD), lambda bi, i: (bi, 0, i, 0))
    cols = pl.BlockSpec((1, MLA_HEADS, HEAD_PAD, ts), lambda bi, i: (bi, 0, 0, i))
    rows_shape = jax.ShapeDtypeStruct((b, MLA_HEADS, s, HEAD_PAD), BF16)
    cols_shape = jax.ShapeDtypeStruct((b, MLA_HEADS, HEAD_PAD, s), BF16)
    return pl.pallas_call(
        _mla_prep_kernel,
        grid=(b, s // ts),
        in_specs=[
            pl.BlockSpec((1, ts, LAT_WIDTH), lambda bi, i: (bi, i, 0)),
            const(1, Q_LORA), const(1, KV_LORA),
            const(hp, Q_LORA), const(KV_LORA, hp), const(hp, KV_LORA),
            pl.BlockSpec((3, LANES, ts), lambda bi, i: (0, 0, i)),
            pl.BlockSpec((3, ts, LANES), lambda bi, i: (0, i, 0)),
        ],
        out_specs=[cols, rows, cols],
        out_shape=[cols_shape, rows_shape, cols_shape],
        compiler_params=pltpu.CompilerParams(
            dimension_semantics=("arbitrary", "arbitrary"), vmem_limit_bytes=VMEM_LIMIT),
        name="mla_prep",
    )(lat, qg.reshape(1, -1), kvg.reshape(1, -1), wqt, wk, wvt, tabs_qt, tabs_k)


def _flash_kernel(qt_ref, k_ref, vt_ref, o_ref, *, tk, qw):
    tq = qt_ref.shape[3]
    s_len = k_ref.shape[2]
    nsub = tk // SUB_K
    nchunk = s_len // tk
    items = [(qb, hh, c) for qb in range(tq // qw) for hh in range(2) for c in range(nchunk)]

    def scores(item, r):
        qb, hh, c = item
        rows = slice(c * tk + r * SUB_K, c * tk + (r + 1) * SUB_K)
        qt = qt_ref[0, hh, :, qb * qw:(qb + 1) * qw]
        return jnp.dot(k_ref[0, hh, rows, :], qt, preferred_element_type=F32)

    def weighted(item, r, p):
        _, hh, c = item
        cols = slice(c * tk + r * SUB_K, c * tk + (r + 1) * SUB_K)
        return jnp.dot(vt_ref[0, hh, 0:VT_ROWS, cols], p, preferred_element_type=F32)

    def fold(state, item, alpha, pv):
        key = item[:2]
        m, acc = state[key]
        state[key] = (m, alpha * acc + pv)

    state = {}
    s_parts = [scores(items[0], r) for r in range(nsub)]
    prev = None
    for t, item in enumerate(items):
        key = item[:2]
        if key not in state:
            state[key] = (jnp.full((1, qw), NEG_BIG, F32), jnp.zeros((VT_ROWS, qw), F32))
        m, acc = state[key]
        m_new = m
        for s in s_parts:
            m_new = jnp.maximum(m_new, jnp.max(s, axis=0, keepdims=True))
        alpha = jnp.exp2(m - m_new)
        state[key] = (m_new, acc)
        nxt, p_parts, pv = [], [], None
        for r in range(nsub):
            if t + 1 < len(items):
                nxt.append(scores(items[t + 1], r))
            if prev is not None:
                part = weighted(prev[0], r, prev[1][r])
                pv = part if pv is None else pv + part
            p_parts.append(jnp.exp2(s_parts[r] - m_new).astype(BF16))
        if prev is not None:
            fold(state, prev[0], prev[2], pv)
        s_parts, prev = nxt, (item, p_parts, alpha)
    pv = None
    for r in range(nsub):
        part = weighted(prev[0], r, prev[1][r])
        pv = part if pv is None else pv + part
    fold(state, prev[0], prev[2], pv)

    for qb in range(tq // qw):
        outs = []
        for hh in range(2):
            acc = state[(qb, hh)][1]
            outs.append(acc[:MLA_V] / acc[MLA_V:MLA_V + 1])
        o_ref[0, qb * qw:(qb + 1) * qw, :] = jnp.concatenate(outs, axis=0).T


def _mla_flash(qt, k, vt, *, tq, tk, qw):
    b, h, s, _ = k.shape
    return pl.pallas_call(
        functools.partial(_flash_kernel, tk=tk, qw=qw),
        grid=(b, h // 2, s // tq),
        in_specs=[
            pl.BlockSpec((1, 2, HEAD_PAD, tq), lambda bi, hp, i: (bi, hp, 0, i)),
            pl.BlockSpec((1, 2, s, HEAD_PAD), lambda bi, hp, i: (bi, hp, 0, 0)),
            pl.BlockSpec((1, 2, HEAD_PAD, s), lambda bi, hp, i: (bi, hp, 0, 0)),
        ],
        out_specs=pl.BlockSpec((1, tq, 2 * MLA_V), lambda bi, hp, i: (bi, i, hp)),
        out_shape=jax.ShapeDtypeStruct((b, s, MLA_WIDTH), F32),
        compiler_params=pltpu.CompilerParams(
            dimension_semantics=("arbitrary", "arbitrary", "arbitrary"),
            vmem_limit_bytes=VMEM_LIMIT),
        name="mla_flash",
    )(qt, k, vt)


def _band_kernel(q_ref, kp_ref, km_ref, kn_ref, vp_ref, vm_ref, vn_ref, o_ref, lse_ref,
                 kwin, vwin, *, half, length, sub):
    tq = q_ref.shape[2]
    i = pl.program_id(2)
    kwin[0:half] = kp_ref[0, 0]
    kwin[half:half + tq] = km_ref[0, 0]
    kwin[half + tq:] = kn_ref[0, 0]
    vwin[0:half] = vp_ref[0, 0]
    vwin[half:half + tq] = vm_ref[0, 0]
    vwin[half + tq:] = vn_ref[0, 0]

    wk = sub + 2 * half
    row = lax.broadcasted_iota(jnp.int32, (2 * sub, wk), 0) % sub
    col = lax.broadcasted_iota(jnp.int32, (2 * sub, wk), 1)
    rel = col - half - row
    band = (rel >= -half) & (rel <= half)
    lane = lax.broadcasted_iota(jnp.int32, (sub, LANES), 1)
    lo_v = lane < DIL_HD
    h0_q = (lane < _HALF_ROT) | ((lane >= ROT_DIM) & (lane < DIL_HD + _HALF_ROT))
    ones = jnp.ones((wk, LANES), BF16)

    def step(t, carry):
        r0 = pl.multiple_of(t * sub, sub)
        kpos = i * tq + r0 - half + col
        mask = band & (kpos >= 0) & (kpos < length)
        for c in range(DIL_WIDTH // LANES):
            cs = slice(c * LANES, (c + 1) * LANES)
            qt = q_ref[0, 0, pl.ds(r0, sub), cs]
            zero = jnp.zeros_like(qt)
            q2 = jnp.concatenate([jnp.where(h0_q, qt, zero), jnp.where(h0_q, zero, qt)], axis=0)
            kt = kwin[pl.ds(r0, wk), cs]
            vt = vwin[pl.ds(r0, wk), cs]
            s = lax.dot_general(q2, kt, (((1,), (1,)), ((), ())), preferred_element_type=F32)
            s = jnp.where(mask, s, NEG_BIG)
            m = jnp.max(s, axis=-1, keepdims=True)
            p = jnp.exp2(s - m).astype(BF16)
            ol = jnp.dot(p, jnp.concatenate([vt, ones], axis=1), preferred_element_type=F32)
            o = jnp.where(lo_v, ol[:sub, :LANES], ol[sub:, :LANES])
            l = jnp.where(lo_v, ol[:sub, LANES:], ol[sub:, LANES:])
            mm = jnp.where(lo_v, m[:sub], m[sub:])
            o_ref[0, 0, pl.ds(r0, sub), cs] = o / l
            lse_ref[0, 0, pl.ds(r0, sub), cs] = mm + jnp.log2(l)
        return carry

    lax.fori_loop(0, tq // sub, step, 0)


def _band_attention(qkv, g, *, tq, sub):
    window, d = DIL_PAIRS[g]
    half = window // (2 * d)
    b, _, length, _ = qkv.shape
    tq = min(tq, length)
    nhalf = length // half
    per = tq // half

    def col(c):
        return lambda bi, r, i: (bi, r, i, c)

    def prev(c):
        return lambda bi, r, i: (bi, r, jnp.maximum(i * per - 1, 0), c)

    def nxt(c):
        return lambda bi, r, i: (bi, r, jnp.minimum((i + 1) * per, nhalf - 1), c)

    main = lambda c: pl.BlockSpec((1, 1, tq, DIL_WIDTH), col(c))
    edge = lambda f, c: pl.BlockSpec((1, 1, half, DIL_WIDTH), f(c))
    out_spec = pl.BlockSpec((1, 1, tq, DIL_WIDTH), lambda bi, r, i: (bi, r, i, 0))
    return pl.pallas_call(
        functools.partial(_band_kernel, half=half, length=length, sub=sub),
        grid=(b, d, length // tq),
        in_specs=[main(0), edge(prev, 1), main(1), edge(nxt, 1), edge(prev, 2), main(2), edge(nxt, 2)],
        out_specs=[out_spec, out_spec],
        out_shape=[jax.ShapeDtypeStruct((b, d, length, DIL_WIDTH), F32)] * 2,
        scratch_shapes=[pltpu.VMEM((tq + 2 * half, DIL_WIDTH), BF16)] * 2,
        compiler_params=pltpu.CompilerParams(
            dimension_semantics=("arbitrary", "arbitrary", "arbitrary"),
            vmem_limit_bytes=VMEM_LIMIT),
        name=f"band_attn_d{d}",
    )(qkv, qkv, qkv, qkv, qkv, qkv, qkv)


def _silu(g):
    return g / (1.0 + jnp.exp(-g))


def _out_kernel(x_ref, a_ref, gate_ref, o1, l1, o2, l2, o3, l3, w_ref, fg_ref, y_ref, nat_scr,
                *, final):
    tm = x_ref.shape[0]

    def natural(ref, slot):
        d = ref.shape[1]
        if d == 1:
            return ref[0, 0]
        ntile = ref.shape[3] // LANES
        for r in range(d):
            for c in range(ntile):
                nat_scr[slot * ntile + c, pl.ds(r, tm // d, stride=d), :] = (
                    ref[0, r, :, c * LANES:(c + 1) * LANES])
        return jnp.concatenate([nat_scr[slot * ntile + c] for c in range(ntile)], axis=1)

    la, lb, lc = l1[0, 0], natural(l2, 0), natural(l3, 1)
    mx = jnp.maximum(jnp.maximum(la, lb), lc)
    ea, eb, ec = jnp.exp2(la - mx), jnp.exp2(lb - mx), jnp.exp2(lc - mx)
    bmix = (ea * o1[0, 0] + eb * natural(o2, 2) + ec * natural(o3, 3)) / (ea + eb + ec)
    gates = gate_ref[...]
    mix = jnp.concatenate([a_ref[...] * _silu(gates[:, :MLA_WIDTH]),
                           bmix * _silu(gates[:, MLA_WIDTH:])], axis=-1).astype(BF16)
    y = x_ref[...] + jnp.dot(mix, w_ref[...], preferred_element_type=F32)
    if final:
        y = _rms_f32(y, fg_ref[...])
    y_ref[...] = y


def _out_proj(x2d, a2d, gates, groups, w_out, final_g, *, tm, seq, final):
    m, dm = x2d.shape
    nblk = seq // tm
    row = lambda n: pl.BlockSpec((tm, n), lambda i: (i, 0))
    args = [x2d, a2d, gates]
    specs = [row(dm), row(MLA_WIDTH), row(2 * MLA_WIDTH)]
    for (_, d), (o, lse) in zip(DIL_PAIRS, groups):
        spec = pl.BlockSpec((1, d, tm // d, DIL_WIDTH), lambda i: (i // nblk, 0, i % nblk, 0))
        args += [o, lse]
        specs += [spec, spec]
    args += [w_out, final_g.reshape(1, dm)]
    specs += [pl.BlockSpec((dm, dm), lambda i: (0, 0)), pl.BlockSpec((1, dm), lambda i: (0, 0))]
    return pl.pallas_call(
        functools.partial(_out_kernel, final=final),
        grid=(m // tm,),
        in_specs=specs,
        out_specs=row(dm),
        out_shape=jax.ShapeDtypeStruct((m, dm), F32),
        scratch_shapes=[pltpu.VMEM((4 * DIL_WIDTH // LANES, tm, LANES), F32)],
        compiler_params=pltpu.CompilerParams(
            dimension_semantics=("arbitrary",), vmem_limit_bytes=VMEM_LIMIT),
        name="merge_out_proj",
    )(*args)


def _split_w_in(w):
    o_kr = Q_LORA + KV_LORA
    o_ga = o_kr + MLA_ROPE
    o_dil = o_ga + MLA_WIDTH
    o_gb = o_dil + 3 * DIL_GROUPS * DIL_WIDTH
    k = w.shape[0]
    w_dil = w[:, o_dil:o_gb][:, _dil_column_perm()]
    w_gate = jnp.concatenate([w[:, o_ga:o_dil], w[:, o_gb:]], axis=1)
    w_lat = jnp.concatenate([w[:, :o_kr], jnp.zeros((k, MLA_NOPE), w.dtype), w[:, o_kr:o_ga],
                             jnp.zeros((k, HEAD_PAD - MLA_NOPE - MLA_ROPE), w.dtype)], axis=1)
    return w_dil.astype(BF16), w_gate.astype(BF16), w_lat.astype(BF16)


def _pad_heads(w, per_head, keep):
    k = w.shape[0]
    wh = w.reshape(k, MLA_HEADS, per_head)[:, :, keep]
    wh = jnp.pad(wh, ((0, 0), (0, 0), (0, HEAD_PAD - wh.shape[2])))
    return wh.reshape(k, MLA_HEADS * HEAD_PAD)


def kernel(x, norm_g, w_in, q_norm_g, kv_norm_g, w_uq, w_ukv, w_out, final_g):
    b, s, dm = x.shape
    m = b * s
    tm_in = 1024
    mla_scale = (MLA_NOPE + MLA_ROPE) ** -0.5 * LOG2E
    dil_scale = DIL_HD ** -0.5 * LOG2E
    tabs_mqt = _mla_tables(s, mla_scale).transpose(0, 2, 1)
    tabs_mk = _mla_tables(s, 1.0)
    tabs_d = jnp.stack([_dil_tables(s, dil_scale, tm_in), _dil_tables(s, 1.0, tm_in)], axis=1)
    tabs_d = tabs_d.reshape(2 * DIL_GROUPS, 2, s, LANES)

    x2d = x.reshape(m, dm)
    for layer in range(DEPTH):
        w_dil, w_gate, w_lat = _split_w_in(w_in[layer])
        wqt = _pad_heads(w_uq[layer], MLA_NOPE + MLA_ROPE, slice(None)).astype(BF16).T
        wk = _pad_heads(w_ukv[layer], MLA_NOPE + MLA_V, slice(0, MLA_NOPE)).astype(BF16)
        wvt = _pad_heads(w_ukv[layer], MLA_NOPE + MLA_V, slice(MLA_NOPE, None)).astype(BF16).T
        g = norm_g[layer]

        dil = _in_proj_dil(x2d, g, w_dil, tabs_d, tm=tm_in, batch=b, seq=s)
        gates = _rms_matmul(x2d, g, w_gate, tm=1024, tn=512, name="in_proj_gate")
        lat = _rms_matmul(x2d, g, w_lat, tm=1024, tn=LAT_WIDTH, name="in_proj_lat")

        qt, kk, vt = _mla_prep(lat.reshape(b, s, LAT_WIDTH), q_norm_g[layer], kv_norm_g[layer],
                               wqt, wk, wvt, tabs_mqt, tabs_mk, ts=512)
        a = _mla_flash(qt, kk, vt, tq=512, tk=1024, qw=512)

        groups = [_band_attention(dil[gi], gi, tq=512, sub=128) for gi in range(DIL_GROUPS)]

        x2d = _out_proj(x2d, a.reshape(m, MLA_WIDTH), gates, groups, w_out[layer].astype(BF16),
                        final_g, tm=512, seq=s, final=(layer == DEPTH - 1))
    return x2d.reshape(b, s, dm)
```

```python
import functools
import math

import numpy as np
import jax
import jax.numpy as jnp
from jax import lax
from jax.experimental import pallas as pl
from jax.experimental.pallas import tpu as pltpu

F32 = jnp.float32
BF16 = jnp.bfloat16

D_MODEL = 1024
DEPTH = 4
MLA_HEADS = 8
MLA_NOPE = 64
MLA_ROPE = 32
MLA_V = 64
Q_LORA = 384
KV_LORA = 256
MLA_WIDTH = MLA_HEADS * MLA_V
DIL_PAIRS = ((128, 1), (512, 4), (2048, 16))
DIL_GROUPS = 3
DIL_HEADS = 8
DIL_HD = 64
DIL_WIDTH = DIL_HEADS * DIL_HD
ROT_DIM = DIL_HD // 4
ROPE_THETA = 500000.0
EPS = 1e-6

LANES = 128
BF16_ROWS = 16
HEAD_PAD = 128
LAT_WIDTH = Q_LORA + KV_LORA + HEAD_PAD
VT_ROWS = -(-(MLA_V + 1) // BF16_ROWS) * BF16_ROWS
ROW_SPLIT = 4
SUB_K = 256
LOG2E = math.log2(math.e)
NEG_BIG = -1e30
VMEM_LIMIT = 48 * 1024 * 1024


def _rope_tables(seq, dim):
    inv = 1.0 / (ROPE_THETA ** (jnp.arange(0, dim, 2, dtype=F32) / dim))
    ang = jnp.arange(seq, dtype=F32)[:, None] * inv[None, :]
    return jnp.cos(ang), jnp.sin(ang)


def _mla_tables(seq, scale):
    cos, sin = _rope_tables(seq, MLA_ROPE)
    half = MLA_ROPE // 2
    ones = jnp.ones((seq, MLA_NOPE), F32)
    z = lambda n: jnp.zeros((seq, n), F32)
    c = jnp.concatenate([ones, cos, cos, z(HEAD_PAD - MLA_NOPE - MLA_ROPE)], axis=1)
    a = jnp.concatenate([z(MLA_NOPE), -sin, z(half), z(HEAD_PAD - MLA_NOPE - MLA_ROPE)], axis=1)
    b = jnp.concatenate([z(MLA_NOPE), z(half), sin, z(HEAD_PAD - MLA_NOPE - MLA_ROPE)], axis=1)
    return jnp.stack([c, a, b]) * scale


_HALF_ROT = ROT_DIM // 2


def _pair_tile_source():
    src = []
    for lane in range(LANES):
        base, off = divmod(lane, DIL_HD)
        if off < _HALF_ROT:
            src.append((0, base * _HALF_ROT + off))
        elif off < ROT_DIM:
            src.append((1, base * _HALF_ROT + off - _HALF_ROT))
        else:
            src.append((base, off))
    return src


def _dil_column_perm():
    src = _pair_tile_source()
    perm = np.arange(3 * DIL_GROUPS * DIL_WIDTH)
    for g in range(DIL_GROUPS):
        for kind in range(2):
            base = (3 * g + kind) * DIL_WIDTH
            for pair in range(DIL_HEADS // 2):
                for lane, (h, dim) in enumerate(src):
                    perm[base + pair * LANES + lane] = base + (2 * pair + h) * DIL_HD + dim
    return perm


def _dil_tables(seq, scale, tm):
    cos, sin = _rope_tables(seq, ROT_DIM)
    z = lambda n: jnp.zeros((seq, n), F32)
    o = lambda n: jnp.ones((seq, n), F32)
    rest = DIL_HD - ROT_DIM
    c = jnp.concatenate([cos, cos, o(rest), cos, cos, o(rest)], axis=1)
    s = jnp.concatenate([-sin, -sin, z(rest), sin, sin, z(rest)], axis=1)
    tab = jnp.stack([c, s]) * scale
    out = []
    for _, d in DIL_PAIRS:
        t = tab.reshape(2, seq // tm, tm // d, d, LANES).transpose(0, 1, 3, 2, 4)
        out.append(t.reshape(2, seq, LANES))
    return jnp.stack(out)


def _rotate3(blk, tab_c, tab_a, tab_b, shift):
    return (blk * tab_c
            + pltpu.roll(blk, LANES - shift, 1) * tab_a
            + pltpu.roll(blk, shift, 1) * tab_b)


def _rms_f32(x, g):
    ms = jnp.mean(x * x, axis=-1, keepdims=True)
    return x * lax.rsqrt(ms + EPS) * g


N_DIL_TILES = 3 * DIL_GROUPS
N_PLAIN_TILES = 2


def _in_proj_kernel(x_ref, g_ref, w_ref, tab_ref, *rest):
    out_refs = rest[:DIL_GROUPS]
    gate_ref, lat_ref, h_scr, xn_scr = rest[DIL_GROUPS:]
    tm, k = x_ref.shape
    j = pl.program_id(1)

    @pl.when(j == 0)
    def _():
        xn = _rms_f32(x_ref[...], g_ref[...])
        ktiles = [slice(c * LANES, (c + 1) * LANES) for c in range(k // LANES)]
        for c, sl in enumerate(ktiles):
            xn_scr[c] = xn[:, sl]
        for gi, (_, d) in enumerate(DIL_PAIRS):
            if d == 1:
                h_scr[gi] = xn.astype(BF16)
                continue
            rows = tm // d
            for r in range(d):
                for c, sl in enumerate(ktiles):
                    h_scr[gi, r * rows:(r + 1) * rows, sl] = (
                        xn_scr[c, pl.ds(r, rows, stride=d), :].astype(BF16))

    is_v = (j % 3) == 2
    grp = j // 3
    span = tm // ROW_SPLIT
    for gi, (_, d) in enumerate(DIL_PAIRS):
        @pl.when(grp == gi)
        def _(gi=gi, d=d):
            rows = tm // d
            for part in range(ROW_SPLIT):
                base = part * span
                y = jnp.dot(h_scr[gi, base:base + span, :], w_ref[...], preferred_element_type=F32)
                tab_c = tab_ref[0, 0, base:base + span, :]
                tab_s = tab_ref[0, 1, base:base + span, :]
                for c in range(y.shape[1] // LANES):
                    sl = slice(c * LANES, (c + 1) * LANES)
                    yc = y[:, sl]
                    roped = yc * tab_c + pltpu.roll(yc, LANES // 2, 1) * tab_s
                    val = jnp.where(is_v, yc, roped).astype(BF16)
                    step = min(rows, span)
                    for off in range(0, span, step):
                        r, within = divmod(base + off, rows)
                        out_refs[gi][0, r, within:within + step, sl] = val[off:off + step]

    for first, ref in ((N_DIL_TILES, gate_ref), (N_DIL_TILES + N_PLAIN_TILES, lat_ref)):
        @pl.when((j >= first) & (j < first + N_PLAIN_TILES))
        def _(ref=ref):
            for part in range(ROW_SPLIT):
                rs = slice(part * span, (part + 1) * span)
                ref[rs, :] = jnp.dot(h_scr[0, rs, :], w_ref[...], preferred_element_type=F32)


def _in_proj(x2d, g, w, tabs, *, tm, batch, seq):
    m, k = x2d.shape
    tn = DIL_WIDTH
    nblk = seq // tm

    def out_map(gi):
        return lambda i, j: (i // nblk, 0, i % nblk, jnp.clip(j - 3 * gi, 0, 2))

    def plain_map(first):
        return lambda i, j: (i, jnp.clip(j - first, 0, N_PLAIN_TILES - 1))

    def tab_map(i, j):
        return (jnp.minimum(j // 3, DIL_GROUPS - 1), 0, i % nblk, 0)

    out_specs, out_shape = [], []
    for gi, (_, d) in enumerate(DIL_PAIRS):
        out_specs.append(pl.BlockSpec((1, d, tm // d, tn), out_map(gi)))
        out_shape.append(jax.ShapeDtypeStruct((batch, d, seq // d, 3 * tn), BF16))
    for first in (N_DIL_TILES, N_DIL_TILES + N_PLAIN_TILES):
        out_specs.append(pl.BlockSpec((tm, tn), plain_map(first)))
        out_shape.append(jax.ShapeDtypeStruct((m, N_PLAIN_TILES * tn), F32))
    return pl.pallas_call(
        _in_proj_kernel,
        grid=(m // tm, N_DIL_TILES + 2 * N_PLAIN_TILES),
        in_specs=[
            pl.BlockSpec((tm, k), lambda i, j: (i, 0)),
            pl.BlockSpec((1, k), lambda i, j: (0, 0)),
            pl.BlockSpec((k, tn), lambda i, j: (0, j)),
            pl.BlockSpec((1, 2, tm, LANES), tab_map),
        ],
        out_specs=out_specs,
        out_shape=out_shape,
        scratch_shapes=[pltpu.VMEM((DIL_GROUPS, tm, k), BF16), pltpu.VMEM((k // LANES, tm, LANES), F32)],
        compiler_params=pltpu.CompilerParams(
            dimension_semantics=("arbitrary", "arbitrary"), vmem_limit_bytes=VMEM_LIMIT),
        name="in_proj",
    )(x2d, g.reshape(1, k), w, tabs)


def _mla_prep_kernel(lat_ref, qg_ref, kvg_ref, wqt_ref, wk_ref, wvt_ref, tqt_ref, tk_ref,
                     qt_out, k_out, vt_out):
    lat = lat_ref[0]
    ts = lat.shape[0]
    cqn = _rms_f32(lat[:, :Q_LORA], qg_ref[...]).astype(BF16)
    ckvn = _rms_f32(lat[:, Q_LORA:Q_LORA + KV_LORA], kvg_ref[...]).astype(BF16)
    kr = lat[:, Q_LORA + KV_LORA:]
    nt = (((1,), (1,)), ((), ()))
    half = MLA_ROPE // 2

    qt = lax.dot_general(wqt_ref[...], cqn, nt, preferred_element_type=F32)
    qc, qa, qb = tqt_ref[0], tqt_ref[1], tqt_ref[2]
    for h in range(MLA_HEADS):
        blk = qt[h * HEAD_PAD:(h + 1) * HEAD_PAD, :]
        up = jnp.concatenate([blk[half:], blk[:half]], axis=0)
        down = jnp.concatenate([blk[-half:], blk[:-half]], axis=0)
        qt_out[0, h] = (blk * qc + up * qa + down * qb).astype(BF16)

    krp = _rotate3(kr, tk_ref[0], tk_ref[1], tk_ref[2], half)
    kk = jnp.dot(ckvn, wk_ref[...], preferred_element_type=F32)
    for h in range(MLA_HEADS):
        k_out[0, h] = (kk[:, h * HEAD_PAD:(h + 1) * HEAD_PAD] + krp).astype(BF16)

    vt = lax.dot_general(wvt_ref[...], ckvn, nt, preferred_element_type=F32)
    row = lax.broadcasted_iota(jnp.int32, (HEAD_PAD, ts), 0)
    ones_row = (row == MLA_V).astype(F32)
    for h in range(MLA_HEADS):
        vt_out[0, h] = (vt[h * HEAD_PAD:(h + 1) * HEAD_PAD, :] + ones_row).astype(BF16)


def _mla_prep(lat, qg, kvg, wqt, wk, wvt, tabs_qt, tabs_k, *, ts):
    b, s, _ = lat.shape
    hp = MLA_HEADS * HEAD_PAD
    const = lambda *shape: pl.BlockSpec(shape, lambda bi, i: (0,) * len(shape))
    rows = pl.BlockSpec((1, MLA_HEADS, ts, HEAD_PAD), lambda bi, i: (bi, 0, i, 0))
    cols = pl.BlockSpec((1, MLA_HEADS, HEAD_PAD, ts), lambda bi, i: (bi, 0, 0, i))
    rows_shape = jax.ShapeDtypeStruct((b, MLA_HEADS, s, HEAD_PAD), BF16)
    cols_shape = jax.ShapeDtypeStruct((b, MLA_HEADS, HEAD_PAD, s), BF16)
    return pl.pallas_call(
        _mla_prep_kernel,
        grid=(b, s // ts),
        in_specs=[
            pl.BlockSpec((1, ts, LAT_WIDTH), lambda bi, i: (bi, i, 0)),
            const(1, Q_LORA), const(1, KV_LORA),
            const(hp, Q_LORA), const(KV_LORA, hp), const(hp, KV_LORA),
            pl.BlockSpec((3, LANES, ts), lambda bi, i: (0, 0, i)),
            pl.BlockSpec((3, ts, LANES), lambda bi, i: (0, i, 0)),
        ],
        out_specs=[cols, rows, cols],
        out_shape=[cols_shape, rows_shape, cols_shape],
        compiler_params=pltpu.CompilerParams(
            dimension_semantics=("arbitrary", "arbitrary"), vmem_limit_bytes=VMEM_LIMIT),
        name="mla_prep",
    )(lat, qg.reshape(1, -1), kvg.reshape(1, -1), wqt, wk, wvt, tabs_qt, tabs_k)


def _flash_kernel(qt_ref, k_ref, vt_ref, o_ref, *, tk, qw):
    tq = qt_ref.shape[3]
    s_len = k_ref.shape[2]
    nsub = tk // SUB_K
    nchunk = s_len // tk
    items = [(qb, hh, c) for qb in range(tq // qw) for hh in range(2) for c in range(nchunk)]

    def scores(item, r):
        qb, hh, c = item
        rows = slice(c * tk + r * SUB_K, c * tk + (r + 1) * SUB_K)
        qt = qt_ref[0, hh, :, qb * qw:(qb + 1) * qw]
        return jnp.dot(k_ref[0, hh, rows, :], qt, preferred_element_type=F32)

    def weighted(item, r, p):
        _, hh, c = item
        cols = slice(c * tk + r * SUB_K, c * tk + (r + 1) * SUB_K)
        return jnp.dot(vt_ref[0, hh, 0:VT_ROWS, cols], p, preferred_element_type=F32)

    def fold(state, item, alpha, pv):
        key = item[:2]
        m, acc = state[key]
        state[key] = (m, alpha * acc + pv)

    state = {}
    s_parts = [scores(items[0], r) for r in range(nsub)]
    prev = None
    for t, item in enumerate(items):
        key = item[:2]
        if key not in state:
            state[key] = (jnp.full((1, qw), NEG_BIG, F32), jnp.zeros((VT_ROWS, qw), F32))
        m, acc = state[key]
        m_new = m
        for s in s_parts:
            m_new = jnp.maximum(m_new, jnp.max(s, axis=0, keepdims=True))
        alpha = jnp.exp2(m - m_new)
        state[key] = (m_new, acc)
        nxt, p_parts, pv = [], [], None
        for r in range(nsub):
            if t + 1 < len(items):
                nxt.append(scores(items[t + 1], r))
            if prev is not None:
                part = weighted(prev[0], r, prev[1][r])
                pv = part if pv is None else pv + part
            p_parts.append(jnp.exp2(s_parts[r] - m_new).astype(BF16))
        if prev is not None:
            fold(state, prev[0], prev[2], pv)
        s_parts, prev = nxt, (item, p_parts, alpha)
    pv = None
    for r in range(nsub):
        part = weighted(prev[0], r, prev[1][r])
        pv = part if pv is None else pv + part
    fold(state, prev[0], prev[2], pv)

    for qb in range(tq // qw):
        outs = []
        for hh in range(2):
            acc = state[(qb, hh)][1]
            outs.append(acc[:MLA_V] / acc[MLA_V:MLA_V + 1])
        o_ref[0, qb * qw:(qb + 1) * qw, :] = jnp.concatenate(outs, axis=0).T.astype(o_ref.dtype)


def _mla_flash(qt, k, vt, *, tq, tk, qw):
    b, h, s, _ = k.shape
    return pl.pallas_call(
        functools.partial(_flash_kernel, tk=tk, qw=qw),
        grid=(b, h // 2, s // tq),
        in_specs=[
            pl.BlockSpec((1, 2, HEAD_PAD, tq), lambda bi, hp, i: (bi, hp, 0, i)),
            pl.BlockSpec((1, 2, s, HEAD_PAD), lambda bi, hp, i: (bi, hp, 0, 0)),
            pl.BlockSpec((1, 2, HEAD_PAD, s), lambda bi, hp, i: (bi, hp, 0, 0)),
        ],
        out_specs=pl.BlockSpec((1, tq, 2 * MLA_V), lambda bi, hp, i: (bi, i, hp)),
        out_shape=jax.ShapeDtypeStruct((b, s, MLA_WIDTH), BF16),
        compiler_params=pltpu.CompilerParams(
            dimension_semantics=("arbitrary", "arbitrary", "arbitrary"),
            vmem_limit_bytes=VMEM_LIMIT),
        name="mla_flash",
    )(qt, k, vt)


def _band_kernel(q_ref, kp_ref, km_ref, kn_ref, vp_ref, vm_ref, vn_ref, o_ref, lse_ref,
                 kwin, vwin, *, half, length, sub):
    tq = q_ref.shape[2]
    i = pl.program_id(2)
    kwin[0:half] = kp_ref[0, 0]
    kwin[half:half + tq] = km_ref[0, 0]
    kwin[half + tq:] = kn_ref[0, 0]
    vwin[0:half] = vp_ref[0, 0]
    vwin[half:half + tq] = vm_ref[0, 0]
    vwin[half + tq:] = vn_ref[0, 0]

    wk = sub + 2 * half
    row = lax.broadcasted_iota(jnp.int32, (2 * sub, wk), 0) % sub
    col = lax.broadcasted_iota(jnp.int32, (2 * sub, wk), 1)
    rel = col - half - row
    band = (rel >= -half) & (rel <= half)
    lane = lax.broadcasted_iota(jnp.int32, (sub, LANES), 1)
    lo_v = lane < DIL_HD
    h0_q = (lane < _HALF_ROT) | ((lane >= ROT_DIM) & (lane < DIL_HD + _HALF_ROT))
    ones = jnp.ones((wk, LANES), BF16)

    def step(t, carry):
        r0 = pl.multiple_of(t * sub, sub)
        kpos = i * tq + r0 - half + col
        mask = band & (kpos >= 0) & (kpos < length)
        for c in range(DIL_WIDTH // LANES):
            cs = slice(c * LANES, (c + 1) * LANES)
            qt = q_ref[0, 0, pl.ds(r0, sub), cs]
            zero = jnp.zeros_like(qt)
            q2 = jnp.concatenate([jnp.where(h0_q, qt, zero), jnp.where(h0_q, zero, qt)], axis=0)
            kt = kwin[pl.ds(r0, wk), cs]
            vt = vwin[pl.ds(r0, wk), cs]
            s = lax.dot_general(q2, kt, (((1,), (1,)), ((), ())), preferred_element_type=F32)
            s = jnp.where(mask, s, NEG_BIG)
            m = jnp.max(s, axis=-1, keepdims=True)
            p = jnp.exp2(s - m).astype(BF16)
            ol = jnp.dot(p, jnp.concatenate([vt, ones], axis=1), preferred_element_type=F32)
            o = jnp.where(lo_v, ol[:sub, :LANES], ol[sub:, :LANES])
            l = jnp.where(lo_v, ol[:sub, LANES:], ol[sub:, LANES:])
            mm = jnp.where(lo_v, m[:sub], m[sub:])
            o_ref[0, 0, pl.ds(r0, sub), cs] = (o / l).astype(o_ref.dtype)
            lse_ref[0, 0, pl.ds(r0, sub), cs] = mm + jnp.log2(l)
        return carry

    lax.fori_loop(0, tq // sub, step, 0, unroll=True)


def _band_attention(qkv, g, *, tq, sub):
    window, d = DIL_PAIRS[g]
    half = window // (2 * d)
    b, _, length, _ = qkv.shape
    tq = min(tq, length)
    nhalf = length // half
    per = tq // half

    def col(c):
        return lambda bi, r, i: (bi, r, i, c)

    def prev(c):
        return lambda bi, r, i: (bi, r, jnp.maximum(i * per - 1, 0), c)

    def nxt(c):
        return lambda bi, r, i: (bi, r, jnp.minimum((i + 1) * per, nhalf - 1), c)

    main = lambda c: pl.BlockSpec((1, 1, tq, DIL_WIDTH), col(c))
    edge = lambda f, c: pl.BlockSpec((1, 1, half, DIL_WIDTH), f(c))
    out_spec = pl.BlockSpec((1, 1, tq, DIL_WIDTH), lambda bi, r, i: (bi, r, i, 0))
    return pl.pallas_call(
        functools.partial(_band_kernel, half=half, length=length, sub=sub),
        grid=(b, d, length // tq),
        in_specs=[main(0), edge(prev, 1), main(1), edge(nxt, 1), edge(prev, 2), main(2), edge(nxt, 2)],
        out_specs=[out_spec, out_spec],
        out_shape=[jax.ShapeDtypeStruct((b, d, length, DIL_WIDTH), BF16),
                   jax.ShapeDtypeStruct((b, d, length, DIL_WIDTH), F32)],
        scratch_shapes=[pltpu.VMEM((tq + 2 * half, DIL_WIDTH), BF16)] * 2,
        compiler_params=pltpu.CompilerParams(
            dimension_semantics=("arbitrary", "arbitrary", "arbitrary"),
            vmem_limit_bytes=VMEM_LIMIT),
        name=f"band_attn_d{d}",
    )(qkv, qkv, qkv, qkv, qkv, qkv, qkv)


def _silu(g):
    return g / (1.0 + jnp.exp(-g))


def _out_kernel(x_ref, a_ref, gate_ref, o1, l1, o2, l2, o3, l3, w_ref, fg_ref, y_ref, nat_scr,
                *, final):
    tm = x_ref.shape[0]

    def natural(ref, slot):
        d = ref.shape[1]
        if d == 1:
            return ref[0, 0]
        ntile = ref.shape[3] // LANES
        for r in range(d):
            for c in range(ntile):
                nat_scr[slot * ntile + c, pl.ds(r, tm // d, stride=d), :] = (
                    ref[0, r, :, c * LANES:(c + 1) * LANES].astype(F32))
        return jnp.concatenate([nat_scr[slot * ntile + c] for c in range(ntile)], axis=1)

    la, lb, lc = l1[0, 0], natural(l2, 0), natural(l3, 1)
    mx = jnp.maximum(jnp.maximum(la, lb), lc)
    ea, eb, ec = jnp.exp2(la - mx), jnp.exp2(lb - mx), jnp.exp2(lc - mx)
    bmix = (ea * o1[0, 0] + eb * natural(o2, 2) + ec * natural(o3, 3)) / (ea + eb + ec)
    gates = gate_ref[...]
    mix = jnp.concatenate([a_ref[...] * _silu(gates[:, :MLA_WIDTH]),
                           bmix * _silu(gates[:, MLA_WIDTH:])], axis=-1).astype(BF16)
    y = x_ref[...] + jnp.dot(mix, w_ref[...], preferred_element_type=F32)
    if final:
        y = _rms_f32(y, fg_ref[...])
    y_ref[...] = y


def _out_proj(x2d, a2d, gates, groups, w_out, final_g, *, tm, seq, final):
    m, dm = x2d.shape
    nblk = seq // tm
    row = lambda n: pl.BlockSpec((tm, n), lambda i: (i, 0))
    args = [x2d, a2d, gates]
    specs = [row(dm), row(MLA_WIDTH), row(2 * MLA_WIDTH)]
    for (_, d), (o, lse) in zip(DIL_PAIRS, groups):
        spec = pl.BlockSpec((1, d, tm // d, DIL_WIDTH), lambda i: (i // nblk, 0, i % nblk, 0))
        args += [o, lse]
        specs += [spec, spec]
    args += [w_out, final_g.reshape(1, dm)]
    specs += [pl.BlockSpec((dm, dm), lambda i: (0, 0)), pl.BlockSpec((1, dm), lambda i: (0, 0))]
    return pl.pallas_call(
        functools.partial(_out_kernel, final=final),
        grid=(m // tm,),
        in_specs=specs,
        out_specs=row(dm),
        out_shape=jax.ShapeDtypeStruct((m, dm), F32),
        scratch_shapes=[pltpu.VMEM((4 * DIL_WIDTH // LANES, tm, LANES), F32)],
        compiler_params=pltpu.CompilerParams(
            dimension_semantics=("arbitrary",), vmem_limit_bytes=VMEM_LIMIT),
        name="merge_out_proj",
    )(*args)


def _arrange_w_in(w):
    o_kr = Q_LORA + KV_LORA
    o_ga = o_kr + MLA_ROPE
    o_dil = o_ga + MLA_WIDTH
    o_gb = o_dil + 3 * DIL_GROUPS * DIL_WIDTH
    k = w.shape[0]
    z = lambda n: jnp.zeros((k, n), w.dtype)
    w_dil = w[:, o_dil:o_gb][:, _dil_column_perm()]
    w_gate = jnp.concatenate([w[:, o_ga:o_dil], w[:, o_gb:]], axis=1)
    w_lat = jnp.concatenate([w[:, :o_kr], z(MLA_NOPE), w[:, o_kr:o_ga],
                             z(HEAD_PAD - MLA_NOPE - MLA_ROPE),
                             z(N_PLAIN_TILES * DIL_WIDTH - LAT_WIDTH)], axis=1)
    return jnp.concatenate([w_dil, w_gate, w_lat], axis=1).astype(BF16)


def _pad_heads(w, per_head, keep):
    k = w.shape[0]
    wh = w.reshape(k, MLA_HEADS, per_head)[:, :, keep]
    wh = jnp.pad(wh, ((0, 0), (0, 0), (0, HEAD_PAD - wh.shape[2])))
    return wh.reshape(k, MLA_HEADS * HEAD_PAD)


def kernel(x, norm_g, w_in, q_norm_g, kv_norm_g, w_uq, w_ukv, w_out, final_g):
    b, s, dm = x.shape
    m = b * s
    tm_in = 1024
    mla_scale = (MLA_NOPE + MLA_ROPE) ** -0.5 * LOG2E
    dil_scale = DIL_HD ** -0.5 * LOG2E
    tabs_mqt = _mla_tables(s, mla_scale).transpose(0, 2, 1)
    tabs_mk = _mla_tables(s, 1.0)
    tabs_d = _dil_tables(s, math.sqrt(dil_scale), tm_in)

    x2d = x.reshape(m, dm)
    for layer in range(DEPTH):
        wqt = _pad_heads(w_uq[layer], MLA_NOPE + MLA_ROPE, slice(None)).astype(BF16).T
        wk = _pad_heads(w_ukv[layer], MLA_NOPE + MLA_V, slice(0, MLA_NOPE)).astype(BF16)
        wvt = _pad_heads(w_ukv[layer], MLA_NOPE + MLA_V, slice(MLA_NOPE, None)).astype(BF16).T

        *dil, gates, lat = _in_proj(x2d, norm_g[layer], _arrange_w_in(w_in[layer]), tabs_d,
                                    tm=tm_in, batch=b, seq=s)

        qt, kk, vt = _mla_prep(lat.reshape(b, s, -1), q_norm_g[layer], kv_norm_g[layer],
                               wqt, wk, wvt, tabs_mqt, tabs_mk, ts=512)
        a = _mla_flash(qt, kk, vt, tq=512, tk=2048, qw=512)

        groups = [_band_attention(dil[gi], gi, tq=512, sub=128) for gi in range(DIL_GROUPS)]

        x2d = _out_proj(x2d, a.reshape(m, MLA_WIDTH), gates, groups, w_out[layer].astype(BF16),
                        final_g, tm=512, seq=s, final=(layer == DEPTH - 1))
    return x2d.reshape(b, s, dm)
```

```python
import functools
import math

import numpy as np
import jax
import jax.numpy as jnp
from jax import lax
from jax.experimental import pallas as pl
from jax.experimental.pallas import tpu as pltpu

F32 = jnp.float32
BF16 = jnp.bfloat16

D_MODEL = 1024
DEPTH = 4
MLA_HEADS = 8
MLA_NOPE = 64
MLA_ROPE = 32
MLA_V = 64
Q_LORA = 384
KV_LORA = 256
MLA_WIDTH = MLA_HEADS * MLA_V
DIL_PAIRS = ((128, 1), (512, 4), (2048, 16))
DIL_GROUPS = 3
DIL_HEADS = 8
DIL_HD = 64
DIL_WIDTH = DIL_HEADS * DIL_HD
ROT_DIM = DIL_HD // 4
ROPE_THETA = 500000.0
EPS = 1e-6

LANES = 128
BF16_ROWS = 16
HEAD_PAD = 128
LAT_WIDTH = Q_LORA + KV_LORA + HEAD_PAD
VT_ROWS = -(-(MLA_V + 1) // BF16_ROWS) * BF16_ROWS
ROW_SPLIT = 4
SUB_K = 256
LOG2E = math.log2(math.e)
NEG_BIG = -1e30
VMEM_LIMIT = 48 * 1024 * 1024
VMEM_LIMIT_IN_PROJ = 56 * 1024 * 1024


def _rope_tables(seq, dim):
    inv = 1.0 / (ROPE_THETA ** (jnp.arange(0, dim, 2, dtype=F32) / dim))
    ang = jnp.arange(seq, dtype=F32)[:, None] * inv[None, :]
    return jnp.cos(ang), jnp.sin(ang)


def _mla_tables(seq, scale):
    cos, sin = _rope_tables(seq, MLA_ROPE)
    half = MLA_ROPE // 2
    ones = jnp.ones((seq, MLA_NOPE), F32)
    z = lambda n: jnp.zeros((seq, n), F32)
    c = jnp.concatenate([ones, cos, cos, z(HEAD_PAD - MLA_NOPE - MLA_ROPE)], axis=1)
    a = jnp.concatenate([z(MLA_NOPE), -sin, z(half), z(HEAD_PAD - MLA_NOPE - MLA_ROPE)], axis=1)
    b = jnp.concatenate([z(MLA_NOPE), z(half), sin, z(HEAD_PAD - MLA_NOPE - MLA_ROPE)], axis=1)
    return jnp.stack([c, a, b]) * scale


_HALF_ROT = ROT_DIM // 2


def _pair_tile_source():
    src = []
    for lane in range(LANES):
        base, off = divmod(lane, DIL_HD)
        if off < _HALF_ROT:
            src.append((0, base * _HALF_ROT + off))
        elif off < ROT_DIM:
            src.append((1, base * _HALF_ROT + off - _HALF_ROT))
        else:
            src.append((base, off))
    return src


def _dil_column_perm():
    src = _pair_tile_source()
    perm = np.arange(3 * DIL_GROUPS * DIL_WIDTH)
    for g in range(DIL_GROUPS):
        for kind in range(2):
            base = (3 * g + kind) * DIL_WIDTH
            for pair in range(DIL_HEADS // 2):
                for lane, (h, dim) in enumerate(src):
                    perm[base + pair * LANES + lane] = base + (2 * pair + h) * DIL_HD + dim
    return perm


def _dil_tables(seq, scale, tm):
    cos, sin = _rope_tables(seq, ROT_DIM)
    z = lambda n: jnp.zeros((seq, n), F32)
    o = lambda n: jnp.ones((seq, n), F32)
    rest = DIL_HD - ROT_DIM
    c = jnp.concatenate([cos, cos, o(rest), cos, cos, o(rest)], axis=1)
    s = jnp.concatenate([-sin, -sin, z(rest), sin, sin, z(rest)], axis=1)
    tab = jnp.stack([c, s]) * scale
    out = []
    for _, d in DIL_PAIRS:
        t = tab.reshape(2, seq // tm, tm // d, d, LANES).transpose(0, 1, 3, 2, 4)
        out.append(t.reshape(2, seq, LANES))
    return jnp.stack(out)


def _rotate3(blk, tab_c, tab_a, tab_b, shift):
    return (blk * tab_c
            + pltpu.roll(blk, LANES - shift, 1) * tab_a
            + pltpu.roll(blk, shift, 1) * tab_b)


def _rms_f32(x, g):
    ms = jnp.mean(x * x, axis=-1, keepdims=True)
    return x * lax.rsqrt(ms + EPS) * g


N_DIL_TILES = 3 * DIL_GROUPS
N_PLAIN_TILES = 2


def _in_proj_kernel(x_ref, g_ref, w_ref, tab_ref, *rest):
    out_refs = rest[:DIL_GROUPS]
    gate_ref, lat_ref, h_scr, xn_scr = rest[DIL_GROUPS:]
    tm, k = x_ref.shape
    j = pl.program_id(1)

    @pl.when(j == 0)
    def _():
        xn = _rms_f32(x_ref[...], g_ref[...])
        ktiles = [slice(c * LANES, (c + 1) * LANES) for c in range(k // LANES)]
        for c, sl in enumerate(ktiles):
            xn_scr[c] = xn[:, sl]
        for gi, (_, d) in enumerate(DIL_PAIRS):
            if d == 1:
                h_scr[gi] = xn.astype(BF16)
                continue
            rows = tm // d
            for r in range(d):
                for c, sl in enumerate(ktiles):
                    h_scr[gi, r * rows:(r + 1) * rows, sl] = (
                        xn_scr[c, pl.ds(r, rows, stride=d), :].astype(BF16))

    is_v = (j % 3) == 2
    grp = j // 3
    span = tm // ROW_SPLIT
    w_cols = pl.ds(pl.multiple_of(j * DIL_WIDTH, DIL_WIDTH), DIL_WIDTH)
    for gi, (_, d) in enumerate(DIL_PAIRS):
        @pl.when(grp == gi)
        def _(gi=gi, d=d):
            rows = tm // d
            for part in range(ROW_SPLIT):
                base = part * span
                y = jnp.dot(h_scr[gi, base:base + span, :], w_ref[:, w_cols], preferred_element_type=F32)
                tab_c = tab_ref[0, 0, base:base + span, :]
                tab_s = tab_ref[0, 1, base:base + span, :]
                for c in range(y.shape[1] // LANES):
                    sl = slice(c * LANES, (c + 1) * LANES)
                    yc = y[:, sl]
                    roped = yc * tab_c + pltpu.roll(yc, LANES // 2, 1) * tab_s
                    val = jnp.where(is_v, yc, roped).astype(BF16)
                    step = min(rows, span)
                    for off in range(0, span, step):
                        r, within = divmod(base + off, rows)
                        out_refs[gi][0, r, within:within + step, sl] = val[off:off + step]

    for first, ref in ((N_DIL_TILES, gate_ref), (N_DIL_TILES + N_PLAIN_TILES, lat_ref)):
        @pl.when((j >= first) & (j < first + N_PLAIN_TILES))
        def _(ref=ref):
            for part in range(ROW_SPLIT):
                rs = slice(part * span, (part + 1) * span)
                ref[rs, :] = jnp.dot(h_scr[0, rs, :], w_ref[:, w_cols], preferred_element_type=F32)


def _in_proj(x2d, g, w_all, layer, tabs, *, tm, batch, seq):
    m, k = x2d.shape
    tn = DIL_WIDTH
    nblk = seq // tm

    def out_map(gi):
        return lambda i, j: (i // nblk, 0, i % nblk, jnp.clip(j - 3 * gi, 0, 2))

    def plain_map(first):
        return lambda i, j: (i, jnp.clip(j - first, 0, N_PLAIN_TILES - 1))

    def tab_map(i, j):
        return (jnp.minimum(j // 3, DIL_GROUPS - 1), 0, i % nblk, 0)

    out_specs, out_shape = [], []
    for gi, (_, d) in enumerate(DIL_PAIRS):
        out_specs.append(pl.BlockSpec((1, d, tm // d, tn), out_map(gi)))
        out_shape.append(jax.ShapeDtypeStruct((batch, d, seq // d, 3 * tn), BF16))
    for first in (N_DIL_TILES, N_DIL_TILES + N_PLAIN_TILES):
        out_specs.append(pl.BlockSpec((tm, tn), plain_map(first)))
        out_shape.append(jax.ShapeDtypeStruct((m, N_PLAIN_TILES * tn), F32))
    return pl.pallas_call(
        _in_proj_kernel,
        grid=(m // tm, N_DIL_TILES + 2 * N_PLAIN_TILES),
        in_specs=[
            pl.BlockSpec((tm, k), lambda i, j: (i, 0)),
            pl.BlockSpec((1, k), lambda i, j: (0, 0)),
            pl.BlockSpec((None, k, w_all.shape[2]), lambda i, j: (layer, 0, 0),
                         pipeline_mode=pl.Buffered(1)),
            pl.BlockSpec((1, 2, tm, LANES), tab_map),
        ],
        out_specs=out_specs,
        out_shape=out_shape,
        scratch_shapes=[pltpu.VMEM((DIL_GROUPS, tm, k), BF16), pltpu.VMEM((k // LANES, tm, LANES), F32)],
        compiler_params=pltpu.CompilerParams(
            dimension_semantics=("arbitrary", "arbitrary"), vmem_limit_bytes=VMEM_LIMIT_IN_PROJ),
        name="in_proj",
    )(x2d, g.reshape(1, k), w_all, tabs)


def _mla_prep_kernel(lat_ref, qg_ref, kvg_ref, wqt_ref, wk_ref, wvt_ref, tqt_ref, tk_ref,
                     qt_out, k_out, vt_out):
    lat = lat_ref[0]
    ts = lat.shape[0]
    cqn = _rms_f32(lat[:, :Q_LORA], qg_ref[...]).astype(BF16)
    ckvn = _rms_f32(lat[:, Q_LORA:Q_LORA + KV_LORA], kvg_ref[...]).astype(BF16)
    kr = lat[:, Q_LORA + KV_LORA:]
    nt = (((1,), (1,)), ((), ()))
    half = MLA_ROPE // 2

    qt = lax.dot_general(wqt_ref[...], cqn, nt, preferred_element_type=F32)
    qc, qa, qb = tqt_ref[0], tqt_ref[1], tqt_ref[2]
    for h in range(MLA_HEADS):
        blk = qt[h * HEAD_PAD:(h + 1) * HEAD_PAD, :]
        up = jnp.concatenate([blk[half:], blk[:half]], axis=0)
        down = jnp.concatenate([blk[-half:], blk[:-half]], axis=0)
        qt_out[0, h] = (blk * qc + up * qa + down * qb).astype(BF16)

    krp = _rotate3(kr, tk_ref[0], tk_ref[1], tk_ref[2], half)
    kk = jnp.dot(ckvn, wk_ref[...], preferred_element_type=F32)
    for h in range(MLA_HEADS):
        k_out[0, h] = (kk[:, h * HEAD_PAD:(h + 1) * HEAD_PAD] + krp).astype(BF16)

    vt = lax.dot_general(wvt_ref[...], ckvn, nt, preferred_element_type=F32)
    row = lax.broadcasted_iota(jnp.int32, (HEAD_PAD, ts), 0)
    ones_row = (row == MLA_V).astype(F32)
    for h in range(MLA_HEADS):
        vt_out[0, h] = (vt[h * HEAD_PAD:(h + 1) * HEAD_PAD, :] + ones_row).astype(BF16)


def _mla_prep(lat, qg, kvg, wqt, wk, wvt, tabs_qt, tabs_k, *, ts):
    b, s, _ = lat.shape
    hp = MLA_HEADS * HEAD_PAD
    const = lambda *shape: pl.BlockSpec(shape, lambda bi, i: (0,) * len(shape))
    rows = pl.BlockSpec((1, MLA_HEADS, ts, HEAD_PAD), lambda bi, i: (bi, 0, i, 0))
    cols = pl.BlockSpec((1, MLA_HEADS, HEAD_PAD, ts), lambda bi, i: (bi, 0, 0, i))
    rows_shape = jax.ShapeDtypeStruct((b, MLA_HEADS, s, HEAD_PAD), BF16)
    cols_shape = jax.ShapeDtypeStruct((b, MLA_HEADS, HEAD_PAD, s), BF16)
    return pl.pallas_call(
        _mla_prep_kernel,
        grid=(b, s // ts),
        in_specs=[
            pl.BlockSpec((1, ts, LAT_WIDTH), lambda bi, i: (bi, i, 0)),
            const(1, Q_LORA), const(1, KV_LORA),
            const(hp, Q_LORA), const(KV_LORA, hp), const(hp, KV_LORA),
            pl.BlockSpec((3, LANES, ts), lambda bi, i: (0, 0, i)),
            pl.BlockSpec((3, ts, LANES), lambda bi, i: (0, i, 0)),
        ],
        out_specs=[cols, rows, cols],
        out_shape=[cols_shape, rows_shape, cols_shape],
        compiler_params=pltpu.CompilerParams(
            dimension_semantics=("arbitrary", "arbitrary"), vmem_limit_bytes=VMEM_LIMIT),
        name="mla_prep",
    )(lat, qg.reshape(1, -1), kvg.reshape(1, -1), wqt, wk, wvt, tabs_qt, tabs_k)


def _flash_kernel(qt_ref, k_ref, vt_ref, o_ref, *, tk, qw):
    tq = qt_ref.shape[3]
    s_len = k_ref.shape[2]
    nsub = tk // SUB_K
    nchunk = s_len // tk
    items = [(qb, hh, c) for qb in range(tq // qw) for hh in range(2) for c in range(nchunk)]

    def scores(item, r):
        qb, hh, c = item
        rows = slice(c * tk + r * SUB_K, c * tk + (r + 1) * SUB_K)
        qt = qt_ref[0, hh, :, qb * qw:(qb + 1) * qw]
        return jnp.dot(k_ref[0, hh, rows, :], qt, preferred_element_type=F32)

    def weighted(item, r, p):
        _, hh, c = item
        cols = slice(c * tk + r * SUB_K, c * tk + (r + 1) * SUB_K)
        return jnp.dot(vt_ref[0, hh, 0:VT_ROWS, cols], p, preferred_element_type=F32)

    def fold(state, item, alpha, pv):
        key = item[:2]
        m, acc = state[key]
        state[key] = (m, alpha * acc + pv)

    state = {}
    s_parts = [scores(items[0], r) for r in range(nsub)]
    prev = None
    for t, item in enumerate(items):
        key = item[:2]
        if key not in state:
            state[key] = (jnp.full((1, qw), NEG_BIG, F32), jnp.zeros((VT_ROWS, qw), F32))
        m, acc = state[key]
        m_new = m
        for s in s_parts:
            m_new = jnp.maximum(m_new, jnp.max(s, axis=0, keepdims=True))
        alpha = jnp.exp2(m - m_new)
        state[key] = (m_new, acc)
        nxt, p_parts, pv = [], [], None
        for r in range(nsub):
            if t + 1 < len(items):
                nxt.append(scores(items[t + 1], r))
            if prev is not None:
                part = weighted(prev[0], r, prev[1][r])
                pv = part if pv is None else pv + part
            p_parts.append(jnp.exp2(s_parts[r] - m_new).astype(BF16))
        if prev is not None:
            fold(state, prev[0], prev[2], pv)
        s_parts, prev = nxt, (item, p_parts, alpha)
    pv = None
    for r in range(nsub):
        part = weighted(prev[0], r, prev[1][r])
        pv = part if pv is None else pv + part
    fold(state, prev[0], prev[2], pv)

    for qb in range(tq // qw):
        outs = []
        for hh in range(2):
            acc = state[(qb, hh)][1]
            outs.append(acc[:MLA_V] / acc[MLA_V:MLA_V + 1])
        o_ref[0, qb * qw:(qb + 1) * qw, :] = jnp.concatenate(outs, axis=0).T.astype(o_ref.dtype)


def _mla_flash(qt, k, vt, *, tq, tk, qw):
    b, h, s, _ = k.shape
    return pl.pallas_call(
        functools.partial(_flash_kernel, tk=tk, qw=qw),
        grid=(b, h // 2, s // tq),
        in_specs=[
            pl.BlockSpec((1, 2, HEAD_PAD, tq), lambda bi, hp, i: (bi, hp, 0, i)),
            pl.BlockSpec((1, 2, s, HEAD_PAD), lambda bi, hp, i: (bi, hp, 0, 0)),
            pl.BlockSpec((1, 2, HEAD_PAD, s), lambda bi, hp, i: (bi, hp, 0, 0)),
        ],
        out_specs=pl.BlockSpec((1, tq, 2 * MLA_V), lambda bi, hp, i: (bi, i, hp)),
        out_shape=jax.ShapeDtypeStruct((b, s, MLA_WIDTH), BF16),
        compiler_params=pltpu.CompilerParams(
            dimension_semantics=("arbitrary", "arbitrary", "arbitrary"),
            vmem_limit_bytes=VMEM_LIMIT),
        name="mla_flash",
    )(qt, k, vt)


def _band_kernel(q_ref, kp_ref, km_ref, kn_ref, vp_ref, vm_ref, vn_ref, o_ref, lse_ref,
                 kwin, vwin, *, half, length, sub):
    tq = q_ref.shape[2]
    i = pl.program_id(2)
    kwin[0:half] = kp_ref[0, 0]
    kwin[half:half + tq] = km_ref[0, 0]
    kwin[half + tq:] = kn_ref[0, 0]
    vwin[0:half] = vp_ref[0, 0]
    vwin[half:half + tq] = vm_ref[0, 0]
    vwin[half + tq:] = vn_ref[0, 0]

    wk = sub + 2 * half
    row = lax.broadcasted_iota(jnp.int32, (2 * sub, wk), 0) % sub
    col = lax.broadcasted_iota(jnp.int32, (2 * sub, wk), 1)
    rel = col - half - row
    band = (rel >= -half) & (rel <= half)
    lane = lax.broadcasted_iota(jnp.int32, (sub, LANES), 1)
    lo_v = lane < DIL_HD
    h0_q = (lane < _HALF_ROT) | ((lane >= ROT_DIM) & (lane < DIL_HD + _HALF_ROT))
    ones = jnp.ones((wk, LANES), BF16)

    def step(t, carry):
        r0 = pl.multiple_of(t * sub, sub)
        kpos = i * tq + r0 - half + col
        mask = band & (kpos >= 0) & (kpos < length)
        for c in range(DIL_WIDTH // LANES):
            cs = slice(c * LANES, (c + 1) * LANES)
            qt = q_ref[0, 0, pl.ds(r0, sub), cs]
            zero = jnp.zeros_like(qt)
            q2 = jnp.concatenate([jnp.where(h0_q, qt, zero), jnp.where(h0_q, zero, qt)], axis=0)
            kt = kwin[pl.ds(r0, wk), cs]
            vt = vwin[pl.ds(r0, wk), cs]
            s = lax.dot_general(q2, kt, (((1,), (1,)), ((), ())), preferred_element_type=F32)
            s = jnp.where(mask, s, NEG_BIG)
            m = jnp.max(s, axis=-1, keepdims=True)
            p = jnp.exp2(s - m).astype(BF16)
            ol = jnp.dot(p, jnp.concatenate([vt, ones], axis=1), preferred_element_type=F32)
            o = jnp.where(lo_v, ol[:sub, :LANES], ol[sub:, :LANES])
            l = jnp.where(lo_v, ol[:sub, LANES:], ol[sub:, LANES:])
            mm = jnp.where(lo_v, m[:sub], m[sub:])
            o_ref[0, 0, pl.ds(r0, sub), cs] = (o / l).astype(o_ref.dtype)
            lse_ref[0, 0, pl.ds(r0, sub), cs] = mm + jnp.log2(l)
        return carry

    lax.fori_loop(0, tq // sub, step, 0, unroll=True)


def _band_attention(qkv, g, *, tq, sub):
    window, d = DIL_PAIRS[g]
    half = window // (2 * d)
    b, _, length, _ = qkv.shape
    tq = min(tq, length)
    nhalf = length // half
    per = tq // half

    def col(c):
        return lambda bi, r, i: (bi, r, i, c)

    def prev(c):
        return lambda bi, r, i: (bi, r, jnp.maximum(i * per - 1, 0), c)

    def nxt(c):
        return lambda bi, r, i: (bi, r, jnp.minimum((i + 1) * per, nhalf - 1), c)

    main = lambda c: pl.BlockSpec((1, 1, tq, DIL_WIDTH), col(c))
    edge = lambda f, c: pl.BlockSpec((1, 1, half, DIL_WIDTH), f(c))
    out_spec = pl.BlockSpec((1, 1, tq, DIL_WIDTH), lambda bi, r, i: (bi, r, i, 0))
    return pl.pallas_call(
        functools.partial(_band_kernel, half=half, length=length, sub=sub),
        grid=(b, d, length // tq),
        in_specs=[main(0), edge(prev, 1), main(1), edge(nxt, 1), edge(prev, 2), main(2), edge(nxt, 2)],
        out_specs=[out_spec, out_spec],
        out_shape=[jax.ShapeDtypeStruct((b, d, length, DIL_WIDTH), BF16),
                   jax.ShapeDtypeStruct((b, d, length, DIL_WIDTH), F32)],
        scratch_shapes=[pltpu.VMEM((tq + 2 * half, DIL_WIDTH), BF16)] * 2,
        compiler_params=pltpu.CompilerParams(
            dimension_semantics=("arbitrary", "arbitrary", "arbitrary"),
            vmem_limit_bytes=VMEM_LIMIT),
        name=f"band_attn_d{d}",
    )(qkv, qkv, qkv, qkv, qkv, qkv, qkv)


def _silu(g):
    return g / (1.0 + jnp.exp(-g))


def _out_kernel(x_ref, a_ref, gate_ref, o1, l1, o2, l2, o3, l3, w_ref, fg_ref, y_ref, nat_scr,
                *, final):
    tm = x_ref.shape[0]

    def natural(ref, slot):
        d = ref.shape[1]
        if d == 1:
            return ref[0, 0]
        ntile = ref.shape[3] // LANES
        for r in range(d):
            for c in range(ntile):
                nat_scr[slot * ntile + c, pl.ds(r, tm // d, stride=d), :] = (
                    ref[0, r, :, c * LANES:(c + 1) * LANES].astype(F32))
        return jnp.concatenate([nat_scr[slot * ntile + c] for c in range(ntile)], axis=1)

    la, lb, lc = l1[0, 0], natural(l2, 0), natural(l3, 1)
    mx = jnp.maximum(jnp.maximum(la, lb), lc)
    ea, eb, ec = jnp.exp2(la - mx), jnp.exp2(lb - mx), jnp.exp2(lc - mx)
    bmix = (ea * o1[0, 0] + eb * natural(o2, 2) + ec * natural(o3, 3)) / (ea + eb + ec)
    gates = gate_ref[...]
    mix = jnp.concatenate([a_ref[...] * _silu(gates[:, :MLA_WIDTH]),
                           bmix * _silu(gates[:, MLA_WIDTH:])], axis=-1).astype(BF16)
    y = x_ref[...] + jnp.dot(mix, w_ref[...], preferred_element_type=F32)
    if final:
        y = _rms_f32(y, fg_ref[...])
    y_ref[...] = y


def _out_proj(x2d, a2d, gates, groups, w_out_all, layer, final_g, *, tm, seq, final):
    m, dm = x2d.shape
    nblk = seq // tm
    row = lambda n: pl.BlockSpec((tm, n), lambda i: (i, 0))
    args = [x2d, a2d, gates]
    specs = [row(dm), row(MLA_WIDTH), row(2 * MLA_WIDTH)]
    for (_, d), (o, lse) in zip(DIL_PAIRS, groups):
        spec = pl.BlockSpec((1, d, tm // d, DIL_WIDTH), lambda i: (i // nblk, 0, i % nblk, 0))
        args += [o, lse]
        specs += [spec, spec]
    args += [w_out_all, final_g.reshape(1, dm)]
    specs += [pl.BlockSpec((None, dm, dm), lambda i: (layer, 0, 0)),
              pl.BlockSpec((1, dm), lambda i: (0, 0))]
    return pl.pallas_call(
        functools.partial(_out_kernel, final=final),
        grid=(m // tm,),
        in_specs=specs,
        out_specs=row(dm),
        out_shape=jax.ShapeDtypeStruct((m, dm), F32),
        scratch_shapes=[pltpu.VMEM((4 * DIL_WIDTH // LANES, tm, LANES), F32)],
        compiler_params=pltpu.CompilerParams(
            dimension_semantics=("arbitrary",), vmem_limit_bytes=VMEM_LIMIT),
        name="merge_out_proj",
    )(*args)


def _w_in_column_plan():
    o_kr = Q_LORA + KV_LORA
    o_ga = o_kr + MLA_ROPE
    o_dil = o_ga + MLA_WIDTH
    o_gb = o_dil + 3 * DIL_GROUPS * DIL_WIDTH
    total = o_gb + DIL_WIDTH
    pad = lambda n: [(0, 0)] * n
    cols = [(o_dil + c, 1) for c in _dil_column_perm()]
    cols += [(c, 1) for c in range(o_ga, o_dil)] + [(c, 1) for c in range(o_gb, total)]
    cols += [(c, 1) for c in range(o_kr)] + pad(MLA_NOPE) + [(c, 1) for c in range(o_kr, o_ga)]
    cols += pad(HEAD_PAD - MLA_NOPE - MLA_ROPE) + pad(N_PLAIN_TILES * DIL_WIDTH - LAT_WIDTH)
    src, keep = zip(*cols)
    return np.asarray(src, np.int32), np.asarray(keep, np.float32)


def _pad_heads(w, per_head, keep):
    lead = w.shape[:-1]
    wh = w.reshape(*lead, MLA_HEADS, per_head)[..., keep]
    wh = jnp.pad(wh, [(0, 0)] * (wh.ndim - 1) + [(0, HEAD_PAD - wh.shape[-1])])
    return wh.reshape(*lead, MLA_HEADS * HEAD_PAD)


def kernel(x, norm_g, w_in, q_norm_g, kv_norm_g, w_uq, w_ukv, w_out, final_g):
    b, s, dm = x.shape
    m = b * s
    tm_in = 1024
    mla_scale = (MLA_NOPE + MLA_ROPE) ** -0.5 * LOG2E
    dil_scale = DIL_HD ** -0.5 * LOG2E
    tabs_mqt = _mla_tables(s, mla_scale).transpose(0, 2, 1)
    tabs_mk = _mla_tables(s, 1.0)
    tabs_d = _dil_tables(s, math.sqrt(dil_scale), tm_in)

    src, keep = _w_in_column_plan()
    w_in_all = (jnp.take(w_in, src, axis=2) * keep).astype(BF16)
    wqt_all = _pad_heads(w_uq, MLA_NOPE + MLA_ROPE, slice(None)).astype(BF16).swapaxes(1, 2)
    wk_all = _pad_heads(w_ukv, MLA_NOPE + MLA_V, slice(0, MLA_NOPE)).astype(BF16)
    wvt_all = _pad_heads(w_ukv, MLA_NOPE + MLA_V, slice(MLA_NOPE, None)).astype(BF16).swapaxes(1, 2)
    w_out_all = w_out.astype(BF16)

    x2d = x.reshape(m, dm)
    for layer in range(DEPTH):
        *dil, gates, lat = _in_proj(x2d, norm_g[layer], w_in_all, layer, tabs_d,
                                    tm=tm_in, batch=b, seq=s)

        qt, kk, vt = _mla_prep(lat.reshape(b, s, -1), q_norm_g[layer], kv_norm_g[layer],
                               wqt_all[layer], wk_all[layer], wvt_all[layer], tabs_mqt, tabs_mk, ts=512)
        a = _mla_flash(qt, kk, vt, tq=512, tk=2048, qw=512)

        groups = [_band_attention(dil[gi], gi, tq=512, sub=128) for gi in range(DIL_GROUPS)]

        x2d = _out_proj(x2d, a.reshape(m, MLA_WIDTH), gates, groups, w_out_all, layer,
                        final_g, tm=512, seq=s, final=(layer == DEPTH - 1))
    return x2d.reshape(b, s, dm)
```

```python
import functools
import math

import jax
import jax.numpy as jnp
from jax import lax
from jax.experimental import pallas as pl
from jax.experimental.pallas import tpu as pltpu

F32 = jnp.float32
BF16 = jnp.bfloat16

D_MODEL = 1024
DEPTH = 4
MLA_HEADS = 8
MLA_NOPE = 64
MLA_ROPE = 32
MLA_V = 64
Q_LORA = 384
KV_LORA = 256
MLA_WIDTH = MLA_HEADS * MLA_V
DIL_PAIRS = ((128, 1), (512, 4), (2048, 16))
DIL_GROUPS = 3
DIL_HEADS = 8
DIL_HD = 64
DIL_WIDTH = DIL_HEADS * DIL_HD
ROT_DIM = DIL_HD // 4
ROPE_THETA = 500000.0
EPS = 1e-6

LANES = 128
BF16_ROWS = 16
HEAD_PAD = 128
LAT_WIDTH = Q_LORA + KV_LORA + HEAD_PAD
VT_ROWS = -(-(MLA_V + 1) // BF16_ROWS) * BF16_ROWS
GROUP_W = 3 * DIL_WIDTH
PLAIN_W = 2 * DIL_WIDTH
ROW_SPLIT = 4
SUB_K = 256
LOG2E = math.log2(math.e)
NEG_BIG = -1e30
VMEM_LIMIT = 48 * 1024 * 1024
VMEM_LIMIT_IN_PROJ = 58 * 1024 * 1024


def _rope_tables(seq, dim):
    inv = 1.0 / (ROPE_THETA ** (jnp.arange(0, dim, 2, dtype=F32) / dim))
    ang = jnp.arange(seq, dtype=F32)[:, None] * inv[None, :]
    return jnp.cos(ang), jnp.sin(ang)


def _mla_tables(seq, scale):
    cos, sin = _rope_tables(seq, MLA_ROPE)
    half = MLA_ROPE // 2
    ones = jnp.ones((seq, MLA_NOPE), F32)
    z = lambda n: jnp.zeros((seq, n), F32)
    c = jnp.concatenate([ones, cos, cos, z(HEAD_PAD - MLA_NOPE - MLA_ROPE)], axis=1)
    a = jnp.concatenate([z(MLA_NOPE), -sin, z(half), z(HEAD_PAD - MLA_NOPE - MLA_ROPE)], axis=1)
    b = jnp.concatenate([z(MLA_NOPE), z(half), sin, z(HEAD_PAD - MLA_NOPE - MLA_ROPE)], axis=1)
    return jnp.stack([c, a, b]) * scale


def _dil_tables(seq, scale, tm):
    cos, sin = _rope_tables(seq, ROT_DIM)
    half = ROT_DIM // 2
    rest = DIL_HD - ROT_DIM
    z = lambda n: jnp.zeros((seq, n), F32)
    c = jnp.concatenate([cos, cos, jnp.ones((seq, rest), F32)], axis=1)
    a = jnp.concatenate([-sin, z(half), z(rest)], axis=1)
    b = jnp.concatenate([z(half), sin, z(rest)], axis=1)
    rep = LANES // DIL_HD
    tab = jnp.stack([jnp.tile(c, (1, rep)), jnp.tile(a, (1, rep)), jnp.tile(b, (1, rep))]) * scale
    out = []
    for _, d in DIL_PAIRS:
        t = tab.reshape(3, seq // tm, tm // d, d, LANES).transpose(0, 1, 3, 2, 4)
        out.append(t.reshape(3, seq, LANES))
    return jnp.stack(out)


def _rotate3(blk, tab_c, tab_a, tab_b, shift):
    return (blk * tab_c
            + pltpu.roll(blk, LANES - shift, 1) * tab_a
            + pltpu.roll(blk, shift, 1) * tab_b)


def _rms_f32(x, g):
    ms = jnp.mean(x * x, axis=-1, keepdims=True)
    return x * lax.rsqrt(ms + EPS) * g


def _in_proj_kernel(x_ref, g_ref, w_ref, tab_ref, dil_ref, plain_ref, h_scr, xn_scr):
    tm, k = x_ref.shape
    j = pl.program_id(1)
    nhalf = xn_scr.shape[0]

    @pl.when(j == 0)
    def _():
        xn = _rms_f32(x_ref[...], g_ref[...])
        h_scr[0] = xn.astype(BF16)
        for base in range(0, k // LANES, nhalf):
            for c in range(nhalf):
                xn_scr[c] = xn[:, (base + c) * LANES:(base + c + 1) * LANES]
            for gi, (_, d) in enumerate(DIL_PAIRS):
                if d == 1:
                    continue
                rows = tm // d
                for r in range(d):
                    for c in range(nhalf):
                        sl = slice((base + c) * LANES, (base + c + 1) * LANES)
                        h_scr[gi, r * rows:(r + 1) * rows, sl] = (
                            xn_scr[c, pl.ds(r, rows, stride=d), :].astype(BF16))

    span = tm // ROW_SPLIT

    @pl.when(j < DIL_GROUPS)
    def _():
        cols = pl.ds(pl.multiple_of(j * GROUP_W, LANES), GROUP_W)
        for part in range(ROW_SPLIT):
            rs = slice(part * span, (part + 1) * span)
            y = jnp.dot(h_scr[j, rs, :], w_ref[:, cols], preferred_element_type=F32)
            tc, ta, tb = tab_ref[0, 0, rs, :], tab_ref[0, 1, rs, :], tab_ref[0, 2, rs, :]
            for c in range(GROUP_W // LANES):
                sl = slice(c * LANES, (c + 1) * LANES)
                yc = y[:, sl]
                if c < 2 * DIL_WIDTH // LANES:
                    yc = _rotate3(yc, tc, ta, tb, ROT_DIM // 2)
                dil_ref[rs, sl] = yc.astype(BF16)

    @pl.when(j >= DIL_GROUPS)
    def _():
        start = DIL_GROUPS * GROUP_W + (j - DIL_GROUPS) * PLAIN_W
        cols = pl.ds(pl.multiple_of(start, LANES), PLAIN_W)
        for part in range(ROW_SPLIT):
            rs = slice(part * span, (part + 1) * span)
            plain_ref[rs, :] = jnp.dot(h_scr[0, rs, :], w_ref[:, cols], preferred_element_type=F32)


def _in_proj(x2d, g, w_all, layer, tabs, *, tm, seq):
    m, k = x2d.shape
    nblk = seq // tm
    nplain = (w_all.shape[2] - DIL_GROUPS * GROUP_W) // PLAIN_W
    return pl.pallas_call(
        _in_proj_kernel,
        grid=(m // tm, DIL_GROUPS + nplain),
        in_specs=[
            pl.BlockSpec((tm, k), lambda i, j: (i, 0)),
            pl.BlockSpec((1, k), lambda i, j: (0, 0)),
            pl.BlockSpec((None, k, w_all.shape[2]), lambda i, j: (layer, 0, 0),
                         pipeline_mode=pl.Buffered(1)),
            pl.BlockSpec((1, 3, tm, LANES), lambda i, j: (jnp.minimum(j, DIL_GROUPS - 1), 0, i % nblk, 0)),
        ],
        out_specs=[
            pl.BlockSpec((tm, GROUP_W), lambda i, j: (i, jnp.minimum(j, DIL_GROUPS - 1))),
            pl.BlockSpec((tm, PLAIN_W), lambda i, j: (i, jnp.maximum(j - DIL_GROUPS, 0))),
        ],
        out_shape=[
            jax.ShapeDtypeStruct((m, DIL_GROUPS * GROUP_W), BF16),
            jax.ShapeDtypeStruct((m, nplain * PLAIN_W), F32),
        ],
        scratch_shapes=[pltpu.VMEM((DIL_GROUPS, tm, k), BF16), pltpu.VMEM((k // LANES // 2, tm, LANES), F32)],
        compiler_params=pltpu.CompilerParams(
            dimension_semantics=("arbitrary", "arbitrary"), vmem_limit_bytes=VMEM_LIMIT_IN_PROJ),
        name="in_proj",
    )(x2d, g.reshape(1, k), w_all, tabs)


def _mla_prep_kernel(lat_ref, qg_ref, kvg_ref, wqt_ref, wk_ref, wvt_ref, tqt_ref, tk_ref,
                     qt_out, k_out, vt_out):
    lat = lat_ref[0]
    ts = lat.shape[0]
    cqn = _rms_f32(lat[:, :Q_LORA], qg_ref[...]).astype(BF16)
    ckvn = _rms_f32(lat[:, Q_LORA:Q_LORA + KV_LORA], kvg_ref[...]).astype(BF16)
    kr = lat[:, Q_LORA + KV_LORA:]
    nt = (((1,), (1,)), ((), ()))
    half = MLA_ROPE // 2

    qt = lax.dot_general(wqt_ref[...], cqn, nt, preferred_element_type=F32)
    qc, qa, qb = tqt_ref[0], tqt_ref[1], tqt_ref[2]
    for h in range(MLA_HEADS):
        blk = qt[h * HEAD_PAD:(h + 1) * HEAD_PAD, :]
        up = jnp.concatenate([blk[half:], blk[:half]], axis=0)
        down = jnp.concatenate([blk[-half:], blk[:-half]], axis=0)
        qt_out[0, h] = (blk * qc + up * qa + down * qb).astype(BF16)

    krp = _rotate3(kr, tk_ref[0], tk_ref[1], tk_ref[2], half)
    kk = jnp.dot(ckvn, wk_ref[...], preferred_element_type=F32)
    for h in range(MLA_HEADS):
        k_out[0, h] = (kk[:, h * HEAD_PAD:(h + 1) * HEAD_PAD] + krp).astype(BF16)

    vt = lax.dot_general(wvt_ref[...], ckvn, nt, preferred_element_type=F32)
    row = lax.broadcasted_iota(jnp.int32, (HEAD_PAD, ts), 0)
    ones_row = (row == MLA_V).astype(F32)
    for h in range(MLA_HEADS):
        vt_out[0, h] = (vt[h * HEAD_PAD:(h + 1) * HEAD_PAD, :] + ones_row).astype(BF16)


def _mla_prep(plain, qg, kvg, wqt, wk, wvt, tabs_qt, tabs_k, *, ts):
    b, s, _ = plain.shape
    hp = MLA_HEADS * HEAD_PAD
    const = lambda *shape: pl.BlockSpec(shape, lambda bi, i: (0,) * len(shape))
    rows = pl.BlockSpec((1, MLA_HEADS, ts, HEAD_PAD), lambda bi, i: (bi, 0, i, 0))
    cols = pl.BlockSpec((1, MLA_HEADS, HEAD_PAD, ts), lambda bi, i: (bi, 0, 0, i))
    rows_shape = jax.ShapeDtypeStruct((b, MLA_HEADS, s, HEAD_PAD), BF16)
    cols_shape = jax.ShapeDtypeStruct((b, MLA_HEADS, HEAD_PAD, s), BF16)
    return pl.pallas_call(
        _mla_prep_kernel,
        grid=(b, s // ts),
        in_specs=[
            pl.BlockSpec((1, ts, LAT_WIDTH), lambda bi, i: (bi, i, 0)),
            const(1, Q_LORA), const(1, KV_LORA),
            const(hp, Q_LORA), const(KV_LORA, hp), const(hp, KV_LORA),
            pl.BlockSpec((3, LANES, ts), lambda bi, i: (0, 0, i)),
            pl.BlockSpec((3, ts, LANES), lambda bi, i: (0, i, 0)),
        ],
        out_specs=[cols, rows, cols],
        out_shape=[cols_shape, rows_shape, cols_shape],
        compiler_params=pltpu.CompilerParams(
            dimension_semantics=("arbitrary", "arbitrary"), vmem_limit_bytes=VMEM_LIMIT),
        name="mla_prep",
    )(plain, qg.reshape(1, -1), kvg.reshape(1, -1), wqt, wk, wvt, tabs_qt, tabs_k)


def _flash_kernel(qt_ref, k_ref, vt_ref, o_ref, *, tk, qw):
    tq = qt_ref.shape[3]
    s_len = k_ref.shape[2]
    nsub = tk // SUB_K
    nchunk = s_len // tk
    items = [(qb, hh, c) for qb in range(tq // qw) for hh in range(2) for c in range(nchunk)]

    def scores(item, r):
        qb, hh, c = item
        rows = slice(c * tk + r * SUB_K, c * tk + (r + 1) * SUB_K)
        qt = qt_ref[0, hh, :, qb * qw:(qb + 1) * qw]
        return jnp.dot(k_ref[0, hh, rows, :], qt, preferred_element_type=F32)

    def weighted(item, r, p):
        _, hh, c = item
        cols = slice(c * tk + r * SUB_K, c * tk + (r + 1) * SUB_K)
        return jnp.dot(vt_ref[0, hh, 0:VT_ROWS, cols], p, preferred_element_type=F32)

    def fold(state, item, alpha, pv):
        key = item[:2]
        m, acc = state[key]
        state[key] = (m, alpha * acc + pv)

    state = {}
    s_parts = [scores(items[0], r) for r in range(nsub)]
    prev = None
    for t, item in enumerate(items):
        key = item[:2]
        if key not in state:
            state[key] = (jnp.full((1, qw), NEG_BIG, F32), jnp.zeros((VT_ROWS, qw), F32))
        m, acc = state[key]
        m_new = m
        for s in s_parts:
            m_new = jnp.maximum(m_new, jnp.max(s, axis=0, keepdims=True))
        alpha = jnp.exp2(m - m_new)
        state[key] = (m_new, acc)
        nxt, p_parts, pv = [], [], None
        for r in range(nsub):
            if t + 1 < len(items):
                nxt.append(scores(items[t + 1], r))
            if prev is not None:
                part = weighted(prev[0], r, prev[1][r])
                pv = part if pv is None else pv + part
            p_parts.append(jnp.exp2(s_parts[r] - m_new).astype(BF16))
        if prev is not None:
            fold(state, prev[0], prev[2], pv)
        s_parts, prev = nxt, (item, p_parts, alpha)
    pv = None
    for r in range(nsub):
        part = weighted(prev[0], r, prev[1][r])
        pv = part if pv is None else pv + part
    fold(state, prev[0], prev[2], pv)

    for qb in range(tq // qw):
        outs = []
        for hh in range(2):
            acc = state[(qb, hh)][1]
            outs.append(acc[:MLA_V] / acc[MLA_V:MLA_V + 1])
        o_ref[0, qb * qw:(qb + 1) * qw, :] = jnp.concatenate(outs, axis=0).T.astype(o_ref.dtype)


def _mla_flash(qt, k, vt, *, tq, tk, qw):
    b, h, s, _ = k.shape
    return pl.pallas_call(
        functools.partial(_flash_kernel, tk=tk, qw=qw),
        grid=(b, h // 2, s // tq),
        in_specs=[
            pl.BlockSpec((1, 2, HEAD_PAD, tq), lambda bi, hp, i: (bi, hp, 0, i)),
            pl.BlockSpec((1, 2, s, HEAD_PAD), lambda bi, hp, i: (bi, hp, 0, 0)),
            pl.BlockSpec((1, 2, HEAD_PAD, s), lambda bi, hp, i: (bi, hp, 0, 0)),
        ],
        out_specs=pl.BlockSpec((1, tq, 2 * MLA_V), lambda bi, hp, i: (bi, i, hp)),
        out_shape=jax.ShapeDtypeStruct((b, s, MLA_WIDTH), BF16),
        compiler_params=pltpu.CompilerParams(
            dimension_semantics=("arbitrary", "arbitrary", "arbitrary"),
            vmem_limit_bytes=VMEM_LIMIT),
        name="mla_flash",
    )(qt, k, vt)


def _band_kernel(q_ref, kp_ref, km_ref, kn_ref, vp_ref, vm_ref, vn_ref, o_ref, lse_ref,
                 qwin, kwin, vwin, *, half, length, sub):
    tq = qwin.shape[0]
    nchunk, chunk = q_ref.shape[1], q_ref.shape[3]
    i = pl.program_id(2)
    for c in range(nchunk):
        rows = slice(c * chunk, (c + 1) * chunk)
        qwin[rows] = q_ref[0, c, 0]
        kwin[half + c * chunk:half + (c + 1) * chunk] = km_ref[0, c, 0]
        vwin[half + c * chunk:half + (c + 1) * chunk] = vm_ref[0, c, 0]
    kwin[0:half] = kp_ref[0, 0, 0]
    kwin[half + tq:] = kn_ref[0, 0, 0]
    vwin[0:half] = vp_ref[0, 0, 0]
    vwin[half + tq:] = vn_ref[0, 0, 0]

    wk = sub + 2 * half
    row = lax.broadcasted_iota(jnp.int32, (2 * sub, wk), 0) % sub
    col = lax.broadcasted_iota(jnp.int32, (2 * sub, wk), 1)
    rel = col - half - row
    band = (rel >= -half) & (rel <= half)
    lo = lax.broadcasted_iota(jnp.int32, (sub, LANES), 1) < DIL_HD
    ones = jnp.ones((wk, LANES), BF16)

    def step(t, carry):
        r0 = pl.multiple_of(t * sub, sub)
        kpos = i * tq + r0 - half + col
        mask = band & (kpos >= 0) & (kpos < length)
        for c in range(DIL_WIDTH // LANES):
            cs = slice(c * LANES, (c + 1) * LANES)
            qt = qwin[pl.ds(r0, sub), cs]
            zero = jnp.zeros_like(qt)
            q2 = jnp.concatenate([jnp.where(lo, qt, zero), jnp.where(lo, zero, qt)], axis=0)
            kt = kwin[pl.ds(r0, wk), cs]
            vt = vwin[pl.ds(r0, wk), cs]
            s = lax.dot_general(q2, kt, (((1,), (1,)), ((), ())), preferred_element_type=F32)
            s = jnp.where(mask, s, NEG_BIG)
            m = jnp.max(s, axis=-1, keepdims=True)
            p = jnp.exp2(s - m).astype(BF16)
            ol = jnp.dot(p, jnp.concatenate([vt, ones], axis=1), preferred_element_type=F32)
            o = jnp.where(lo, ol[:sub, :LANES], ol[sub:, :LANES])
            l = jnp.where(lo, ol[:sub, LANES:], ol[sub:, LANES:])
            mm = jnp.where(lo, m[:sub], m[sub:])
            o_ref[0, 0, pl.ds(r0, sub), cs] = (o / l).astype(o_ref.dtype)
            lse_ref[0, 0, pl.ds(r0, sub), cs] = mm + jnp.log2(l)
        return carry

    lax.fori_loop(0, tq // sub, step, 0, unroll=True)


def _band_attention(dil2d, g, *, tm, tq, sub, batch, seq):
    window, d = DIL_PAIRS[g]
    half = window // (2 * d)
    length = seq // d
    tq = min(tq, length)
    chunk = tm // d
    view = dil2d.reshape(batch, seq // tm, d, chunk, dil2d.shape[1])
    rows_blk = min(chunk, tq)
    nchunk = tq // rows_blk

    def main(c):
        col = 3 * g + c
        if chunk >= tq:
            per = chunk // tq
            imap = lambda bi, r, i: (bi, i // per, r, i % per, col)
        else:
            imap = lambda bi, r, i: (bi, i, r, 0, col)
        return pl.BlockSpec((1, nchunk, 1, rows_blk, DIL_WIDTH), imap)

    def edge(c, start_of):
        col = 3 * g + c

        def imap(bi, r, i):
            start = start_of(i)
            return (bi, start // chunk, r, (start % chunk) // half, col)
        return pl.BlockSpec((1, 1, 1, half, DIL_WIDTH), imap)

    before = lambda i: jnp.maximum(i * tq - half, 0)
    after = lambda i: jnp.minimum((i + 1) * tq, length - half)
    out_spec = pl.BlockSpec((1, 1, tq, DIL_WIDTH), lambda bi, r, i: (bi, r, i, 0))
    return pl.pallas_call(
        functools.partial(_band_kernel, half=half, length=length, sub=sub),
        grid=(batch, d, length // tq),
        in_specs=[main(0), edge(1, before), main(1), edge(1, after),
                  edge(2, before), main(2), edge(2, after)],
        out_specs=[out_spec, out_spec],
        out_shape=[jax.ShapeDtypeStruct((batch, d, length, DIL_WIDTH), BF16),
                   jax.ShapeDtypeStruct((batch, d, length, DIL_WIDTH), F32)],
        scratch_shapes=[pltpu.VMEM((tq, DIL_WIDTH), BF16),
                        pltpu.VMEM((tq + 2 * half, DIL_WIDTH), BF16),
                        pltpu.VMEM((tq + 2 * half, DIL_WIDTH), BF16)],
        compiler_params=pltpu.CompilerParams(
            dimension_semantics=("arbitrary", "arbitrary", "arbitrary"),
            vmem_limit_bytes=VMEM_LIMIT),
        name=f"band_attn_d{d}",
    )(view, view, view, view, view, view, view)


def _silu(g):
    return g / (1.0 + jnp.exp(-g))


def _out_kernel(x_ref, a_ref, gate_ref, o1, l1, o2, l2, o3, l3, w_ref, fg_ref, y_ref, nat_scr,
                *, final):
    tm = x_ref.shape[0]

    def natural(ref, slot):
        d = ref.shape[1]
        if d == 1:
            return ref[0, 0]
        ntile = ref.shape[3] // LANES
        for r in range(d):
            for c in range(ntile):
                nat_scr[slot * ntile + c, pl.ds(r, tm // d, stride=d), :] = (
                    ref[0, r, :, c * LANES:(c + 1) * LANES].astype(F32))
        return jnp.concatenate([nat_scr[slot * ntile + c] for c in range(ntile)], axis=1)

    la, lb, lc = l1[0, 0], natural(l2, 0), natural(l3, 1)
    mx = jnp.maximum(jnp.maximum(la, lb), lc)
    ea, eb, ec = jnp.exp2(la - mx), jnp.exp2(lb - mx), jnp.exp2(lc - mx)
    bmix = (ea * o1[0, 0] + eb * natural(o2, 2) + ec * natural(o3, 3)) / (ea + eb + ec)
    gates = gate_ref[...]
    mix = jnp.concatenate([a_ref[...] * _silu(gates[:, :MLA_WIDTH]),
                           bmix * _silu(gates[:, MLA_WIDTH:])], axis=-1).astype(BF16)
    y = x_ref[...] + jnp.dot(mix, w_ref[...], preferred_element_type=F32)
    if final:
        y = _rms_f32(y, fg_ref[...])
    y_ref[...] = y


def _out_proj(x2d, a2d, plain, groups, w_out_all, layer, final_g, *, tm, seq, final):
    m, dm = x2d.shape
    nblk = seq // tm
    row = lambda n: pl.BlockSpec((tm, n), lambda i: (i, 0))
    gate_blk = plain.shape[1] // PLAIN_W - 1
    args = [x2d, a2d, plain]
    specs = [row(dm), row(MLA_WIDTH), pl.BlockSpec((tm, PLAIN_W), lambda i: (i, gate_blk))]
    for (_, d), (o, lse) in zip(DIL_PAIRS, groups):
        spec = pl.BlockSpec((1, d, tm // d, DIL_WIDTH), lambda i: (i // nblk, 0, i % nblk, 0))
        args += [o, lse]
        specs += [spec, spec]
    args += [w_out_all, final_g.reshape(1, dm)]
    specs += [pl.BlockSpec((None, dm, dm), lambda i: (layer, 0, 0)),
              pl.BlockSpec((1, dm), lambda i: (0, 0))]
    return pl.pallas_call(
        functools.partial(_out_kernel, final=final),
        grid=(m // tm,),
        in_specs=specs,
        out_specs=row(dm),
        out_shape=jax.ShapeDtypeStruct((m, dm), F32),
        scratch_shapes=[pltpu.VMEM((4 * DIL_WIDTH // LANES, tm, LANES), F32)],
        compiler_params=pltpu.CompilerParams(
            dimension_semantics=("arbitrary",), vmem_limit_bytes=VMEM_LIMIT),
        name="merge_out_proj",
    )(*args)


def _arrange_w_in(w):
    o_kr = Q_LORA + KV_LORA
    o_ga = o_kr + MLA_ROPE
    o_dil = o_ga + MLA_WIDTH
    o_gb = o_dil + DIL_GROUPS * GROUP_W
    z = lambda n: jnp.zeros(w.shape[:2] + (n,), w.dtype)
    parts = [w[..., o_dil:o_gb],
             w[..., :o_kr], z(MLA_NOPE), w[..., o_kr:o_ga], z(HEAD_PAD - MLA_NOPE - MLA_ROPE),
             z(PLAIN_W - LAT_WIDTH),
             w[..., o_ga:o_dil], w[..., o_gb:]]
    return jnp.concatenate(parts, axis=-1).astype(BF16)


def _pad_heads(w, per_head, keep):
    lead = w.shape[:-1]
    wh = w.reshape(*lead, MLA_HEADS, per_head)[..., keep]
    wh = jnp.pad(wh, [(0, 0)] * (wh.ndim - 1) + [(0, HEAD_PAD - wh.shape[-1])])
    return wh.reshape(*lead, MLA_HEADS * HEAD_PAD)


def kernel(x, norm_g, w_in, q_norm_g, kv_norm_g, w_uq, w_ukv, w_out, final_g):
    b, s, dm = x.shape
    m = b * s
    tm_in = 1024
    mla_scale = (MLA_NOPE + MLA_ROPE) ** -0.5 * LOG2E
    dil_scale = DIL_HD ** -0.5 * LOG2E
    tabs_mqt = _mla_tables(s, mla_scale).transpose(0, 2, 1)
    tabs_mk = _mla_tables(s, 1.0)
    tabs_d = _dil_tables(s, math.sqrt(dil_scale), tm_in)

    w_in_all = _arrange_w_in(w_in)
    wqt_all = _pad_heads(w_uq, MLA_NOPE + MLA_ROPE, slice(None)).astype(BF16).swapaxes(1, 2)
    wk_all = _pad_heads(w_ukv, MLA_NOPE + MLA_V, slice(0, MLA_NOPE)).astype(BF16)
    wvt_all = _pad_heads(w_ukv, MLA_NOPE + MLA_V, slice(MLA_NOPE, None)).astype(BF16).swapaxes(1, 2)
    w_out_all = w_out.astype(BF16)

    x2d = x.reshape(m, dm)
    for layer in range(DEPTH):
        dil, plain = _in_proj(x2d, norm_g[layer], w_in_all, layer, tabs_d, tm=tm_in, seq=s)

        qt, kk, vt = _mla_prep(plain.reshape(b, s, -1), q_norm_g[layer], kv_norm_g[layer],
                               wqt_all[layer], wk_all[layer], wvt_all[layer], tabs_mqt, tabs_mk, ts=512)
        a = _mla_flash(qt, kk, vt, tq=512, tk=2048, qw=512)

        groups = [_band_attention(dil, gi, tm=tm_in, tq=512, sub=128, batch=b, seq=s)
                  for gi in range(DIL_GROUPS)]

        x2d = _out_proj(x2d, a.reshape(m, MLA_WIDTH), plain, groups, w_out_all, layer,
                        final_g, tm=512, seq=s, final=(layer == DEPTH - 1))
    return x2d.reshape(b, s, dm)
```

```python
import functools
import math

import numpy as np
import jax
import jax.numpy as jnp
from jax import lax
from jax.experimental import pallas as pl
from jax.experimental.pallas import tpu as pltpu

F32 = jnp.float32
BF16 = jnp.bfloat16

D_MODEL = 1024
DEPTH = 4
MLA_HEADS = 8
MLA_NOPE = 64
MLA_ROPE = 32
MLA_V = 64
Q_LORA = 384
KV_LORA = 256
MLA_WIDTH = MLA_HEADS * MLA_V
DIL_PAIRS = ((128, 1), (512, 4), (2048, 16))
DIL_GROUPS = 3
DIL_HEADS = 8
DIL_HD = 64
DIL_WIDTH = DIL_HEADS * DIL_HD
ROT_DIM = DIL_HD // 4
ROPE_THETA = 500000.0
EPS = 1e-6

LANES = 128
BF16_ROWS = 16
HEAD_PAD = 128
LAT_WIDTH = Q_LORA + KV_LORA + HEAD_PAD
VT_ROWS = -(-(MLA_V + 1) // BF16_ROWS) * BF16_ROWS
GROUP_W = 3 * DIL_WIDTH
PLAIN_W = 2 * DIL_WIDTH
ROW_SPLIT = 4
SUB_K = 256
LOG2E = math.log2(math.e)
NEG_BIG = -1e30
VMEM_LIMIT = 48 * 1024 * 1024
VMEM_LIMIT_IN_PROJ = 58 * 1024 * 1024


def _rope_tables(seq, dim):
    inv = 1.0 / (ROPE_THETA ** (jnp.arange(0, dim, 2, dtype=F32) / dim))
    ang = jnp.arange(seq, dtype=F32)[:, None] * inv[None, :]
    return jnp.cos(ang), jnp.sin(ang)


def _mla_tables(seq, scale):
    cos, sin = _rope_tables(seq, MLA_ROPE)
    half = MLA_ROPE // 2
    ones = jnp.ones((seq, MLA_NOPE), F32)
    z = lambda n: jnp.zeros((seq, n), F32)
    c = jnp.concatenate([ones, cos, cos, z(HEAD_PAD - MLA_NOPE - MLA_ROPE)], axis=1)
    a = jnp.concatenate([z(MLA_NOPE), -sin, z(half), z(HEAD_PAD - MLA_NOPE - MLA_ROPE)], axis=1)
    b = jnp.concatenate([z(MLA_NOPE), z(half), sin, z(HEAD_PAD - MLA_NOPE - MLA_ROPE)], axis=1)
    return jnp.stack([c, a, b]) * scale


def _dil_tables(seq, scale, tm):
    half = ROT_DIM // 2
    rest = DIL_HD - ROT_DIM
    rep = LANES // DIL_HD
    inv = 1.0 / (ROPE_THETA ** (jnp.arange(0, ROT_DIM, 2, dtype=F32) / ROT_DIM))
    z = lambda n: jnp.zeros((seq, n), F32)
    out = []
    for _, d in DIL_PAIRS:
        pos = np.arange(seq).reshape(seq // tm, tm // d, d).transpose(0, 2, 1).reshape(seq)
        ang = jnp.asarray(pos, F32)[:, None] * inv[None, :]
        cos, sin = jnp.cos(ang) * scale, jnp.sin(ang) * scale
        c = jnp.concatenate([cos, cos, jnp.full((seq, rest), scale, F32)], axis=1)
        a = jnp.concatenate([-sin, z(half), z(rest)], axis=1)
        b = jnp.concatenate([z(half), sin, z(rest)], axis=1)
        out.append(jnp.stack([jnp.tile(c, (1, rep)), jnp.tile(a, (1, rep)), jnp.tile(b, (1, rep))]))
    return jnp.stack(out)


def _rotate3(blk, tab_c, tab_a, tab_b, shift):
    return (blk * tab_c
            + pltpu.roll(blk, LANES - shift, 1) * tab_a
            + pltpu.roll(blk, shift, 1) * tab_b)


def _rms_f32(x, g):
    ms = jnp.mean(x * x, axis=-1, keepdims=True)
    return x * lax.rsqrt(ms + EPS) * g


def _in_proj_kernel(x_ref, g_ref, w_ref, tab_ref, dil_ref, plain_ref, h_scr, xn_scr):
    tm, k = x_ref.shape
    j = pl.program_id(1)
    nhalf = xn_scr.shape[0]

    @pl.when(j == 0)
    def _():
        xn = _rms_f32(x_ref[...], g_ref[...])
        h_scr[0] = xn.astype(BF16)
        for base in range(0, k // LANES, nhalf):
            for c in range(nhalf):
                xn_scr[c] = xn[:, (base + c) * LANES:(base + c + 1) * LANES]
            for gi, (_, d) in enumerate(DIL_PAIRS):
                if d == 1:
                    continue
                rows = tm // d
                for r in range(d):
                    for c in range(nhalf):
                        sl = slice((base + c) * LANES, (base + c + 1) * LANES)
                        h_scr[gi, r * rows:(r + 1) * rows, sl] = (
                            xn_scr[c, pl.ds(r, rows, stride=d), :].astype(BF16))

    span = tm // ROW_SPLIT

    @pl.when(j < DIL_GROUPS)
    def _():
        cols = pl.ds(pl.multiple_of(j * GROUP_W, LANES), GROUP_W)
        for part in range(ROW_SPLIT):
            rs = slice(part * span, (part + 1) * span)
            y = jnp.dot(h_scr[j, rs, :], w_ref[:, cols], preferred_element_type=F32)
            tc, ta, tb = tab_ref[0, 0, rs, :], tab_ref[0, 1, rs, :], tab_ref[0, 2, rs, :]
            for c in range(GROUP_W // LANES):
                sl = slice(c * LANES, (c + 1) * LANES)
                yc = y[:, sl]
                if c < 2 * DIL_WIDTH // LANES:
                    yc = _rotate3(yc, tc, ta, tb, ROT_DIM // 2)
                dil_ref[rs, sl] = yc.astype(BF16)

    @pl.when(j >= DIL_GROUPS)
    def _():
        start = DIL_GROUPS * GROUP_W + (j - DIL_GROUPS) * PLAIN_W
        cols = pl.ds(pl.multiple_of(start, LANES), PLAIN_W)
        for part in range(ROW_SPLIT):
            rs = slice(part * span, (part + 1) * span)
            plain_ref[rs, :] = jnp.dot(h_scr[0, rs, :], w_ref[:, cols], preferred_element_type=F32)


def _in_proj(x2d, g, w_all, layer, tabs, *, tm, seq):
    m, k = x2d.shape
    nblk = seq // tm
    nplain = (w_all.shape[2] - DIL_GROUPS * GROUP_W) // PLAIN_W
    return pl.pallas_call(
        _in_proj_kernel,
        grid=(m // tm, DIL_GROUPS + nplain),
        in_specs=[
            pl.BlockSpec((tm, k), lambda i, j: (i, 0)),
            pl.BlockSpec((1, k), lambda i, j: (0, 0)),
            pl.BlockSpec((None, k, w_all.shape[2]), lambda i, j: (layer, 0, 0),
                         pipeline_mode=pl.Buffered(1)),
            pl.BlockSpec((1, 3, tm, LANES), lambda i, j: (jnp.minimum(j, DIL_GROUPS - 1), 0, i % nblk, 0)),
        ],
        out_specs=[
            pl.BlockSpec((tm, GROUP_W), lambda i, j: (i, jnp.minimum(j, DIL_GROUPS - 1))),
            pl.BlockSpec((tm, PLAIN_W), lambda i, j: (i, jnp.maximum(j - DIL_GROUPS, 0))),
        ],
        out_shape=[
            jax.ShapeDtypeStruct((m, DIL_GROUPS * GROUP_W), BF16),
            jax.ShapeDtypeStruct((m, nplain * PLAIN_W), F32),
        ],
        scratch_shapes=[pltpu.VMEM((DIL_GROUPS, tm, k), BF16), pltpu.VMEM((k // LANES // 2, tm, LANES), F32)],
        compiler_params=pltpu.CompilerParams(
            dimension_semantics=("arbitrary", "arbitrary"), vmem_limit_bytes=VMEM_LIMIT_IN_PROJ),
        name="in_proj",
    )(x2d, g.reshape(1, k), w_all, tabs)


def _mla_prep_kernel(lat_ref, qg_ref, kvg_ref, wqt_ref, wk_ref, wvt_ref, tqt_ref, tk_ref,
                     qt_out, k_out, vt_out):
    lat = lat_ref[0]
    ts = lat.shape[0]
    cqn = _rms_f32(lat[:, :Q_LORA], qg_ref[...]).astype(BF16)
    ckvn = _rms_f32(lat[:, Q_LORA:Q_LORA + KV_LORA], kvg_ref[...]).astype(BF16)
    kr = lat[:, Q_LORA + KV_LORA:]
    nt = (((1,), (1,)), ((), ()))
    half = MLA_ROPE // 2

    qt = lax.dot_general(wqt_ref[...], cqn, nt, preferred_element_type=F32)
    qc, qa, qb = tqt_ref[0], tqt_ref[1], tqt_ref[2]
    for h in range(MLA_HEADS):
        blk = qt[h * HEAD_PAD:(h + 1) * HEAD_PAD, :]
        up = jnp.concatenate([blk[half:], blk[:half]], axis=0)
        down = jnp.concatenate([blk[-half:], blk[:-half]], axis=0)
        qt_out[0, h] = (blk * qc + up * qa + down * qb).astype(BF16)

    krp = _rotate3(kr, tk_ref[0], tk_ref[1], tk_ref[2], half)
    kk = jnp.dot(ckvn, wk_ref[...], preferred_element_type=F32)
    for h in range(MLA_HEADS):
        k_out[0, h] = (kk[:, h * HEAD_PAD:(h + 1) * HEAD_PAD] + krp).astype(BF16)

    vt = lax.dot_general(wvt_ref[...], ckvn, nt, preferred_element_type=F32)
    row = lax.broadcasted_iota(jnp.int32, (HEAD_PAD, ts), 0)
    ones_row = (row == MLA_V).astype(F32)
    for h in range(MLA_HEADS):
        vt_out[0, h] = (vt[h * HEAD_PAD:(h + 1) * HEAD_PAD, :] + ones_row).astype(BF16)


def _mla_prep(plain, qg, kvg, wqt, wk, wvt, tabs_qt, tabs_k, *, ts):
    b, s, _ = plain.shape
    hp = MLA_HEADS * HEAD_PAD
    const = lambda *shape: pl.BlockSpec(shape, lambda bi, i: (0,) * len(shape))
    rows = pl.BlockSpec((1, MLA_HEADS, ts, HEAD_PAD), lambda bi, i: (bi, 0, i, 0))
    cols = pl.BlockSpec((1, MLA_HEADS, HEAD_PAD, ts), lambda bi, i: (bi, 0, 0, i))
    rows_shape = jax.ShapeDtypeStruct((b, MLA_HEADS, s, HEAD_PAD), BF16)
    cols_shape = jax.ShapeDtypeStruct((b, MLA_HEADS, HEAD_PAD, s), BF16)
    return pl.pallas_call(
        _mla_prep_kernel,
        grid=(b, s // ts),
        in_specs=[
            pl.BlockSpec((1, ts, LAT_WIDTH), lambda bi, i: (bi, i, 0)),
            const(1, Q_LORA), const(1, KV_LORA),
            const(hp, Q_LORA), const(KV_LORA, hp), const(hp, KV_LORA),
            pl.BlockSpec((3, LANES, ts), lambda bi, i: (0, 0, i)),
            pl.BlockSpec((3, ts, LANES), lambda bi, i: (0, i, 0)),
        ],
        out_specs=[cols, rows, cols],
        out_shape=[cols_shape, rows_shape, cols_shape],
        compiler_params=pltpu.CompilerParams(
            dimension_semantics=("arbitrary", "arbitrary"), vmem_limit_bytes=VMEM_LIMIT),
        name="mla_prep",
    )(plain, qg.reshape(1, -1), kvg.reshape(1, -1), wqt, wk, wvt, tabs_qt, tabs_k)


def _flash_kernel(qt_ref, k_ref, vt_ref, o_ref, *, tk, qw):
    tq = qt_ref.shape[3]
    s_len = k_ref.shape[2]
    nsub = tk // SUB_K
    nchunk = s_len // tk
    items = [(qb, hh, c) for qb in range(tq // qw) for hh in range(2) for c in range(nchunk)]

    def scores(item, r):
        qb, hh, c = item
        rows = slice(c * tk + r * SUB_K, c * tk + (r + 1) * SUB_K)
        qt = qt_ref[0, hh, :, qb * qw:(qb + 1) * qw]
        return jnp.dot(k_ref[0, hh, rows, :], qt, preferred_element_type=F32)

    def weighted(item, r, p):
        _, hh, c = item
        cols = slice(c * tk + r * SUB_K, c * tk + (r + 1) * SUB_K)
        return jnp.dot(vt_ref[0, hh, 0:VT_ROWS, cols], p, preferred_element_type=F32)

    def fold(state, item, alpha, pv):
        key = item[:2]
        m, acc = state[key]
        state[key] = (m, alpha * acc + pv)

    state = {}
    s_parts = [scores(items[0], r) for r in range(nsub)]
    prev = None
    for t, item in enumerate(items):
        key = item[:2]
        if key not in state:
            state[key] = (jnp.full((1, qw), NEG_BIG, F32), jnp.zeros((VT_ROWS, qw), F32))
        m, acc = state[key]
        m_new = m
        for s in s_parts:
            m_new = jnp.maximum(m_new, jnp.max(s, axis=0, keepdims=True))
        alpha = jnp.exp2(m - m_new)
        state[key] = (m_new, acc)
        nxt, p_parts, pv = [], [], None
        for r in range(nsub):
            shift = m_new
            if t + 1 < len(items):
                nxt.append(scores(items[t + 1], r))
                shift = m_new + nxt[r][0:1, :] * 0.0
            if prev is not None:
                part = weighted(prev[0], r, prev[1][r])
                pv = part if pv is None else pv + part
            p_parts.append(jnp.exp2(s_parts[r] - shift).astype(BF16))
        if prev is not None:
            fold(state, prev[0], prev[2], pv)
        s_parts, prev = nxt, (item, p_parts, alpha)
    pv = None
    for r in range(nsub):
        part = weighted(prev[0], r, prev[1][r])
        pv = part if pv is None else pv + part
    fold(state, prev[0], prev[2], pv)

    for qb in range(tq // qw):
        outs = []
        for hh in range(2):
            acc = state[(qb, hh)][1]
            outs.append(acc[:MLA_V] / acc[MLA_V:MLA_V + 1])
        o_ref[0, qb * qw:(qb + 1) * qw, :] = jnp.concatenate(outs, axis=0).T.astype(o_ref.dtype)


def _mla_flash(qt, k, vt, *, tq, tk, qw):
    b, h, s, _ = k.shape
    return pl.pallas_call(
        functools.partial(_flash_kernel, tk=tk, qw=qw),
        grid=(b, h // 2, s // tq),
        in_specs=[
            pl.BlockSpec((1, 2, HEAD_PAD, tq), lambda bi, hp, i: (bi, hp, 0, i)),
            pl.BlockSpec((1, 2, s, HEAD_PAD), lambda bi, hp, i: (bi, hp, 0, 0)),
            pl.BlockSpec((1, 2, HEAD_PAD, s), lambda bi, hp, i: (bi, hp, 0, 0)),
        ],
        out_specs=pl.BlockSpec((1, tq, 2 * MLA_V), lambda bi, hp, i: (bi, i, hp)),
        out_shape=jax.ShapeDtypeStruct((b, s, MLA_WIDTH), BF16),
        compiler_params=pltpu.CompilerParams(
            dimension_semantics=("arbitrary", "arbitrary", "arbitrary"),
            vmem_limit_bytes=VMEM_LIMIT),
        name="mla_flash",
    )(qt, k, vt)


def _band_kernel(q_ref, kp_ref, km_ref, kn_ref, vp_ref, vm_ref, vn_ref, o_ref, lse_ref,
                 qwin, kwin, vwin, *, half, length, sub):
    tq = qwin.shape[0]
    nchunk, chunk = q_ref.shape[1], q_ref.shape[3]
    i = pl.program_id(2)
    for c in range(nchunk):
        rows = slice(c * chunk, (c + 1) * chunk)
        qwin[rows] = q_ref[0, c, 0]
        kwin[half + c * chunk:half + (c + 1) * chunk] = km_ref[0, c, 0]
        vwin[half + c * chunk:half + (c + 1) * chunk] = vm_ref[0, c, 0]
    kwin[0:half] = kp_ref[0, 0, 0]
    kwin[half + tq:] = kn_ref[0, 0, 0]
    vwin[0:half] = vp_ref[0, 0, 0]
    vwin[half + tq:] = vn_ref[0, 0, 0]

    wk = sub + 2 * half
    row = lax.broadcasted_iota(jnp.int32, (2 * sub, wk), 0) % sub
    col = lax.broadcasted_iota(jnp.int32, (2 * sub, wk), 1)
    rel = col - half - row
    band = (rel >= -half) & (rel <= half)
    lo = lax.broadcasted_iota(jnp.int32, (sub, LANES), 1) < DIL_HD
    ones = jnp.ones((wk, LANES), BF16)

    def step(t, carry):
        r0 = pl.multiple_of(t * sub, sub)
        kpos = i * tq + r0 - half + col
        mask = band & (kpos >= 0) & (kpos < length)
        for c in range(DIL_WIDTH // LANES):
            cs = slice(c * LANES, (c + 1) * LANES)
            qt = qwin[pl.ds(r0, sub), cs]
            zero = jnp.zeros_like(qt)
            q2 = jnp.concatenate([jnp.where(lo, qt, zero), jnp.where(lo, zero, qt)], axis=0)
            kt = kwin[pl.ds(r0, wk), cs]
            vt = vwin[pl.ds(r0, wk), cs]
            s = lax.dot_general(q2, kt, (((1,), (1,)), ((), ())), preferred_element_type=F32)
            s = jnp.where(mask, s, NEG_BIG)
            m = jnp.max(s, axis=-1, keepdims=True)
            p = jnp.exp2(s - m).astype(BF16)
            ol = jnp.dot(p, jnp.concatenate([vt, ones], axis=1), preferred_element_type=F32)
            o = jnp.where(lo, ol[:sub, :LANES], ol[sub:, :LANES])
            l = jnp.where(lo, ol[:sub, LANES:], ol[sub:, LANES:])
            mm = jnp.where(lo, m[:sub], m[sub:])
            o_ref[0, 0, pl.ds(r0, sub), cs] = (o / l).astype(o_ref.dtype)
            lse_ref[0, 0, pl.ds(r0, sub), cs] = mm + jnp.log2(l)
        return carry

    lax.fori_loop(0, tq // sub, step, 0, unroll=True)


def _band_attention(dil2d, g, *, tm, tq, sub, batch, seq):
    window, d = DIL_PAIRS[g]
    half = window // (2 * d)
    length = seq // d
    tq = min(tq, length)
    chunk = tm // d
    view = dil2d.reshape(batch, seq // tm, d, chunk, dil2d.shape[1])
    rows_blk = min(chunk, tq)
    nchunk = tq // rows_blk

    def main(c):
        col = 3 * g + c
        if chunk >= tq:
            per = chunk // tq
            imap = lambda bi, r, i: (bi, i // per, r, i % per, col)
        else:
            imap = lambda bi, r, i: (bi, i, r, 0, col)
        return pl.BlockSpec((1, nchunk, 1, rows_blk, DIL_WIDTH), imap)

    def edge(c, start_of):
        col = 3 * g + c

        def imap(bi, r, i):
            start = start_of(i)
            return (bi, start // chunk, r, (start % chunk) // half, col)
        return pl.BlockSpec((1, 1, 1, half, DIL_WIDTH), imap)

    before = lambda i: jnp.maximum(i * tq - half, 0)
    after = lambda i: jnp.minimum((i + 1) * tq, length - half)
    out_spec = pl.BlockSpec((1, 1, tq, DIL_WIDTH), lambda bi, r, i: (bi, r, i, 0))
    return pl.pallas_call(
        functools.partial(_band_kernel, half=half, length=length, sub=sub),
        grid=(batch, d, length // tq),
        in_specs=[main(0), edge(1, before), main(1), edge(1, after),
                  edge(2, before), main(2), edge(2, after)],
        out_specs=[out_spec, out_spec],
        out_shape=[jax.ShapeDtypeStruct((batch, d, length, DIL_WIDTH), BF16),
                   jax.ShapeDtypeStruct((batch, d, length, DIL_WIDTH), F32)],
        scratch_shapes=[pltpu.VMEM((tq, DIL_WIDTH), BF16),
                        pltpu.VMEM((tq + 2 * half, DIL_WIDTH), BF16),
                        pltpu.VMEM((tq + 2 * half, DIL_WIDTH), BF16)],
        compiler_params=pltpu.CompilerParams(
            dimension_semantics=("arbitrary", "arbitrary", "arbitrary"),
            vmem_limit_bytes=VMEM_LIMIT),
        name=f"band_attn_d{d}",
    )(view, view, view, view, view, view, view)


def _silu(g):
    return g / (1.0 + jnp.exp(-g))


def _out_kernel(x_ref, a_ref, gate_ref, o1, l1, o2, l2, o3, l3, w_ref, fg_ref, y_ref, nat_scr,
                *, final):
    tm = x_ref.shape[0]

    def natural(ref, slot):
        d = ref.shape[1]
        if d == 1:
            return ref[0, 0]
        ntile = ref.shape[3] // LANES
        for r in range(d):
            for c in range(ntile):
                nat_scr[slot * ntile + c, pl.ds(r, tm // d, stride=d), :] = (
                    ref[0, r, :, c * LANES:(c + 1) * LANES].astype(F32))
        return jnp.concatenate([nat_scr[slot * ntile + c] for c in range(ntile)], axis=1)

    la, lb, lc = l1[0, 0], natural(l2, 0), natural(l3, 1)
    mx = jnp.maximum(jnp.maximum(la, lb), lc)
    ea, eb, ec = jnp.exp2(la - mx), jnp.exp2(lb - mx), jnp.exp2(lc - mx)
    bmix = (ea * o1[0, 0] + eb * natural(o2, 2) + ec * natural(o3, 3)) / (ea + eb + ec)
    gates = gate_ref[...]
    mix = jnp.concatenate([a_ref[...] * _silu(gates[:, :MLA_WIDTH]),
                           bmix * _silu(gates[:, MLA_WIDTH:])], axis=-1).astype(BF16)
    y = x_ref[...] + jnp.dot(mix, w_ref[...], preferred_element_type=F32)
    if final:
        y = _rms_f32(y, fg_ref[...])
    y_ref[...] = y


def _out_proj(x2d, a2d, plain, groups, w_out_all, layer, final_g, *, tm, seq, final):
    m, dm = x2d.shape
    nblk = seq // tm
    row = lambda n: pl.BlockSpec((tm, n), lambda i: (i, 0))
    gate_blk = plain.shape[1] // PLAIN_W - 1
    args = [x2d, a2d, plain]
    specs = [row(dm), row(MLA_WIDTH), pl.BlockSpec((tm, PLAIN_W), lambda i: (i, gate_blk))]
    for (_, d), (o, lse) in zip(DIL_PAIRS, groups):
        spec = pl.BlockSpec((1, d, tm // d, DIL_WIDTH), lambda i: (i // nblk, 0, i % nblk, 0))
        args += [o, lse]
        specs += [spec, spec]
    args += [w_out_all, final_g.reshape(1, dm)]
    specs += [pl.BlockSpec((None, dm, dm), lambda i: (layer, 0, 0)),
              pl.BlockSpec((1, dm), lambda i: (0, 0))]
    return pl.pallas_call(
        functools.partial(_out_kernel, final=final),
        grid=(m // tm,),
        in_specs=specs,
        out_specs=row(dm),
        out_shape=jax.ShapeDtypeStruct((m, dm), F32),
        scratch_shapes=[pltpu.VMEM((4 * DIL_WIDTH // LANES, tm, LANES), F32)],
        compiler_params=pltpu.CompilerParams(
            dimension_semantics=("arbitrary",), vmem_limit_bytes=VMEM_LIMIT),
        name="merge_out_proj",
    )(*args)


def _arrange_w_in(w):
    o_kr = Q_LORA + KV_LORA
    o_ga = o_kr + MLA_ROPE
    o_dil = o_ga + MLA_WIDTH
    o_gb = o_dil + DIL_GROUPS * GROUP_W
    z = lambda n: jnp.zeros(w.shape[:2] + (n,), w.dtype)
    parts = [w[..., o_dil:o_gb],
             w[..., :o_kr], z(MLA_NOPE), w[..., o_kr:o_ga], z(HEAD_PAD - MLA_NOPE - MLA_ROPE),
             z(PLAIN_W - LAT_WIDTH),
             w[..., o_ga:o_dil], w[..., o_gb:]]
    return jnp.concatenate(parts, axis=-1).astype(BF16)


def _pad_heads(w, per_head, keep):
    lead = w.shape[:-1]
    wh = w.reshape(*lead, MLA_HEADS, per_head)[..., keep]
    wh = jnp.pad(wh, [(0, 0)] * (wh.ndim - 1) + [(0, HEAD_PAD - wh.shape[-1])])
    return wh.reshape(*lead, MLA_HEADS * HEAD_PAD)


def kernel(x, norm_g, w_in, q_norm_g, kv_norm_g, w_uq, w_ukv, w_out, final_g):
    b, s, dm = x.shape
    m = b * s
    tm_in = 1024
    mla_scale = (MLA_NOPE + MLA_ROPE) ** -0.5 * LOG2E
    dil_scale = DIL_HD ** -0.5 * LOG2E
    tabs_mqt = _mla_tables(s, mla_scale).transpose(0, 2, 1)
    tabs_mk = _mla_tables(s, 1.0)
    tabs_d = _dil_tables(s, math.sqrt(dil_scale), tm_in)

    w_in_all = _arrange_w_in(w_in)
    wqt_all = _pad_heads(w_uq, MLA_NOPE + MLA_ROPE, slice(None)).astype(BF16).swapaxes(1, 2)
    wk_all = _pad_heads(w_ukv, MLA_NOPE + MLA_V, slice(0, MLA_NOPE)).astype(BF16)
    wvt_all = _pad_heads(w_ukv, MLA_NOPE + MLA_V, slice(MLA_NOPE, None)).astype(BF16).swapaxes(1, 2)
    w_out_all = w_out.astype(BF16)

    x2d = x.reshape(m, dm)
    for layer in range(DEPTH):
        dil, plain = _in_proj(x2d, norm_g[layer], w_in_all, layer, tabs_d, tm=tm_in, seq=s)

        qt, kk, vt = _mla_prep(plain.reshape(b, s, -1), q_norm_g[layer], kv_norm_g[layer],
                               wqt_all[layer], wk_all[layer], wvt_all[layer], tabs_mqt, tabs_mk, ts=1024)
        a = _mla_flash(qt, kk, vt, tq=512, tk=2048, qw=512)

        groups = [_band_attention(dil, gi, tm=tm_in, tq=512, sub=128, batch=b, seq=s)
                  for gi in range(DIL_GROUPS)]

        x2d = _out_proj(x2d, a.reshape(m, MLA_WIDTH), plain, groups, w_out_all, layer,
                        final_g, tm=512, seq=s, final=(layer == DEPTH - 1))
    return x2d.reshape(b, s, dm)
```

```python
import functools
import math

import numpy as np
import jax
import jax.numpy as jnp
from jax import lax
from jax.experimental import pallas as pl
from jax.experimental.pallas import tpu as pltpu

F32 = jnp.float32
BF16 = jnp.bfloat16

D_MODEL = 1024
DEPTH = 4
MLA_HEADS = 8
MLA_NOPE = 64
MLA_ROPE = 32
MLA_V = 64
Q_LORA = 384
KV_LORA = 256
MLA_WIDTH = MLA_HEADS * MLA_V
DIL_PAIRS = ((128, 1), (512, 4), (2048, 16))
DIL_GROUPS = 3
DIL_HEADS = 8
DIL_HD = 64
DIL_WIDTH = DIL_HEADS * DIL_HD
ROT_DIM = DIL_HD // 4
ROPE_THETA = 500000.0
EPS = 1e-6

LANES = 128
BF16_ROWS = 16
HEAD_PAD = 128
LAT_WIDTH = Q_LORA + KV_LORA + HEAD_PAD
VT_ROWS = -(-(MLA_V + 1) // BF16_ROWS) * BF16_ROWS
GROUP_W = 3 * DIL_WIDTH
ROW_SPLIT = 4
SUB_K = 256
LOG2E = math.log2(math.e)
NEG_BIG = -1e30
VMEM_LIMIT = 48 * 1024 * 1024
VMEM_LIMIT_IN_PROJ = 58 * 1024 * 1024


def _rope_tables(seq, dim):
    inv = 1.0 / (ROPE_THETA ** (jnp.arange(0, dim, 2, dtype=F32) / dim))
    ang = jnp.arange(seq, dtype=F32)[:, None] * inv[None, :]
    return jnp.cos(ang), jnp.sin(ang)


def _mla_tables(seq, scale):
    cos, sin = _rope_tables(seq, MLA_ROPE)
    half = MLA_ROPE // 2
    ones = jnp.ones((seq, MLA_NOPE), F32)
    z = lambda n: jnp.zeros((seq, n), F32)
    c = jnp.concatenate([ones, cos, cos, z(HEAD_PAD - MLA_NOPE - MLA_ROPE)], axis=1)
    a = jnp.concatenate([z(MLA_NOPE), -sin, z(half), z(HEAD_PAD - MLA_NOPE - MLA_ROPE)], axis=1)
    b = jnp.concatenate([z(MLA_NOPE), z(half), sin, z(HEAD_PAD - MLA_NOPE - MLA_ROPE)], axis=1)
    return jnp.stack([c, a, b]) * scale


def _dil_tables(seq, scale, tm):
    half = ROT_DIM // 2
    rest = DIL_HD - ROT_DIM
    rep = LANES // DIL_HD
    inv = 1.0 / (ROPE_THETA ** (jnp.arange(0, ROT_DIM, 2, dtype=F32) / ROT_DIM))
    z = lambda n: jnp.zeros((seq, n), F32)
    out = []
    for _, d in DIL_PAIRS:
        pos = np.arange(seq).reshape(seq // tm, tm // d, d).transpose(0, 2, 1).reshape(seq)
        ang = jnp.asarray(pos, F32)[:, None] * inv[None, :]
        cos, sin = jnp.cos(ang) * scale, jnp.sin(ang) * scale
        c = jnp.concatenate([cos, cos, jnp.full((seq, rest), scale, F32)], axis=1)
        a = jnp.concatenate([-sin, z(half), z(rest)], axis=1)
        b = jnp.concatenate([z(half), sin, z(rest)], axis=1)
        out.append(jnp.stack([jnp.tile(c, (1, rep)), jnp.tile(a, (1, rep)), jnp.tile(b, (1, rep))]))
    return jnp.stack(out)


def _rotate3(blk, tab_c, tab_a, tab_b, shift):
    return (blk * tab_c
            + pltpu.roll(blk, LANES - shift, 1) * tab_a
            + pltpu.roll(blk, shift, 1) * tab_b)


def _rms_f32(x, g):
    ms = jnp.mean(x * x, axis=-1, keepdims=True)
    return x * lax.rsqrt(ms + EPS) * g


def _in_proj_kernel(x_ref, g_ref, w_ref, tab_ref, dil_ref, lat_ref, gate_ref, h_scr, xn_scr):
    tm, k = x_ref.shape
    j = pl.program_id(1)
    nhalf = xn_scr.shape[0]

    @pl.when(j == 0)
    def _():
        xn = _rms_f32(x_ref[...], g_ref[...])
        h_scr[0] = xn.astype(BF16)
        for base in range(0, k // LANES, nhalf):
            for c in range(nhalf):
                xn_scr[c] = xn[:, (base + c) * LANES:(base + c + 1) * LANES]
            for gi, (_, d) in enumerate(DIL_PAIRS):
                if d == 1:
                    continue
                rows = tm // d
                for r in range(d):
                    for c in range(nhalf):
                        sl = slice((base + c) * LANES, (base + c + 1) * LANES)
                        h_scr[gi, r * rows:(r + 1) * rows, sl] = (
                            xn_scr[c, pl.ds(r, rows, stride=d), :].astype(BF16))

    span = tm // ROW_SPLIT

    @pl.when(j < DIL_GROUPS)
    def _():
        cols = pl.ds(pl.multiple_of(j * GROUP_W, LANES), GROUP_W)
        for part in range(ROW_SPLIT):
            rs = slice(part * span, (part + 1) * span)
            y = jnp.dot(h_scr[j, rs, :], w_ref[:, cols], preferred_element_type=F32)
            tc, ta, tb = tab_ref[0, 0, rs, :], tab_ref[0, 1, rs, :], tab_ref[0, 2, rs, :]
            for c in range(GROUP_W // LANES):
                sl = slice(c * LANES, (c + 1) * LANES)
                yc = y[:, sl]
                if c < 2 * DIL_WIDTH // LANES:
                    yc = _rotate3(yc, tc, ta, tb, ROT_DIM // 2)
                dil_ref[rs, sl] = yc.astype(BF16)

    lat_start = DIL_GROUPS * GROUP_W
    for step, ref, start in ((DIL_GROUPS, lat_ref, lat_start),
                             (DIL_GROUPS + 1, gate_ref, lat_start + LAT_WIDTH)):
        @pl.when(j == step)
        def _(ref=ref, start=start):
            cols = slice(start, start + ref.shape[1])
            for part in range(ROW_SPLIT):
                rs = slice(part * span, (part + 1) * span)
                y = jnp.dot(h_scr[0, rs, :], w_ref[:, cols], preferred_element_type=F32)
                ref[rs, :] = y.astype(ref.dtype)


def _in_proj(x2d, g, w_all, layer, tabs, *, tm, seq):
    m, k = x2d.shape
    nblk = seq // tm
    return pl.pallas_call(
        _in_proj_kernel,
        grid=(m // tm, DIL_GROUPS + 2),
        in_specs=[
            pl.BlockSpec((tm, k), lambda i, j: (i, 0)),
            pl.BlockSpec((1, k), lambda i, j: (0, 0)),
            pl.BlockSpec((None, k, w_all.shape[2]), lambda i, j: (layer, 0, 0),
                         pipeline_mode=pl.Buffered(1)),
            pl.BlockSpec((1, 3, tm, LANES), lambda i, j: (jnp.minimum(j, DIL_GROUPS - 1), 0, i % nblk, 0)),
        ],
        out_specs=[
            pl.BlockSpec((tm, GROUP_W), lambda i, j: (i, jnp.minimum(j, DIL_GROUPS - 1))),
            pl.BlockSpec((tm, LAT_WIDTH), lambda i, j: (i, 0)),
            pl.BlockSpec((tm, 2 * MLA_WIDTH), lambda i, j: (i, 0)),
        ],
        out_shape=[
            jax.ShapeDtypeStruct((m, DIL_GROUPS * GROUP_W), BF16),
            jax.ShapeDtypeStruct((m, LAT_WIDTH), F32),
            jax.ShapeDtypeStruct((m, 2 * MLA_WIDTH), BF16),
        ],
        scratch_shapes=[pltpu.VMEM((DIL_GROUPS, tm, k), BF16), pltpu.VMEM((k // LANES // 2, tm, LANES), F32)],
        compiler_params=pltpu.CompilerParams(
            dimension_semantics=("arbitrary", "arbitrary"), vmem_limit_bytes=VMEM_LIMIT_IN_PROJ),
        name="in_proj",
    )(x2d, g.reshape(1, k), w_all, tabs)


def _mla_prep_kernel(lat_ref, qg_ref, kvg_ref, wqt_ref, wk_ref, wvt_ref, tqt_ref, tk_ref,
                     qt_out, k_out, vt_out):
    lat = lat_ref[0]
    ts = lat.shape[0]
    cqn = _rms_f32(lat[:, :Q_LORA], qg_ref[...]).astype(BF16)
    ckvn = _rms_f32(lat[:, Q_LORA:Q_LORA + KV_LORA], kvg_ref[...]).astype(BF16)
    kr = lat[:, Q_LORA + KV_LORA:]
    nt = (((1,), (1,)), ((), ()))
    half = MLA_ROPE // 2

    qt = lax.dot_general(wqt_ref[...], cqn, nt, preferred_element_type=F32)
    qc, qa, qb = tqt_ref[0], tqt_ref[1], tqt_ref[2]
    for h in range(MLA_HEADS):
        blk = qt[h * HEAD_PAD:(h + 1) * HEAD_PAD, :]
        up = jnp.concatenate([blk[half:], blk[:half]], axis=0)
        down = jnp.concatenate([blk[-half:], blk[:-half]], axis=0)
        qt_out[0, h] = (blk * qc + up * qa + down * qb).astype(BF16)

    krp = _rotate3(kr, tk_ref[0], tk_ref[1], tk_ref[2], half)
    kk = jnp.dot(ckvn, wk_ref[...], preferred_element_type=F32)
    for h in range(MLA_HEADS):
        k_out[0, h] = (kk[:, h * HEAD_PAD:(h + 1) * HEAD_PAD] + krp).astype(BF16)

    vt = lax.dot_general(wvt_ref[...], ckvn, nt, preferred_element_type=F32)
    row = lax.broadcasted_iota(jnp.int32, (HEAD_PAD, ts), 0)
    ones_row = (row == MLA_V).astype(F32)
    for h in range(MLA_HEADS):
        vt_out[0, h] = (vt[h * HEAD_PAD:(h + 1) * HEAD_PAD, :] + ones_row).astype(BF16)


def _mla_prep(lat, qg, kvg, wqt, wk, wvt, tabs_qt, tabs_k, *, ts):
    b, s, _ = lat.shape
    hp = MLA_HEADS * HEAD_PAD
    const = lambda *shape: pl.BlockSpec(shape, lambda bi, i: (0,) * len(shape))
    rows = pl.BlockSpec((1, MLA_HEADS, ts, HEAD_PAD), lambda bi, i: (bi, 0, i, 0))
    cols = pl.BlockSpec((1, MLA_HEADS, HEAD_PAD, ts), lambda bi, i: (bi, 0, 0, i))
    rows_shape = jax.ShapeDtypeStruct((b, MLA_HEADS, s, HEAD_PAD), BF16)
    cols_shape = jax.ShapeDtypeStruct((b, MLA_HEADS, HEAD_PAD, s), BF16)
    return pl.pallas_call(
        _mla_prep_kernel,
        grid=(b, s // ts),
        in_specs=[
            pl.BlockSpec((1, ts, LAT_WIDTH), lambda bi, i: (bi, i, 0)),
            const(1, Q_LORA), const(1, KV_LORA),
            const(hp, Q_LORA), const(KV_LORA, hp), const(hp, KV_LORA),
            pl.BlockSpec((3, LANES, ts), lambda bi, i: (0, 0, i)),
            pl.BlockSpec((3, ts, LANES), lambda bi, i: (0, i, 0)),
        ],
        out_specs=[cols, rows, cols],
        out_shape=[cols_shape, rows_shape, cols_shape],
        compiler_params=pltpu.CompilerParams(
            dimension_semantics=("arbitrary", "arbitrary"), vmem_limit_bytes=VMEM_LIMIT),
        name="mla_prep",
    )(lat, qg.reshape(1, -1), kvg.reshape(1, -1), wqt, wk, wvt, tabs_qt, tabs_k)


def _flash_kernel(qt_ref, k_ref, vt_ref, o_ref, *, tk, qw):
    tq = qt_ref.shape[3]
    s_len = k_ref.shape[2]
    nsub = tk // SUB_K
    nchunk = s_len // tk
    items = [(qb, hh, c) for qb in range(tq // qw) for hh in range(2) for c in range(nchunk)]

    def scores(item, r):
        qb, hh, c = item
        rows = slice(c * tk + r * SUB_K, c * tk + (r + 1) * SUB_K)
        qt = qt_ref[0, hh, :, qb * qw:(qb + 1) * qw]
        return jnp.dot(k_ref[0, hh, rows, :], qt, preferred_element_type=F32)

    def weighted(item, r, p):
        _, hh, c = item
        cols = slice(c * tk + r * SUB_K, c * tk + (r + 1) * SUB_K)
        return jnp.dot(vt_ref[0, hh, 0:VT_ROWS, cols], p, preferred_element_type=F32)

    def fold(state, item, alpha, pv):
        key = item[:2]
        m, acc = state[key]
        state[key] = (m, alpha * acc + pv)

    state = {}
    s_parts = [scores(items[0], r) for r in range(nsub)]
    prev = None
    for t, item in enumerate(items):
        key = item[:2]
        if key not in state:
            state[key] = (jnp.full((1, qw), NEG_BIG, F32), jnp.zeros((VT_ROWS, qw), F32))
        m, acc = state[key]
        m_new = m
        for s in s_parts:
            m_new = jnp.maximum(m_new, jnp.max(s, axis=0, keepdims=True))
        alpha = jnp.exp2(m - m_new)
        state[key] = (m_new, acc)
        nxt, p_parts, pv = [], [], None
        for r in range(nsub):
            shift = m_new
            if t + 1 < len(items):
                nxt.append(scores(items[t + 1], r))
                shift = m_new + nxt[r][0:1, :] * 0.0
            if prev is not None:
                part = weighted(prev[0], r, prev[1][r])
                pv = part if pv is None else pv + part
            p_parts.append(jnp.exp2(s_parts[r] - shift).astype(BF16))
        if prev is not None:
            fold(state, prev[0], prev[2], pv)
        s_parts, prev = nxt, (item, p_parts, alpha)
    pv = None
    for r in range(nsub):
        part = weighted(prev[0], r, prev[1][r])
        pv = part if pv is None else pv + part
    fold(state, prev[0], prev[2], pv)

    for qb in range(tq // qw):
        outs = []
        for hh in range(2):
            acc = state[(qb, hh)][1]
            outs.append(acc[:MLA_V] / acc[MLA_V:MLA_V + 1])
        o_ref[0, qb * qw:(qb + 1) * qw, :] = jnp.concatenate(outs, axis=0).T.astype(o_ref.dtype)


def _mla_flash(qt, k, vt, *, tq, tk, qw):
    b, h, s, _ = k.shape
    return pl.pallas_call(
        functools.partial(_flash_kernel, tk=tk, qw=qw),
        grid=(b, h // 2, s // tq),
        in_specs=[
            pl.BlockSpec((1, 2, HEAD_PAD, tq), lambda bi, hp, i: (bi, hp, 0, i)),
            pl.BlockSpec((1, 2, s, HEAD_PAD), lambda bi, hp, i: (bi, hp, 0, 0)),
            pl.BlockSpec((1, 2, HEAD_PAD, s), lambda bi, hp, i: (bi, hp, 0, 0)),
        ],
        out_specs=pl.BlockSpec((1, tq, 2 * MLA_V), lambda bi, hp, i: (bi, i, hp)),
        out_shape=jax.ShapeDtypeStruct((b, s, MLA_WIDTH), BF16),
        compiler_params=pltpu.CompilerParams(
            dimension_semantics=("arbitrary", "arbitrary", "arbitrary"),
            vmem_limit_bytes=VMEM_LIMIT),
        name="mla_flash",
    )(qt, k, vt)


def _band_kernel(q_ref, kp_ref, km_ref, kn_ref, vp_ref, vm_ref, vn_ref, o_ref, lse_ref,
                 qwin, kwin, vwin, *, half, sub):
    tq = qwin.shape[0]
    nchunk, chunk = q_ref.shape[1], q_ref.shape[3]
    i = pl.program_id(2)
    for c in range(nchunk):
        rows = slice(c * chunk, (c + 1) * chunk)
        qwin[rows] = q_ref[0, c, 0]
        kwin[half + c * chunk:half + (c + 1) * chunk] = km_ref[0, c, 0]
        vwin[half + c * chunk:half + (c + 1) * chunk] = vm_ref[0, c, 0]
    kwin[0:half] = kp_ref[0, 0, 0]
    kwin[half + tq:] = kn_ref[0, 0, 0]
    vwin[0:half] = vp_ref[0, 0, 0]
    vwin[half + tq:] = vn_ref[0, 0, 0]

    wk = sub + 2 * half
    lo = lax.broadcasted_iota(jnp.int32, (sub, LANES), 1) < DIL_HD
    ones = jnp.ones((wk, LANES), BF16)
    row = lax.broadcasted_iota(jnp.int32, (2 * sub, wk), 0) % sub
    col = lax.broadcasted_iota(jnp.int32, (2 * sub, wk), 1)
    rel = col - half - row
    in_band = (rel >= -half) & (rel <= half)
    bias_mid = jnp.where(in_band, 0.0, NEG_BIG)
    bias_first = jnp.where(in_band & (col >= half), 0.0, NEG_BIG)
    bias_last = jnp.where(in_band & (col < half + sub), 0.0, NEG_BIG)
    nstep = tq // sub
    assert nstep >= 2
    at_start = i == 0
    at_end = i == pl.num_programs(2) - 1

    for t in range(nstep):
        r0 = t * sub
        bias = bias_mid
        if t == 0:
            bias = jnp.where(at_start, bias_first, bias)
        if t == nstep - 1:
            bias = jnp.where(at_end, bias_last, bias)
        for c in range(DIL_WIDTH // LANES):
            cs = slice(c * LANES, (c + 1) * LANES)
            qt = qwin[r0:r0 + sub, cs]
            zero = jnp.zeros_like(qt)
            q2 = jnp.concatenate([jnp.where(lo, qt, zero), jnp.where(lo, zero, qt)], axis=0)
            kt = kwin[r0:r0 + wk, cs]
            vt = vwin[r0:r0 + wk, cs]
            s = lax.dot_general(q2, kt, (((1,), (1,)), ((), ())), preferred_element_type=F32) + bias
            m = jnp.max(s, axis=-1, keepdims=True)
            p = jnp.exp2(s - m).astype(BF16)
            ol = jnp.dot(p, jnp.concatenate([vt, ones], axis=1), preferred_element_type=F32)
            o = jnp.where(lo, ol[:sub, :LANES], ol[sub:, :LANES])
            l = jnp.where(lo, ol[:sub, LANES:], ol[sub:, LANES:])
            mm = jnp.where(lo, m[:sub], m[sub:])
            o_ref[0, 0, r0:r0 + sub, cs] = (o / l).astype(o_ref.dtype)
            lse_ref[0, 0, r0:r0 + sub, cs] = mm + jnp.log2(l)


def _band_attention(dil2d, g, *, tm, tq, sub, batch, seq):
    window, d = DIL_PAIRS[g]
    half = window // (2 * d)
    length = seq // d
    tq = min(tq, length)
    chunk = tm // d
    view = dil2d.reshape(batch, seq // tm, d, chunk, dil2d.shape[1])
    rows_blk = min(chunk, tq)
    nchunk = tq // rows_blk

    def main(c):
        col = 3 * g + c
        if chunk >= tq:
            per = chunk // tq
            imap = lambda bi, r, i: (bi, i // per, r, i % per, col)
        else:
            imap = lambda bi, r, i: (bi, i, r, 0, col)
        return pl.BlockSpec((1, nchunk, 1, rows_blk, DIL_WIDTH), imap)

    def edge(c, start_of):
        col = 3 * g + c

        def imap(bi, r, i):
            start = start_of(i)
            return (bi, start // chunk, r, (start % chunk) // half, col)
        return pl.BlockSpec((1, 1, 1, half, DIL_WIDTH), imap)

    before = lambda i: jnp.maximum(i * tq - half, 0)
    after = lambda i: jnp.minimum((i + 1) * tq, length - half)
    out_spec = pl.BlockSpec((1, 1, tq, DIL_WIDTH), lambda bi, r, i: (bi, r, i, 0))
    return pl.pallas_call(
        functools.partial(_band_kernel, half=half, sub=sub),
        grid=(batch, d, length // tq),
        in_specs=[main(0), edge(1, before), main(1), edge(1, after),
                  edge(2, before), main(2), edge(2, after)],
        out_specs=[out_spec, out_spec],
        out_shape=[jax.ShapeDtypeStruct((batch, d, length, DIL_WIDTH), BF16),
                   jax.ShapeDtypeStruct((batch, d, length, DIL_WIDTH), F32)],
        scratch_shapes=[pltpu.VMEM((tq, DIL_WIDTH), BF16),
                        pltpu.VMEM((tq + 2 * half, DIL_WIDTH), BF16),
                        pltpu.VMEM((tq + 2 * half, DIL_WIDTH), BF16)],
        compiler_params=pltpu.CompilerParams(
            dimension_semantics=("arbitrary", "arbitrary", "arbitrary"),
            vmem_limit_bytes=VMEM_LIMIT),
        name=f"band_attn_d{d}",
    )(view, view, view, view, view, view, view)


def _silu(g):
    return g / (1.0 + jnp.exp(-g))


def _out_kernel(x_ref, a_ref, gate_ref, o1, l1, o2, l2, o3, l3, w_ref, fg_ref, y_ref, nat_scr,
                *, final):
    tm = x_ref.shape[0]

    def natural(ref, slot):
        d = ref.shape[1]
        if d == 1:
            return ref[0, 0]
        ntile = ref.shape[3] // LANES
        for r in range(d):
            for c in range(ntile):
                nat_scr[slot * ntile + c, pl.ds(r, tm // d, stride=d), :] = (
                    ref[0, r, :, c * LANES:(c + 1) * LANES].astype(F32))
        return jnp.concatenate([nat_scr[slot * ntile + c] for c in range(ntile)], axis=1)

    la, lb, lc = l1[0, 0], natural(l2, 0), natural(l3, 1)
    mx = jnp.maximum(jnp.maximum(la, lb), lc)
    ea, eb, ec = jnp.exp2(la - mx), jnp.exp2(lb - mx), jnp.exp2(lc - mx)
    bmix = (ea * o1[0, 0] + eb * natural(o2, 2) + ec * natural(o3, 3)) / (ea + eb + ec)
    gates = gate_ref[...].astype(F32)
    mix = jnp.concatenate([a_ref[...] * _silu(gates[:, :MLA_WIDTH]),
                           bmix * _silu(gates[:, MLA_WIDTH:])], axis=-1).astype(BF16)
    y = x_ref[...] + jnp.dot(mix, w_ref[...], preferred_element_type=F32)
    if final:
        y = _rms_f32(y, fg_ref[...])
    y_ref[...] = y


def _out_proj(x2d, a2d, gates, groups, w_out_all, layer, final_g, *, tm, seq, final):
    m, dm = x2d.shape
    nblk = seq // tm
    row = lambda n: pl.BlockSpec((tm, n), lambda i: (i, 0))
    args = [x2d, a2d, gates]
    specs = [row(dm), row(MLA_WIDTH), row(2 * MLA_WIDTH)]
    for (_, d), (o, lse) in zip(DIL_PAIRS, groups):
        spec = pl.BlockSpec((1, d, tm // d, DIL_WIDTH), lambda i: (i // nblk, 0, i % nblk, 0))
        args += [o, lse]
        specs += [spec, spec]
    args += [w_out_all, final_g.reshape(1, dm)]
    specs += [pl.BlockSpec((None, dm, dm), lambda i: (layer, 0, 0)),
              pl.BlockSpec((1, dm), lambda i: (0, 0))]
    return pl.pallas_call(
        functools.partial(_out_kernel, final=final),
        grid=(m // tm,),
        in_specs=specs,
        out_specs=row(dm),
        out_shape=jax.ShapeDtypeStruct((m, dm), F32),
        scratch_shapes=[pltpu.VMEM((4 * DIL_WIDTH // LANES, tm, LANES), F32)],
        compiler_params=pltpu.CompilerParams(
            dimension_semantics=("arbitrary",), vmem_limit_bytes=VMEM_LIMIT),
        name="merge_out_proj",
    )(*args)


def _arrange_w_in(w):
    o_kr = Q_LORA + KV_LORA
    o_ga = o_kr + MLA_ROPE
    o_dil = o_ga + MLA_WIDTH
    o_gb = o_dil + DIL_GROUPS * GROUP_W
    z = lambda n: jnp.zeros(w.shape[:2] + (n,), w.dtype)
    parts = [w[..., o_dil:o_gb],
             w[..., :o_kr], z(MLA_NOPE), w[..., o_kr:o_ga], z(HEAD_PAD - MLA_NOPE - MLA_ROPE),
             w[..., o_ga:o_dil], w[..., o_gb:]]
    return jnp.concatenate(parts, axis=-1).astype(BF16)


def _pad_heads(w, per_head, keep):
    lead = w.shape[:-1]
    wh = w.reshape(*lead, MLA_HEADS, per_head)[..., keep]
    wh = jnp.pad(wh, [(0, 0)] * (wh.ndim - 1) + [(0, HEAD_PAD - wh.shape[-1])])
    return wh.reshape(*lead, MLA_HEADS * HEAD_PAD)


def kernel(x, norm_g, w_in, q_norm_g, kv_norm_g, w_uq, w_ukv, w_out, final_g):
    b, s, dm = x.shape
    m = b * s
    tm_in = 1024
    mla_scale = (MLA_NOPE + MLA_ROPE) ** -0.5 * LOG2E
    dil_scale = DIL_HD ** -0.5 * LOG2E
    tabs_mqt = _mla_tables(s, mla_scale).transpose(0, 2, 1)
    tabs_mk = _mla_tables(s, 1.0)
    tabs_d = _dil_tables(s, math.sqrt(dil_scale), tm_in)

    w_in_all = _arrange_w_in(w_in)
    wqt_all = _pad_heads(w_uq, MLA_NOPE + MLA_ROPE, slice(None)).astype(BF16).swapaxes(1, 2)
    wk_all = _pad_heads(w_ukv, MLA_NOPE + MLA_V, slice(0, MLA_NOPE)).astype(BF16)
    wvt_all = _pad_heads(w_ukv, MLA_NOPE + MLA_V, slice(MLA_NOPE, None)).astype(BF16).swapaxes(1, 2)
    w_out_all = w_out.astype(BF16)

    x2d = x.reshape(m, dm)
    for layer in range(DEPTH):
        dil, lat, gates = _in_proj(x2d, norm_g[layer], w_in_all, layer, tabs_d, tm=tm_in, seq=s)

        qt, kk, vt = _mla_prep(lat.reshape(b, s, -1), q_norm_g[layer], kv_norm_g[layer],
                               wqt_all[layer], wk_all[layer], wvt_all[layer], tabs_mqt, tabs_mk, ts=1024)
        a = _mla_flash(qt, kk, vt, tq=512, tk=2048, qw=512)

        groups = [_band_attention(dil, gi, tm=tm_in, tq=512, sub=128, batch=b, seq=s)
                  for gi in range(DIL_GROUPS)]

        x2d = _out_proj(x2d, a.reshape(m, MLA_WIDTH), gates, groups, w_out_all, layer,
                        final_g, tm=512, seq=s, final=(layer == DEPTH - 1))
    return x2d.reshape(b, s, dm)
```

```python
import functools
import math

import numpy as np
import jax
import jax.numpy as jnp
from jax import lax
from jax.experimental import pallas as pl
from jax.experimental.pallas import tpu as pltpu

F32 = jnp.float32
BF16 = jnp.bfloat16

D_MODEL = 1024
DEPTH = 4
MLA_HEADS = 8
MLA_NOPE = 64
MLA_ROPE = 32
MLA_V = 64
Q_LORA = 384
KV_LORA = 256
MLA_WIDTH = MLA_HEADS * MLA_V
DIL_PAIRS = ((128, 1), (512, 4), (2048, 16))
DIL_GROUPS = 3
DIL_HEADS = 8
DIL_HD = 64
DIL_WIDTH = DIL_HEADS * DIL_HD
ROT_DIM = DIL_HD // 4
ROPE_THETA = 500000.0
EPS = 1e-6

LANES = 128
BF16_ROWS = 16
HEAD_PAD = 128
LAT_WIDTH = Q_LORA + KV_LORA + HEAD_PAD
VT_ROWS = -(-(MLA_V + 1) // BF16_ROWS) * BF16_ROWS
GROUP_W = 3 * DIL_WIDTH
ROW_SPLIT = 4
SUB_K = 256
LOG2E = math.log2(math.e)
NEG_BIG = -1e30
VMEM_LIMIT = 48 * 1024 * 1024
VMEM_LIMIT_IN_PROJ = 58 * 1024 * 1024


def _rope_tables(seq, dim):
    inv = 1.0 / (ROPE_THETA ** (jnp.arange(0, dim, 2, dtype=F32) / dim))
    ang = jnp.arange(seq, dtype=F32)[:, None] * inv[None, :]
    return jnp.cos(ang), jnp.sin(ang)


def _mla_tables(seq, scale):
    cos, sin = _rope_tables(seq, MLA_ROPE)
    half = MLA_ROPE // 2
    ones = jnp.ones((seq, MLA_NOPE), F32)
    z = lambda n: jnp.zeros((seq, n), F32)
    c = jnp.concatenate([ones, cos, cos, z(HEAD_PAD - MLA_NOPE - MLA_ROPE)], axis=1)
    a = jnp.concatenate([z(MLA_NOPE), -sin, z(half), z(HEAD_PAD - MLA_NOPE - MLA_ROPE)], axis=1)
    b = jnp.concatenate([z(MLA_NOPE), z(half), sin, z(HEAD_PAD - MLA_NOPE - MLA_ROPE)], axis=1)
    return jnp.stack([c, a, b]) * scale


def _dil_tables(seq, scale, tm):
    half = ROT_DIM // 2
    rest = DIL_HD - ROT_DIM
    rep = LANES // DIL_HD
    inv = 1.0 / (ROPE_THETA ** (jnp.arange(0, ROT_DIM, 2, dtype=F32) / ROT_DIM))
    z = lambda n: jnp.zeros((seq, n), F32)
    out = []
    for _, d in DIL_PAIRS:
        pos = np.arange(seq).reshape(seq // tm, tm // d, d).transpose(0, 2, 1).reshape(seq)
        ang = jnp.asarray(pos, F32)[:, None] * inv[None, :]
        cos, sin = jnp.cos(ang) * scale, jnp.sin(ang) * scale
        c = jnp.concatenate([cos, cos, jnp.full((seq, rest), scale, F32)], axis=1)
        a = jnp.concatenate([-sin, z(half), z(rest)], axis=1)
        b = jnp.concatenate([z(half), sin, z(rest)], axis=1)
        out.append(jnp.stack([jnp.tile(c, (1, rep)), jnp.tile(a, (1, rep)), jnp.tile(b, (1, rep))]))
    return jnp.stack(out)


def _rotate3(blk, tab_c, tab_a, tab_b, shift):
    return (blk * tab_c
            + pltpu.roll(blk, LANES - shift, 1) * tab_a
            + pltpu.roll(blk, shift, 1) * tab_b)


def _rms_f32(x, g):
    ms = jnp.mean(x * x, axis=-1, keepdims=True)
    return x * lax.rsqrt(ms + EPS) * g


def _in_proj_kernel(x_ref, g_ref, w_ref, tab_ref, dil_ref, lat_ref, gate_ref, h_scr, xn_scr):
    tm, k = x_ref.shape
    j = pl.program_id(1)
    nhalf = xn_scr.shape[0]

    span = tm // ROW_SPLIT

    def group_step(gi):
        start = gi * GROUP_W
        if not isinstance(gi, int):
            start = pl.multiple_of(start, LANES)
        cols = pl.ds(start, GROUP_W)
        for part in range(ROW_SPLIT):
            rs = slice(part * span, (part + 1) * span)
            y = jnp.dot(h_scr[gi, rs, :], w_ref[:, cols], preferred_element_type=F32)
            tc, ta, tb = tab_ref[0, 0, rs, :], tab_ref[0, 1, rs, :], tab_ref[0, 2, rs, :]
            for c in range(GROUP_W // LANES):
                sl = slice(c * LANES, (c + 1) * LANES)
                yc = y[:, sl]
                if c < 2 * DIL_WIDTH // LANES:
                    yc = _rotate3(yc, tc, ta, tb, ROT_DIM // 2)
                dil_ref[rs, sl] = yc.astype(BF16)

    @pl.when(j == 0)
    def _():
        xn = _rms_f32(x_ref[...], g_ref[...])
        h_scr[0] = xn.astype(BF16)
        assert DIL_PAIRS[0][1] == 1
        group_step(0)
        for base in range(0, k // LANES, nhalf):
            for c in range(nhalf):
                xn_scr[c] = xn[:, (base + c) * LANES:(base + c + 1) * LANES]
            for gi, (_, d) in enumerate(DIL_PAIRS):
                if d == 1:
                    continue
                rows = tm // d
                for r in range(d):
                    for c in range(nhalf):
                        sl = slice((base + c) * LANES, (base + c + 1) * LANES)
                        h_scr[gi, r * rows:(r + 1) * rows, sl] = (
                            xn_scr[c, pl.ds(r, rows, stride=d), :].astype(BF16))

    @pl.when((j > 0) & (j < DIL_GROUPS))
    def _():
        group_step(j)

    lat_start = DIL_GROUPS * GROUP_W
    for step, ref, start in ((DIL_GROUPS, lat_ref, lat_start),
                             (DIL_GROUPS + 1, gate_ref, lat_start + LAT_WIDTH)):
        @pl.when(j == step)
        def _(ref=ref, start=start):
            cols = slice(start, start + ref.shape[1])
            for part in range(ROW_SPLIT):
                rs = slice(part * span, (part + 1) * span)
                y = jnp.dot(h_scr[0, rs, :], w_ref[:, cols], preferred_element_type=F32)
                ref[rs, :] = y.astype(ref.dtype)


def _in_proj(x2d, g, w_all, layer, tabs, *, tm, seq):
    m, k = x2d.shape
    nblk = seq // tm
    return pl.pallas_call(
        _in_proj_kernel,
        grid=(m // tm, DIL_GROUPS + 2),
        in_specs=[
            pl.BlockSpec((tm, k), lambda i, j: (i, 0)),
            pl.BlockSpec((1, k), lambda i, j: (0, 0)),
            pl.BlockSpec((None, k, w_all.shape[2]), lambda i, j: (layer, 0, 0),
                         pipeline_mode=pl.Buffered(1)),
            pl.BlockSpec((1, 3, tm, LANES), lambda i, j: (jnp.minimum(j, DIL_GROUPS - 1), 0, i % nblk, 0)),
        ],
        out_specs=[
            pl.BlockSpec((tm, GROUP_W), lambda i, j: (i, jnp.minimum(j, DIL_GROUPS - 1))),
            pl.BlockSpec((tm, LAT_WIDTH), lambda i, j: (i, 0)),
            pl.BlockSpec((tm, 2 * MLA_WIDTH), lambda i, j: (i, 0)),
        ],
        out_shape=[
            jax.ShapeDtypeStruct((m, DIL_GROUPS * GROUP_W), BF16),
            jax.ShapeDtypeStruct((m, LAT_WIDTH), F32),
            jax.ShapeDtypeStruct((m, 2 * MLA_WIDTH), BF16),
        ],
        scratch_shapes=[pltpu.VMEM((DIL_GROUPS, tm, k), BF16), pltpu.VMEM((k // LANES // 2, tm, LANES), F32)],
        compiler_params=pltpu.CompilerParams(
            dimension_semantics=("arbitrary", "arbitrary"), vmem_limit_bytes=VMEM_LIMIT_IN_PROJ),
        name="in_proj",
    )(x2d, g.reshape(1, k), w_all, tabs)


def _mla_prep_kernel(lat_ref, qg_ref, kvg_ref, wqt_ref, wk_ref, wvt_ref, tqt_ref, tk_ref,
                     qt_out, k_out, vt_out):
    lat = lat_ref[0]
    ts = lat.shape[0]
    cqn = _rms_f32(lat[:, :Q_LORA], qg_ref[...]).astype(BF16)
    ckvn = _rms_f32(lat[:, Q_LORA:Q_LORA + KV_LORA], kvg_ref[...]).astype(BF16)
    kr = lat[:, Q_LORA + KV_LORA:]
    nt = (((1,), (1,)), ((), ()))
    half = MLA_ROPE // 2

    qt = lax.dot_general(wqt_ref[...], cqn, nt, preferred_element_type=F32)
    qc, qa, qb = tqt_ref[0], tqt_ref[1], tqt_ref[2]
    for h in range(MLA_HEADS):
        blk = qt[h * HEAD_PAD:(h + 1) * HEAD_PAD, :]
        up = jnp.concatenate([blk[half:], blk[:half]], axis=0)
        down = jnp.concatenate([blk[-half:], blk[:-half]], axis=0)
        qt_out[0, h] = (blk * qc + up * qa + down * qb).astype(BF16)

    krp = _rotate3(kr, tk_ref[0], tk_ref[1], tk_ref[2], half)
    kk = jnp.dot(ckvn, wk_ref[...], preferred_element_type=F32)
    for h in range(MLA_HEADS):
        k_out[0, h] = (kk[:, h * HEAD_PAD:(h + 1) * HEAD_PAD] + krp).astype(BF16)

    vt = lax.dot_general(wvt_ref[...], ckvn, nt, preferred_element_type=F32)
    row = lax.broadcasted_iota(jnp.int32, (HEAD_PAD, ts), 0)
    ones_row = (row == MLA_V).astype(F32)
    for h in range(MLA_HEADS):
        vt_out[0, h] = (vt[h * HEAD_PAD:(h + 1) * HEAD_PAD, :] + ones_row).astype(BF16)


def _mla_prep(lat, qg, kvg, wqt, wk, wvt, tabs_qt, tabs_k, *, ts):
    b, s, _ = lat.shape
    hp = MLA_HEADS * HEAD_PAD
    const = lambda *shape: pl.BlockSpec(shape, lambda bi, i: (0,) * len(shape))
    rows = pl.BlockSpec((1, MLA_HEADS, ts, HEAD_PAD), lambda bi, i: (bi, 0, i, 0))
    cols = pl.BlockSpec((1, MLA_HEADS, HEAD_PAD, ts), lambda bi, i: (bi, 0, 0, i))
    rows_shape = jax.ShapeDtypeStruct((b, MLA_HEADS, s, HEAD_PAD), BF16)
    cols_shape = jax.ShapeDtypeStruct((b, MLA_HEADS, HEAD_PAD, s), BF16)
    return pl.pallas_call(
        _mla_prep_kernel,
        grid=(b, s // ts),
        in_specs=[
            pl.BlockSpec((1, ts, LAT_WIDTH), lambda bi, i: (bi, i, 0)),
            const(1, Q_LORA), const(1, KV_LORA),
            const(hp, Q_LORA), const(KV_LORA, hp), const(hp, KV_LORA),
            pl.BlockSpec((3, LANES, ts), lambda bi, i: (0, 0, i)),
            pl.BlockSpec((3, ts, LANES), lambda bi, i: (0, i, 0)),
        ],
        out_specs=[cols, rows, cols],
        out_shape=[cols_shape, rows_shape, cols_shape],
        compiler_params=pltpu.CompilerParams(
            dimension_semantics=("arbitrary", "arbitrary"), vmem_limit_bytes=VMEM_LIMIT),
        name="mla_prep",
    )(lat, qg.reshape(1, -1), kvg.reshape(1, -1), wqt, wk, wvt, tabs_qt, tabs_k)


def _flash_kernel(qt_ref, k_ref, vt_ref, o_ref, *, tk, qw):
    tq = qt_ref.shape[3]
    s_len = k_ref.shape[2]
    nsub = tk // SUB_K
    nchunk = s_len // tk
    items = [(qb, hh, c) for qb in range(tq // qw) for hh in range(2) for c in range(nchunk)]

    def scores(item, r):
        qb, hh, c = item
        rows = slice(c * tk + r * SUB_K, c * tk + (r + 1) * SUB_K)
        qt = qt_ref[0, hh, :, qb * qw:(qb + 1) * qw]
        return jnp.dot(k_ref[0, hh, rows, :], qt, preferred_element_type=F32)

    def weighted(item, r, p):
        _, hh, c = item
        cols = slice(c * tk + r * SUB_K, c * tk + (r + 1) * SUB_K)
        return jnp.dot(vt_ref[0, hh, 0:VT_ROWS, cols], p, preferred_element_type=F32)

    def fold(state, item, alpha, pv):
        key = item[:2]
        m, acc = state[key]
        state[key] = (m, alpha * acc + pv)

    state = {}
    s_parts = [scores(items[0], r) for r in range(nsub)]
    prev = None
    for t, item in enumerate(items):
        key = item[:2]
        if key not in state:
            state[key] = (jnp.full((1, qw), NEG_BIG, F32), jnp.zeros((VT_ROWS, qw), F32))
        m, acc = state[key]
        m_new = m
        for s in s_parts:
            m_new = jnp.maximum(m_new, jnp.max(s, axis=0, keepdims=True))
        alpha = jnp.exp2(m - m_new)
        state[key] = (m_new, acc)
        nxt, p_parts, pv = [], [], None
        for r in range(nsub):
            shift = m_new
            if t + 1 < len(items):
                nxt.append(scores(items[t + 1], r))
                shift = m_new + nxt[r][0:1, :] * 0.0
            if prev is not None:
                part = weighted(prev[0], r, prev[1][r])
                pv = part if pv is None else pv + part
            p_parts.append(jnp.exp2(s_parts[r] - shift).astype(BF16))
        if prev is not None:
            fold(state, prev[0], prev[2], pv)
        s_parts, prev = nxt, (item, p_parts, alpha)
    pv = None
    for r in range(nsub):
        part = weighted(prev[0], r, prev[1][r])
        pv = part if pv is None else pv + part
    fold(state, prev[0], prev[2], pv)

    for qb in range(tq // qw):
        outs = []
        for hh in range(2):
            acc = state[(qb, hh)][1]
            outs.append(acc[:MLA_V] / acc[MLA_V:MLA_V + 1])
        o_ref[0, qb * qw:(qb + 1) * qw, :] = jnp.concatenate(outs, axis=0).T.astype(o_ref.dtype)


def _mla_flash(qt, k, vt, *, tq, tk, qw):
    b, h, s, _ = k.shape
    return pl.pallas_call(
        functools.partial(_flash_kernel, tk=tk, qw=qw),
        grid=(b, h // 2, s // tq),
        in_specs=[
            pl.BlockSpec((1, 2, HEAD_PAD, tq), lambda bi, hp, i: (bi, hp, 0, i)),
            pl.BlockSpec((1, 2, s, HEAD_PAD), lambda bi, hp, i: (bi, hp, 0, 0)),
            pl.BlockSpec((1, 2, HEAD_PAD, s), lambda bi, hp, i: (bi, hp, 0, 0)),
        ],
        out_specs=pl.BlockSpec((1, tq, 2 * MLA_V), lambda bi, hp, i: (bi, i, hp)),
        out_shape=jax.ShapeDtypeStruct((b, s, MLA_WIDTH), BF16),
        compiler_params=pltpu.CompilerParams(
            dimension_semantics=("arbitrary", "arbitrary", "arbitrary"),
            vmem_limit_bytes=VMEM_LIMIT),
        name="mla_flash",
    )(qt, k, vt)


def _band_kernel(q_ref, kp_ref, km_ref, kn_ref, vp_ref, vm_ref, vn_ref, o_ref, lse_ref,
                 qwin, kwin, vwin, *, half, sub):
    tq = qwin.shape[0]
    nchunk, chunk = q_ref.shape[1], q_ref.shape[3]
    i = pl.program_id(2)
    for c in range(nchunk):
        rows = slice(c * chunk, (c + 1) * chunk)
        qwin[rows] = q_ref[0, c, 0]
        kwin[half + c * chunk:half + (c + 1) * chunk] = km_ref[0, c, 0]
        vwin[half + c * chunk:half + (c + 1) * chunk] = vm_ref[0, c, 0]
    kwin[0:half] = kp_ref[0, 0, 0]
    kwin[half + tq:] = kn_ref[0, 0, 0]
    vwin[0:half] = vp_ref[0, 0, 0]
    vwin[half + tq:] = vn_ref[0, 0, 0]

    wk = sub + 2 * half
    lo = lax.broadcasted_iota(jnp.int32, (sub, LANES), 1) < DIL_HD
    ones = jnp.ones((wk, LANES), BF16)
    row = lax.broadcasted_iota(jnp.int32, (2 * sub, wk), 0) % sub
    col = lax.broadcasted_iota(jnp.int32, (2 * sub, wk), 1)
    rel = col - half - row
    in_band = (rel >= -half) & (rel <= half)
    bias_mid = jnp.where(in_band, 0.0, NEG_BIG)
    bias_first = jnp.where(in_band & (col >= half), 0.0, NEG_BIG)
    bias_last = jnp.where(in_band & (col < half + sub), 0.0, NEG_BIG)
    nstep = tq // sub
    assert nstep >= 2
    at_start = i == 0
    at_end = i == pl.num_programs(2) - 1

    for t in range(nstep):
        r0 = t * sub
        bias = bias_mid
        if t == 0:
            bias = jnp.where(at_start, bias_first, bias)
        if t == nstep - 1:
            bias = jnp.where(at_end, bias_last, bias)
        for c in range(DIL_WIDTH // LANES):
            cs = slice(c * LANES, (c + 1) * LANES)
            qt = qwin[r0:r0 + sub, cs]
            zero = jnp.zeros_like(qt)
            q2 = jnp.concatenate([jnp.where(lo, qt, zero), jnp.where(lo, zero, qt)], axis=0)
            kt = kwin[r0:r0 + wk, cs]
            vt = vwin[r0:r0 + wk, cs]
            s = lax.dot_general(q2, kt, (((1,), (1,)), ((), ())), preferred_element_type=F32) + bias
            m = jnp.max(s, axis=-1, keepdims=True)
            p = jnp.exp2(s - m).astype(BF16)
            ol = jnp.dot(p, jnp.concatenate([vt, ones], axis=1), preferred_element_type=F32)
            o = jnp.where(lo, ol[:sub, :LANES], ol[sub:, :LANES])
            l = jnp.where(lo, ol[:sub, LANES:], ol[sub:, LANES:])
            mm = jnp.where(lo, m[:sub], m[sub:])
            o_ref[0, 0, r0:r0 + sub, cs] = (o / l).astype(o_ref.dtype)
            lse_ref[0, 0, r0:r0 + sub, cs] = mm + jnp.log2(l)


def _band_attention(dil2d, g, *, tm, tq, sub, batch, seq):
    window, d = DIL_PAIRS[g]
    half = window // (2 * d)
    length = seq // d
    tq = min(tq, length)
    chunk = tm // d
    view = dil2d.reshape(batch, seq // tm, d, chunk, dil2d.shape[1])
    rows_blk = min(chunk, tq)
    nchunk = tq // rows_blk

    def main(c):
        col = 3 * g + c
        if chunk >= tq:
            per = chunk // tq
            imap = lambda bi, r, i: (bi, i // per, r, i % per, col)
        else:
            imap = lambda bi, r, i: (bi, i, r, 0, col)
        return pl.BlockSpec((1, nchunk, 1, rows_blk, DIL_WIDTH), imap)

    def edge(c, start_of):
        col = 3 * g + c

        def imap(bi, r, i):
            start = start_of(i)
            return (bi, start // chunk, r, (start % chunk) // half, col)
        return pl.BlockSpec((1, 1, 1, half, DIL_WIDTH), imap)

    before = lambda i: jnp.maximum(i * tq - half, 0)
    after = lambda i: jnp.minimum((i + 1) * tq, length - half)
    out_spec = pl.BlockSpec((1, 1, tq, DIL_WIDTH), lambda bi, r, i: (bi, r, i, 0))
    return pl.pallas_call(
        functools.partial(_band_kernel, half=half, sub=sub),
        grid=(batch, d, length // tq),
        in_specs=[main(0), edge(1, before), main(1), edge(1, after),
                  edge(2, before), main(2), edge(2, after)],
        out_specs=[out_spec, out_spec],
        out_shape=[jax.ShapeDtypeStruct((batch, d, length, DIL_WIDTH), BF16),
                   jax.ShapeDtypeStruct((batch, d, length, DIL_WIDTH), F32)],
        scratch_shapes=[pltpu.VMEM((tq, DIL_WIDTH), BF16),
                        pltpu.VMEM((tq + 2 * half, DIL_WIDTH), BF16),
                        pltpu.VMEM((tq + 2 * half, DIL_WIDTH), BF16)],
        compiler_params=pltpu.CompilerParams(
            dimension_semantics=("arbitrary", "arbitrary", "arbitrary"),
            vmem_limit_bytes=VMEM_LIMIT),
        name=f"band_attn_d{d}",
    )(view, view, view, view, view, view, view)


def _silu(g):
    return g / (1.0 + jnp.exp(-g))


def _out_kernel(x_ref, a_ref, gate_ref, o1, l1, o2, l2, o3, l3, w_ref, fg_ref, y_ref, nat_scr,
                *, final):
    tm = x_ref.shape[0]

    def natural(ref, slot):
        d = ref.shape[1]
        if d == 1:
            return ref[0, 0]
        ntile = ref.shape[3] // LANES
        for r in range(d):
            for c in range(ntile):
                nat_scr[slot * ntile + c, pl.ds(r, tm // d, stride=d), :] = (
                    ref[0, r, :, c * LANES:(c + 1) * LANES].astype(F32))
        return jnp.concatenate([nat_scr[slot * ntile + c] for c in range(ntile)], axis=1)

    la, lb, lc = l1[0, 0], natural(l2, 0), natural(l3, 1)
    mx = jnp.maximum(jnp.maximum(la, lb), lc)
    ea, eb, ec = jnp.exp2(la - mx), jnp.exp2(lb - mx), jnp.exp2(lc - mx)
    bmix = (ea * o1[0, 0] + eb * natural(o2, 2) + ec * natural(o3, 3)) / (ea + eb + ec)
    gates = gate_ref[...].astype(F32)
    mix = jnp.concatenate([a_ref[...] * _silu(gates[:, :MLA_WIDTH]),
                           bmix * _silu(gates[:, MLA_WIDTH:])], axis=-1).astype(BF16)
    y = x_ref[...] + jnp.dot(mix, w_ref[...], preferred_element_type=F32)
    if final:
        y = _rms_f32(y, fg_ref[...])
    y_ref[...] = y


def _out_proj(x2d, a2d, gates, groups, w_out_all, layer, final_g, *, tm, seq, final):
    m, dm = x2d.shape
    nblk = seq // tm
    row = lambda n: pl.BlockSpec((tm, n), lambda i: (i, 0))
    args = [x2d, a2d, gates]
    specs = [row(dm), row(MLA_WIDTH), row(2 * MLA_WIDTH)]
    for (_, d), (o, lse) in zip(DIL_PAIRS, groups):
        spec = pl.BlockSpec((1, d, tm // d, DIL_WIDTH), lambda i: (i // nblk, 0, i % nblk, 0))
        args += [o, lse]
        specs += [spec, spec]
    args += [w_out_all, final_g.reshape(1, dm)]
    specs += [pl.BlockSpec((None, dm, dm), lambda i: (layer, 0, 0)),
              pl.BlockSpec((1, dm), lambda i: (0, 0))]
    return pl.pallas_call(
        functools.partial(_out_kernel, final=final),
        grid=(m // tm,),
        in_specs=specs,
        out_specs=row(dm),
        out_shape=jax.ShapeDtypeStruct((m, dm), F32),
        scratch_shapes=[pltpu.VMEM((4 * DIL_WIDTH // LANES, tm, LANES), F32)],
        compiler_params=pltpu.CompilerParams(
            dimension_semantics=("arbitrary",), vmem_limit_bytes=VMEM_LIMIT),
        name="merge_out_proj",
    )(*args)


def _arrange_w_in(w):
    o_kr = Q_LORA + KV_LORA
    o_ga = o_kr + MLA_ROPE
    o_dil = o_ga + MLA_WIDTH
    o_gb = o_dil + DIL_GROUPS * GROUP_W
    total = o_gb + MLA_WIDTH
    plan = [((o_dil, o_gb), 0), ((0, o_kr), MLA_NOPE), ((o_kr, o_ga), HEAD_PAD - MLA_NOPE - MLA_ROPE),
            ((o_ga, o_dil), 0), ((o_gb, total), 0)]
    width = sum(hi - lo + gap for (lo, hi), gap in plan)
    out, pos = None, 0
    for (lo, hi), gap in plan:
        seg = jnp.pad(w[..., lo:hi], ((0, 0), (0, 0), (pos, width - pos - (hi - lo))))
        out = seg if out is None else out + seg
        pos += hi - lo + gap
    return out.astype(BF16)


def _pad_heads(w, per_head, keep):
    lead = w.shape[:-1]
    wh = w.reshape(*lead, MLA_HEADS, per_head)[..., keep]
    wh = jnp.pad(wh, [(0, 0)] * (wh.ndim - 1) + [(0, HEAD_PAD - wh.shape[-1])])
    return wh.reshape(*lead, MLA_HEADS * HEAD_PAD)


def kernel(x, norm_g, w_in, q_norm_g, kv_norm_g, w_uq, w_ukv, w_out, final_g):
    b, s, dm = x.shape
    m = b * s
    tm_in = 1024
    mla_scale = (MLA_NOPE + MLA_ROPE) ** -0.5 * LOG2E
    dil_scale = DIL_HD ** -0.5 * LOG2E
    tabs_mqt = _mla_tables(s, mla_scale).transpose(0, 2, 1)
    tabs_mk = _mla_tables(s, 1.0)
    tabs_d = _dil_tables(s, math.sqrt(dil_scale), tm_in)

    w_in_all = _arrange_w_in(w_in)
    wqt_all = _pad_heads(w_uq, MLA_NOPE + MLA_ROPE, slice(None)).astype(BF16).swapaxes(1, 2)
    wk_all = _pad_heads(w_ukv, MLA_NOPE + MLA_V, slice(0, MLA_NOPE)).astype(BF16)
    wvt_all = _pad_heads(w_ukv, MLA_NOPE + MLA_V, slice(MLA_NOPE, None)).astype(BF16).swapaxes(1, 2)
    w_out_all = w_out.astype(BF16)

    x2d = x.reshape(m, dm)
    for layer in range(DEPTH):
        dil, lat, gates = _in_proj(x2d, norm_g[layer], w_in_all, layer, tabs_d, tm=tm_in, seq=s)

        qt, kk, vt = _mla_prep(lat.reshape(b, s, -1), q_norm_g[layer], kv_norm_g[layer],
                               wqt_all[layer], wk_all[layer], wvt_all[layer], tabs_mqt, tabs_mk, ts=1024)
        a = _mla_flash(qt, kk, vt, tq=512, tk=2048, qw=512)

        groups = [_band_attention(dil, gi, tm=tm_in, tq=512, sub=128, batch=b, seq=s)
                  for gi in range(DIL_GROUPS)]

        x2d = _out_proj(x2d, a.reshape(m, MLA_WIDTH), gates, groups, w_out_all, layer,
                        final_g, tm=512, seq=s, final=(layer == DEPTH - 1))
    return x2d.reshape(b, s, dm)
```

```python
import functools
import math

import numpy as np
import jax
import jax.numpy as jnp
from jax import lax
from jax.experimental import pallas as pl
from jax.experimental.pallas import tpu as pltpu

F32 = jnp.float32
BF16 = jnp.bfloat16

D_MODEL = 1024
DEPTH = 4
MLA_HEADS = 8
MLA_NOPE = 64
MLA_ROPE = 32
MLA_V = 64
Q_LORA = 384
KV_LORA = 256
MLA_WIDTH = MLA_HEADS * MLA_V
DIL_PAIRS = ((128, 1), (512, 4), (2048, 16))
DIL_GROUPS = 3
DIL_HEADS = 8
DIL_HD = 64
DIL_WIDTH = DIL_HEADS * DIL_HD
ROT_DIM = DIL_HD // 4
ROPE_THETA = 500000.0
EPS = 1e-6

LANES = 128
BF16_ROWS = 16
HEAD_PAD = 128
LAT_WIDTH = Q_LORA + KV_LORA + HEAD_PAD
VT_ROWS = -(-(MLA_V + 1) // BF16_ROWS) * BF16_ROWS
GROUP_W = 3 * DIL_WIDTH
ROW_SPLIT = 4
SUB_K = 256
LOG2E = math.log2(math.e)
NEG_BIG = -1e30
VMEM_LIMIT = 48 * 1024 * 1024
VMEM_LIMIT_IN_PROJ = 58 * 1024 * 1024


def _rope_tables(seq, dim):
    inv = 1.0 / (ROPE_THETA ** (jnp.arange(0, dim, 2, dtype=F32) / dim))
    ang = jnp.arange(seq, dtype=F32)[:, None] * inv[None, :]
    return jnp.cos(ang), jnp.sin(ang)


def _mla_tables(seq, scale):
    cos, sin = _rope_tables(seq, MLA_ROPE)
    half = MLA_ROPE // 2
    ones = jnp.ones((seq, MLA_NOPE), F32)
    z = lambda n: jnp.zeros((seq, n), F32)
    c = jnp.concatenate([ones, cos, cos, z(HEAD_PAD - MLA_NOPE - MLA_ROPE)], axis=1)
    a = jnp.concatenate([z(MLA_NOPE), -sin, z(half), z(HEAD_PAD - MLA_NOPE - MLA_ROPE)], axis=1)
    b = jnp.concatenate([z(MLA_NOPE), z(half), sin, z(HEAD_PAD - MLA_NOPE - MLA_ROPE)], axis=1)
    return jnp.stack([c, a, b]) * scale


def _dil_tables(seq, scale, tm):
    half = ROT_DIM // 2
    rest = DIL_HD - ROT_DIM
    rep = LANES // DIL_HD
    inv = 1.0 / (ROPE_THETA ** (jnp.arange(0, ROT_DIM, 2, dtype=F32) / ROT_DIM))
    z = lambda n: jnp.zeros((seq, n), F32)
    out = []
    for _, d in DIL_PAIRS:
        pos = np.arange(seq).reshape(seq // tm, tm // d, d).transpose(0, 2, 1).reshape(seq)
        ang = jnp.asarray(pos, F32)[:, None] * inv[None, :]
        cos, sin = jnp.cos(ang) * scale, jnp.sin(ang) * scale
        c = jnp.concatenate([cos, cos, jnp.full((seq, rest), scale, F32)], axis=1)
        a = jnp.concatenate([-sin, z(half), z(rest)], axis=1)
        b = jnp.concatenate([z(half), sin, z(rest)], axis=1)
        out.append(jnp.stack([jnp.tile(c, (1, rep)), jnp.tile(a, (1, rep)), jnp.tile(b, (1, rep))]))
    return jnp.stack(out)


def _rotate3(blk, tab_c, tab_a, tab_b, shift):
    return (blk * tab_c
            + pltpu.roll(blk, LANES - shift, 1) * tab_a
            + pltpu.roll(blk, shift, 1) * tab_b)


def _rms_f32(x, g):
    ms = jnp.mean(x * x, axis=-1, keepdims=True)
    return x * lax.rsqrt(ms + EPS) * g


def _in_proj_kernel(x_ref, g_ref, w_ref, tab_ref, dil_ref, lat_ref, gate_ref, h_scr, xn_scr):
    tm, k = x_ref.shape
    j = pl.program_id(1)
    nhalf = xn_scr.shape[0]

    span = tm // ROW_SPLIT

    def group_step(gi):
        start = gi * GROUP_W
        if not isinstance(gi, int):
            start = pl.multiple_of(start, LANES)
        cols = pl.ds(start, GROUP_W)
        for part in range(ROW_SPLIT):
            rs = slice(part * span, (part + 1) * span)
            y = jnp.dot(h_scr[gi, rs, :], w_ref[:, cols], preferred_element_type=F32)
            tc, ta, tb = tab_ref[0, 0, rs, :], tab_ref[0, 1, rs, :], tab_ref[0, 2, rs, :]
            for c in range(GROUP_W // LANES):
                sl = slice(c * LANES, (c + 1) * LANES)
                yc = y[:, sl]
                if c < 2 * DIL_WIDTH // LANES:
                    yc = _rotate3(yc, tc, ta, tb, ROT_DIM // 2)
                dil_ref[rs, sl] = yc.astype(BF16)

    @pl.when(j == 0)
    def _():
        xn = _rms_f32(x_ref[...], g_ref[...])
        h_scr[0] = xn.astype(BF16)
        assert DIL_PAIRS[0][1] == 1
        group_step(0)
        for base in range(0, k // LANES, nhalf):
            for c in range(nhalf):
                xn_scr[c] = xn[:, (base + c) * LANES:(base + c + 1) * LANES]
            for gi, (_, d) in enumerate(DIL_PAIRS):
                if d == 1:
                    continue
                rows = tm // d
                for r in range(d):
                    for c in range(nhalf):
                        sl = slice((base + c) * LANES, (base + c + 1) * LANES)
                        h_scr[gi, r * rows:(r + 1) * rows, sl] = (
                            xn_scr[c, pl.ds(r, rows, stride=d), :].astype(BF16))

    @pl.when((j > 0) & (j < DIL_GROUPS))
    def _():
        group_step(j)

    lat_start = DIL_GROUPS * GROUP_W
    for step, ref, start in ((DIL_GROUPS, lat_ref, lat_start),
                             (DIL_GROUPS + 1, gate_ref, lat_start + LAT_WIDTH)):
        @pl.when(j == step)
        def _(ref=ref, start=start):
            cols = slice(start, start + ref.shape[1])
            for part in range(ROW_SPLIT):
                rs = slice(part * span, (part + 1) * span)
                y = jnp.dot(h_scr[0, rs, :], w_ref[:, cols], preferred_element_type=F32)
                ref[rs, :] = y.astype(ref.dtype)


def _in_proj(x2d, g, w_all, layer, tabs, *, tm, seq):
    m, k = x2d.shape
    nblk = seq // tm
    return pl.pallas_call(
        _in_proj_kernel,
        grid=(m // tm, DIL_GROUPS + 2),
        in_specs=[
            pl.BlockSpec((tm, k), lambda i, j: (i, 0)),
            pl.BlockSpec((1, k), lambda i, j: (0, 0)),
            pl.BlockSpec((None, k, w_all.shape[2]), lambda i, j: (layer, 0, 0),
                         pipeline_mode=pl.Buffered(1)),
            pl.BlockSpec((1, 3, tm, LANES), lambda i, j: (jnp.minimum(j, DIL_GROUPS - 1), 0, i % nblk, 0)),
        ],
        out_specs=[
            pl.BlockSpec((tm, GROUP_W), lambda i, j: (i, jnp.minimum(j, DIL_GROUPS - 1))),
            pl.BlockSpec((tm, LAT_WIDTH), lambda i, j: (i, 0)),
            pl.BlockSpec((tm, 2 * MLA_WIDTH), lambda i, j: (i, 0)),
        ],
        out_shape=[
            jax.ShapeDtypeStruct((m, DIL_GROUPS * GROUP_W), BF16),
            jax.ShapeDtypeStruct((m, LAT_WIDTH), F32),
            jax.ShapeDtypeStruct((m, 2 * MLA_WIDTH), BF16),
        ],
        scratch_shapes=[pltpu.VMEM((DIL_GROUPS, tm, k), BF16), pltpu.VMEM((k // LANES // 2, tm, LANES), F32)],
        compiler_params=pltpu.CompilerParams(
            dimension_semantics=("arbitrary", "arbitrary"), vmem_limit_bytes=VMEM_LIMIT_IN_PROJ),
        name="in_proj",
    )(x2d, g.reshape(1, k), w_all, tabs)


def _mla_prep_kernel(lat_ref, qg_ref, kvg_ref, wqt_ref, wk_ref, wvt_ref, tqt_ref, tk_ref,
                     qt_out, k_out, vt_out):
    lat = lat_ref[0]
    ts = lat.shape[0]
    cqn = _rms_f32(lat[:, :Q_LORA], qg_ref[...]).astype(BF16)
    ckvn = _rms_f32(lat[:, Q_LORA:Q_LORA + KV_LORA], kvg_ref[...]).astype(BF16)
    kr = lat[:, Q_LORA + KV_LORA:]
    nt = (((1,), (1,)), ((), ()))
    half = MLA_ROPE // 2

    qt = lax.dot_general(wqt_ref[...], cqn, nt, preferred_element_type=F32)
    qc, qa, qb = tqt_ref[0], tqt_ref[1], tqt_ref[2]
    for h in range(MLA_HEADS):
        blk = qt[h * HEAD_PAD:(h + 1) * HEAD_PAD, :]
        up = jnp.concatenate([blk[half:], blk[:half]], axis=0)
        down = jnp.concatenate([blk[-half:], blk[:-half]], axis=0)
        qt_out[0, h] = (blk * qc + up * qa + down * qb).astype(BF16)

    krp = _rotate3(kr, tk_ref[0], tk_ref[1], tk_ref[2], half)
    kk = jnp.dot(ckvn, wk_ref[...], preferred_element_type=F32)
    for h in range(MLA_HEADS):
        k_out[0, h] = (kk[:, h * HEAD_PAD:(h + 1) * HEAD_PAD] + krp).astype(BF16)

    vt = lax.dot_general(wvt_ref[...], ckvn, nt, preferred_element_type=F32)
    row = lax.broadcasted_iota(jnp.int32, (HEAD_PAD, ts), 0)
    ones_row = (row == MLA_V).astype(F32)
    for h in range(MLA_HEADS):
        vt_out[0, h] = (vt[h * HEAD_PAD:(h + 1) * HEAD_PAD, :] + ones_row).astype(BF16)


def _mla_prep(lat, qg, kvg, wqt_all, wk_all, wvt_all, layer, tabs_qt, tabs_k, *, ts):
    b, s, _ = lat.shape
    hp = MLA_HEADS * HEAD_PAD
    const = lambda *shape: pl.BlockSpec(shape, lambda bi, i: (0,) * len(shape))
    of_layer = lambda *shape: pl.BlockSpec((None,) + shape, lambda bi, i: (layer,) + (0,) * len(shape))
    rows = pl.BlockSpec((1, MLA_HEADS, ts, HEAD_PAD), lambda bi, i: (bi, 0, i, 0))
    cols = pl.BlockSpec((1, MLA_HEADS, HEAD_PAD, ts), lambda bi, i: (bi, 0, 0, i))
    rows_shape = jax.ShapeDtypeStruct((b, MLA_HEADS, s, HEAD_PAD), BF16)
    cols_shape = jax.ShapeDtypeStruct((b, MLA_HEADS, HEAD_PAD, s), BF16)
    return pl.pallas_call(
        _mla_prep_kernel,
        grid=(b, s // ts),
        in_specs=[
            pl.BlockSpec((1, ts, LAT_WIDTH), lambda bi, i: (bi, i, 0)),
            const(1, Q_LORA), const(1, KV_LORA),
            of_layer(hp, Q_LORA), of_layer(KV_LORA, hp), of_layer(hp, KV_LORA),
            pl.BlockSpec((3, LANES, ts), lambda bi, i: (0, 0, i)),
            pl.BlockSpec((3, ts, LANES), lambda bi, i: (0, i, 0)),
        ],
        out_specs=[cols, rows, cols],
        out_shape=[cols_shape, rows_shape, cols_shape],
        compiler_params=pltpu.CompilerParams(
            dimension_semantics=("arbitrary", "arbitrary"), vmem_limit_bytes=VMEM_LIMIT),
        name="mla_prep",
    )(lat, qg.reshape(1, -1), kvg.reshape(1, -1), wqt_all, wk_all, wvt_all, tabs_qt, tabs_k)


def _flash_kernel(qt_ref, k_ref, vt_ref, o_ref, *, tk, qw):
    tq = qt_ref.shape[3]
    s_len = k_ref.shape[2]
    nsub = tk // SUB_K
    nchunk = s_len // tk
    items = [(qb, hh, c) for qb in range(tq // qw) for hh in range(2) for c in range(nchunk)]

    def scores(item, r):
        qb, hh, c = item
        rows = slice(c * tk + r * SUB_K, c * tk + (r + 1) * SUB_K)
        qt = qt_ref[0, hh, :, qb * qw:(qb + 1) * qw]
        return jnp.dot(k_ref[0, hh, rows, :], qt, preferred_element_type=F32)

    def weighted(item, r, p):
        _, hh, c = item
        cols = slice(c * tk + r * SUB_K, c * tk + (r + 1) * SUB_K)
        return jnp.dot(vt_ref[0, hh, 0:VT_ROWS, cols], p, preferred_element_type=F32)

    def fold(state, item, alpha, pv):
        key = item[:2]
        m, acc = state[key]
        state[key] = (m, alpha * acc + pv)

    state = {}
    s_parts = [scores(items[0], r) for r in range(nsub)]
    prev = None
    for t, item in enumerate(items):
        key = item[:2]
        if key not in state:
            state[key] = (jnp.full((1, qw), NEG_BIG, F32), jnp.zeros((VT_ROWS, qw), F32))
        m, acc = state[key]
        m_new = m
        for s in s_parts:
            m_new = jnp.maximum(m_new, jnp.max(s, axis=0, keepdims=True))
        alpha = jnp.exp2(m - m_new)
        state[key] = (m_new, acc)
        nxt, p_parts, pv = [], [], None
        for r in range(nsub):
            shift = m_new
            if t + 1 < len(items):
                nxt.append(scores(items[t + 1], r))
                shift = m_new + nxt[r][0:1, :] * 0.0
            if prev is not None:
                part = weighted(prev[0], r, prev[1][r])
                pv = part if pv is None else pv + part
            p_parts.append(jnp.exp2(s_parts[r] - shift).astype(BF16))
        if prev is not None:
            fold(state, prev[0], prev[2], pv)
        s_parts, prev = nxt, (item, p_parts, alpha)
    pv = None
    for r in range(nsub):
        part = weighted(prev[0], r, prev[1][r])
        pv = part if pv is None else pv + part
    fold(state, prev[0], prev[2], pv)

    for qb in range(tq // qw):
        outs = []
        for hh in range(2):
            acc = state[(qb, hh)][1]
            outs.append(acc[:MLA_V] / acc[MLA_V:MLA_V + 1])
        o_ref[0, qb * qw:(qb + 1) * qw, :] = jnp.concatenate(outs, axis=0).T.astype(o_ref.dtype)


def _mla_flash(qt, k, vt, *, tq, tk, qw):
    b, h, s, _ = k.shape
    return pl.pallas_call(
        functools.partial(_flash_kernel, tk=tk, qw=qw),
        grid=(b, h // 2, s // tq),
        in_specs=[
            pl.BlockSpec((1, 2, HEAD_PAD, tq), lambda bi, hp, i: (bi, hp, 0, i)),
            pl.BlockSpec((1, 2, s, HEAD_PAD), lambda bi, hp, i: (bi, hp, 0, 0)),
            pl.BlockSpec((1, 2, HEAD_PAD, s), lambda bi, hp, i: (bi, hp, 0, 0)),
        ],
        out_specs=pl.BlockSpec((1, tq, 2 * MLA_V), lambda bi, hp, i: (bi, i, hp)),
        out_shape=jax.ShapeDtypeStruct((b, s, MLA_WIDTH), BF16),
        compiler_params=pltpu.CompilerParams(
            dimension_semantics=("arbitrary", "arbitrary", "arbitrary"),
            vmem_limit_bytes=VMEM_LIMIT),
        name="mla_flash",
    )(qt, k, vt)


def _band_kernel(q_ref, kp_ref, km_ref, kn_ref, vp_ref, vm_ref, vn_ref, o_ref, lse_ref,
                 qwin, kwin, vwin, *, half, sub):
    tq = qwin.shape[0]
    nchunk, chunk = q_ref.shape[1], q_ref.shape[3]
    i = pl.program_id(2)
    for c in range(nchunk):
        rows = slice(c * chunk, (c + 1) * chunk)
        qwin[rows] = q_ref[0, c, 0]
        kwin[half + c * chunk:half + (c + 1) * chunk] = km_ref[0, c, 0]
        vwin[half + c * chunk:half + (c + 1) * chunk] = vm_ref[0, c, 0]
    kwin[0:half] = kp_ref[0, 0, 0]
    kwin[half + tq:] = kn_ref[0, 0, 0]
    vwin[0:half] = vp_ref[0, 0, 0]
    vwin[half + tq:] = vn_ref[0, 0, 0]

    wk = sub + 2 * half
    lo = lax.broadcasted_iota(jnp.int32, (sub, LANES), 1) < DIL_HD
    ones = jnp.ones((wk, LANES), BF16)
    row = lax.broadcasted_iota(jnp.int32, (2 * sub, wk), 0) % sub
    col = lax.broadcasted_iota(jnp.int32, (2 * sub, wk), 1)
    rel = col - half - row
    in_band = (rel >= -half) & (rel <= half)
    bias_mid = jnp.where(in_band, 0.0, NEG_BIG)
    bias_first = jnp.where(in_band & (col >= half), 0.0, NEG_BIG)
    bias_last = jnp.where(in_band & (col < half + sub), 0.0, NEG_BIG)
    nstep = tq // sub
    assert nstep >= 2
    at_start = i == 0
    at_end = i == pl.num_programs(2) - 1

    for t in range(nstep):
        r0 = t * sub
        bias = bias_mid
        if t == 0:
            bias = jnp.where(at_start, bias_first, bias)
        if t == nstep - 1:
            bias = jnp.where(at_end, bias_last, bias)
        for c in range(DIL_WIDTH // LANES):
            cs = slice(c * LANES, (c + 1) * LANES)
            qt = qwin[r0:r0 + sub, cs]
            zero = jnp.zeros_like(qt)
            q2 = jnp.concatenate([jnp.where(lo, qt, zero), jnp.where(lo, zero, qt)], axis=0)
            kt = kwin[r0:r0 + wk, cs]
            vt = vwin[r0:r0 + wk, cs]
            s = lax.dot_general(q2, kt, (((1,), (1,)), ((), ())), preferred_element_type=F32) + bias
            m = jnp.max(s, axis=-1, keepdims=True)
            p = jnp.exp2(s - m).astype(BF16)
            ol = jnp.dot(p, jnp.concatenate([vt, ones], axis=1), preferred_element_type=F32)
            o = jnp.where(lo, ol[:sub, :LANES], ol[sub:, :LANES])
            l = jnp.where(lo, ol[:sub, LANES:], ol[sub:, LANES:])
            mm = jnp.where(lo, m[:sub], m[sub:])
            o_ref[0, 0, r0:r0 + sub, cs] = (o / l).astype(o_ref.dtype)
            lse_ref[0, 0, r0:r0 + sub, cs] = mm + jnp.log2(l)


def _band_attention(dil2d, g, *, tm, tq, sub, batch, seq):
    window, d = DIL_PAIRS[g]
    half = window // (2 * d)
    length = seq // d
    tq = min(tq, length)
    chunk = tm // d
    view = dil2d.reshape(batch, seq // tm, d, chunk, dil2d.shape[1])
    rows_blk = min(chunk, tq)
    nchunk = tq // rows_blk

    def main(c):
        col = 3 * g + c
        if chunk >= tq:
            per = chunk // tq
            imap = lambda bi, r, i: (bi, i // per, r, i % per, col)
        else:
            imap = lambda bi, r, i: (bi, i, r, 0, col)
        return pl.BlockSpec((1, nchunk, 1, rows_blk, DIL_WIDTH), imap)

    def edge(c, start_of):
        col = 3 * g + c

        def imap(bi, r, i):
            start = start_of(i)
            return (bi, start // chunk, r, (start % chunk) // half, col)
        return pl.BlockSpec((1, 1, 1, half, DIL_WIDTH), imap)

    before = lambda i: jnp.maximum(i * tq - half, 0)
    after = lambda i: jnp.minimum((i + 1) * tq, length - half)
    out_spec = pl.BlockSpec((1, 1, tq, DIL_WIDTH), lambda bi, r, i: (bi, r, i, 0))
    return pl.pallas_call(
        functools.partial(_band_kernel, half=half, sub=sub),
        grid=(batch, d, length // tq),
        in_specs=[main(0), edge(1, before), main(1), edge(1, after),
                  edge(2, before), main(2), edge(2, after)],
        out_specs=[out_spec, out_spec],
        out_shape=[jax.ShapeDtypeStruct((batch, d, length, DIL_WIDTH), BF16),
                   jax.ShapeDtypeStruct((batch, d, length, DIL_WIDTH), F32)],
        scratch_shapes=[pltpu.VMEM((tq, DIL_WIDTH), BF16),
                        pltpu.VMEM((tq + 2 * half, DIL_WIDTH), BF16),
                        pltpu.VMEM((tq + 2 * half, DIL_WIDTH), BF16)],
        compiler_params=pltpu.CompilerParams(
            dimension_semantics=("arbitrary", "arbitrary", "arbitrary"),
            vmem_limit_bytes=VMEM_LIMIT),
        name=f"band_attn_d{d}",
    )(view, view, view, view, view, view, view)


def _silu(g):
    return g / (1.0 + jnp.exp(-g))


def _out_kernel(x_ref, a_ref, gate_ref, o1, l1, o2, l2, o3, l3, w_ref, fg_ref, y_ref, nat_scr,
                *, final):
    tm = x_ref.shape[0]

    def natural(ref, slot):
        d = ref.shape[1]
        if d == 1:
            return ref[0, 0]
        ntile = ref.shape[3] // LANES
        for r in range(d):
            for c in range(ntile):
                nat_scr[slot * ntile + c, pl.ds(r, tm // d, stride=d), :] = (
                    ref[0, r, :, c * LANES:(c + 1) * LANES].astype(F32))
        return jnp.concatenate([nat_scr[slot * ntile + c] for c in range(ntile)], axis=1)

    la, lb, lc = l1[0, 0], natural(l2, 0), natural(l3, 1)
    mx = jnp.maximum(jnp.maximum(la, lb), lc)
    ea, eb, ec = jnp.exp2(la - mx), jnp.exp2(lb - mx), jnp.exp2(lc - mx)
    bmix = (ea * o1[0, 0] + eb * natural(o2, 2) + ec * natural(o3, 3)) / (ea + eb + ec)
    gates = gate_ref[...].astype(F32)
    mix = jnp.concatenate([a_ref[...] * _silu(gates[:, :MLA_WIDTH]),
                           bmix * _silu(gates[:, MLA_WIDTH:])], axis=-1).astype(BF16)
    y = x_ref[...] + jnp.dot(mix, w_ref[...], preferred_element_type=F32)
    if final:
        y = _rms_f32(y, fg_ref[...])
    y_ref[...] = y


def _out_proj(x2d, a2d, gates, groups, w_out_all, layer, final_g, *, tm, seq, final):
    m, dm = x2d.shape
    nblk = seq // tm
    row = lambda n: pl.BlockSpec((tm, n), lambda i: (i, 0))
    args = [x2d, a2d, gates]
    specs = [row(dm), row(MLA_WIDTH), row(2 * MLA_WIDTH)]
    for (_, d), (o, lse) in zip(DIL_PAIRS, groups):
        spec = pl.BlockSpec((1, d, tm // d, DIL_WIDTH), lambda i: (i // nblk, 0, i % nblk, 0))
        args += [o, lse]
        specs += [spec, spec]
    args += [w_out_all, final_g.reshape(1, dm)]
    specs += [pl.BlockSpec((None, dm, dm), lambda i: (layer, 0, 0)),
              pl.BlockSpec((1, dm), lambda i: (0, 0))]
    return pl.pallas_call(
        functools.partial(_out_kernel, final=final),
        grid=(m // tm,),
        in_specs=specs,
        out_specs=row(dm),
        out_shape=jax.ShapeDtypeStruct((m, dm), F32),
        scratch_shapes=[pltpu.VMEM((4 * DIL_WIDTH // LANES, tm, LANES), F32)],
        compiler_params=pltpu.CompilerParams(
            dimension_semantics=("arbitrary",), vmem_limit_bytes=VMEM_LIMIT),
        name="merge_out_proj",
    )(*args)


def _arrange_w_in(w):
    o_kr = Q_LORA + KV_LORA
    o_ga = o_kr + MLA_ROPE
    o_dil = o_ga + MLA_WIDTH
    o_gb = o_dil + DIL_GROUPS * GROUP_W
    total = o_gb + MLA_WIDTH
    plan = [((o_dil, o_gb), 0), ((0, o_kr), MLA_NOPE), ((o_kr, o_ga), HEAD_PAD - MLA_NOPE - MLA_ROPE),
            ((o_ga, o_dil), 0), ((o_gb, total), 0)]
    width = sum(hi - lo + gap for (lo, hi), gap in plan)
    out, pos = None, 0
    for (lo, hi), gap in plan:
        seg = jnp.pad(w[..., lo:hi], ((0, 0), (0, 0), (pos, width - pos - (hi - lo))))
        out = seg if out is None else out + seg
        pos += hi - lo + gap
    return out.astype(BF16)


def _pad_heads(w, per_head, keep):
    lead = w.shape[:-1]
    wh = w.reshape(*lead, MLA_HEADS, per_head)[..., keep]
    wh = jnp.pad(wh, [(0, 0)] * (wh.ndim - 1) + [(0, HEAD_PAD - wh.shape[-1])])
    return wh.reshape(*lead, MLA_HEADS * HEAD_PAD)


def kernel(x, norm_g, w_in, q_norm_g, kv_norm_g, w_uq, w_ukv, w_out, final_g):
    b, s, dm = x.shape
    m = b * s
    tm_in = 1024
    mla_scale = (MLA_NOPE + MLA_ROPE) ** -0.5 * LOG2E
    dil_scale = DIL_HD ** -0.5 * LOG2E
    tabs_mqt = _mla_tables(s, mla_scale).transpose(0, 2, 1)
    tabs_mk = _mla_tables(s, 1.0)
    tabs_d = _dil_tables(s, math.sqrt(dil_scale), tm_in)

    w_in_all = _arrange_w_in(w_in)
    wqt_all = _pad_heads(w_uq, MLA_NOPE + MLA_ROPE, slice(None)).astype(BF16).swapaxes(1, 2)
    wk_all = _pad_heads(w_ukv, MLA_NOPE + MLA_V, slice(0, MLA_NOPE)).astype(BF16)
    wvt_all = _pad_heads(w_ukv, MLA_NOPE + MLA_V, slice(MLA_NOPE, None)).astype(BF16).swapaxes(1, 2)
    w_out_all = w_out.astype(BF16)

    x2d = x.reshape(m, dm)
    for layer in range(DEPTH):
        dil, lat, gates = _in_proj(x2d, norm_g[layer], w_in_all, layer, tabs_d, tm=tm_in, seq=s)

        qt, kk, vt = _mla_prep(lat.reshape(b, s, -1), q_norm_g[layer], kv_norm_g[layer],
                               wqt_all, wk_all, wvt_all, layer, tabs_mqt, tabs_mk, ts=1024)
        a = _mla_flash(qt, kk, vt, tq=512, tk=2048, qw=512)

        groups = [_band_attention(dil, gi, tm=tm_in, tq=1024, sub=128, batch=b, seq=s)
                  for gi in range(DIL_GROUPS)]

        x2d = _out_proj(x2d, a.reshape(m, MLA_WIDTH), gates, groups, w_out_all, layer,
                        final_g, tm=512, seq=s, final=(layer == DEPTH - 1))
    return x2d.reshape(b, s, dm)
```

```python
import functools
import math

import numpy as np
import jax
import jax.numpy as jnp
from jax import lax
from jax.experimental import pallas as pl
from jax.experimental.pallas import tpu as pltpu

F32 = jnp.float32
BF16 = jnp.bfloat16

D_MODEL = 1024
DEPTH = 4
MLA_HEADS = 8
MLA_NOPE = 64
MLA_ROPE = 32
MLA_V = 64
Q_LORA = 384
KV_LORA = 256
MLA_WIDTH = MLA_HEADS * MLA_V
DIL_PAIRS = ((128, 1), (512, 4), (2048, 16))
DIL_GROUPS = 3
DIL_HEADS = 8
DIL_HD = 64
DIL_WIDTH = DIL_HEADS * DIL_HD
ROT_DIM = DIL_HD // 4
ROPE_THETA = 500000.0
EPS = 1e-6

LANES = 128
BF16_ROWS = 16
HEAD_PAD = 128
LAT_WIDTH = Q_LORA + KV_LORA + HEAD_PAD
VT_ROWS = -(-(MLA_V + 1) // BF16_ROWS) * BF16_ROWS
GROUP_W = 3 * DIL_WIDTH
ROW_SPLIT = 4
SUB_K = 256
LOG2E = math.log2(math.e)
NEG_BIG = -1e30
VMEM_LIMIT = 48 * 1024 * 1024
VMEM_LIMIT_IN_PROJ = 58 * 1024 * 1024


def _rope_tables(seq, dim):
    inv = 1.0 / (ROPE_THETA ** (jnp.arange(0, dim, 2, dtype=F32) / dim))
    ang = jnp.arange(seq, dtype=F32)[:, None] * inv[None, :]
    return jnp.cos(ang), jnp.sin(ang)


def _mla_tables(seq, scale):
    cos, sin = _rope_tables(seq, MLA_ROPE)
    half = MLA_ROPE // 2
    ones = jnp.ones((seq, MLA_NOPE), F32)
    z = lambda n: jnp.zeros((seq, n), F32)
    c = jnp.concatenate([ones, cos, cos, z(HEAD_PAD - MLA_NOPE - MLA_ROPE)], axis=1)
    a = jnp.concatenate([z(MLA_NOPE), -sin, z(half), z(HEAD_PAD - MLA_NOPE - MLA_ROPE)], axis=1)
    b = jnp.concatenate([z(MLA_NOPE), z(half), sin, z(HEAD_PAD - MLA_NOPE - MLA_ROPE)], axis=1)
    return jnp.stack([c, a, b]) * scale


def _dil_tables(seq, scale, tm):
    half = ROT_DIM // 2
    rest = DIL_HD - ROT_DIM
    rep = LANES // DIL_HD
    inv = 1.0 / (ROPE_THETA ** (jnp.arange(0, ROT_DIM, 2, dtype=F32) / ROT_DIM))
    z = lambda n: jnp.zeros((seq, n), F32)
    out = []
    for _, d in DIL_PAIRS:
        pos = np.arange(seq).reshape(seq // tm, tm // d, d).transpose(0, 2, 1).reshape(seq)
        ang = jnp.asarray(pos, F32)[:, None] * inv[None, :]
        cos, sin = jnp.cos(ang) * scale, jnp.sin(ang) * scale
        c = jnp.concatenate([cos, cos, jnp.full((seq, rest), scale, F32)], axis=1)
        a = jnp.concatenate([-sin, z(half), z(rest)], axis=1)
        b = jnp.concatenate([z(half), sin, z(rest)], axis=1)
        out.append(jnp.stack([jnp.tile(c, (1, rep)), jnp.tile(a, (1, rep)), jnp.tile(b, (1, rep))]))
    return jnp.stack(out)


def _rotate3(blk, tab_c, tab_a, tab_b, shift):
    return (blk * tab_c
            + pltpu.roll(blk, LANES - shift, 1) * tab_a
            + pltpu.roll(blk, shift, 1) * tab_b)


def _rms_f32(x, g):
    ms = jnp.mean(x * x, axis=-1, keepdims=True)
    return x * lax.rsqrt(ms + EPS) * g


def _in_proj_kernel(x_ref, g_ref, w_ref, tab_ref, dil_ref, lat_ref, gate_ref, h_scr, xn_scr):
    tm, k = x_ref.shape
    j = pl.program_id(1)
    span = tm // ROW_SPLIT

    def group_step(gi):
        cols = slice(gi * GROUP_W, (gi + 1) * GROUP_W)
        for part in range(ROW_SPLIT):
            rs = slice(part * span, (part + 1) * span)
            y = jnp.dot(h_scr[gi, rs, :], w_ref[:, cols], preferred_element_type=F32)
            tc, ta, tb = tab_ref[0, 0, rs, :], tab_ref[0, 1, rs, :], tab_ref[0, 2, rs, :]
            for c in range(GROUP_W // LANES):
                sl = slice(c * LANES, (c + 1) * LANES)
                yc = y[:, sl]
                if c < 2 * DIL_WIDTH // LANES:
                    yc = _rotate3(yc, tc, ta, tb, ROT_DIM // 2)
                dil_ref[rs, sl] = yc.astype(BF16)

    def reorder(gi):
        d_prev, d = DIL_PAIRS[gi - 1][1], DIL_PAIRS[gi][1]
        ratio, rows = d // d_prev, tm // d
        last = gi == DIL_GROUPS - 1
        for r in range(d_prev):
            for a in range(ratio):
                dst = slice((a * d_prev + r) * rows, (a * d_prev + r + 1) * rows)
                for c in range(k // LANES):
                    y = xn_scr[gi - 1, c, pl.ds(r * (tm // d_prev) + a, rows, stride=ratio), :]
                    if not last:
                        xn_scr[gi, c, dst, :] = y
                    h_scr[gi, dst, c * LANES:(c + 1) * LANES] = y.astype(BF16)

    assert DIL_PAIRS[0][1] == 1 and all(DIL_PAIRS[g + 1][1] % DIL_PAIRS[g][1] == 0
                                        for g in range(DIL_GROUPS - 1))

    @pl.when(j == 0)
    def _():
        xn = _rms_f32(x_ref[...], g_ref[...])
        h_scr[0] = xn.astype(BF16)
        for c in range(k // LANES):
            xn_scr[0, c] = xn[:, c * LANES:(c + 1) * LANES]
        group_step(0)
        reorder(1)

    @pl.when(j == 1)
    def _():
        group_step(1)
        reorder(2)

    @pl.when(j == 2)
    def _():
        group_step(2)

    lat_start = DIL_GROUPS * GROUP_W
    for step, ref, start in ((DIL_GROUPS, lat_ref, lat_start),
                             (DIL_GROUPS + 1, gate_ref, lat_start + LAT_WIDTH)):
        @pl.when(j == step)
        def _(ref=ref, start=start):
            cols = slice(start, start + ref.shape[1])
            for part in range(ROW_SPLIT):
                rs = slice(part * span, (part + 1) * span)
                y = jnp.dot(h_scr[0, rs, :], w_ref[:, cols], preferred_element_type=F32)
                ref[rs, :] = y.astype(ref.dtype)


def _in_proj(x2d, g, w_all, layer, tabs, *, tm, seq):
    m, k = x2d.shape
    nblk = seq // tm
    return pl.pallas_call(
        _in_proj_kernel,
        grid=(m // tm, DIL_GROUPS + 2),
        in_specs=[
            pl.BlockSpec((tm, k), lambda i, j: (i, 0)),
            pl.BlockSpec((1, k), lambda i, j: (0, 0)),
            pl.BlockSpec((None, k, w_all.shape[2]), lambda i, j: (layer, 0, 0),
                         pipeline_mode=pl.Buffered(1)),
            pl.BlockSpec((1, 3, tm, LANES), lambda i, j: (jnp.minimum(j, DIL_GROUPS - 1), 0, i % nblk, 0)),
        ],
        out_specs=[
            pl.BlockSpec((tm, GROUP_W), lambda i, j: (i, jnp.minimum(j, DIL_GROUPS - 1))),
            pl.BlockSpec((tm, LAT_WIDTH), lambda i, j: (i, 0)),
            pl.BlockSpec((tm, 2 * MLA_WIDTH), lambda i, j: (i, 0)),
        ],
        out_shape=[
            jax.ShapeDtypeStruct((m, DIL_GROUPS * GROUP_W), BF16),
            jax.ShapeDtypeStruct((m, LAT_WIDTH), F32),
            jax.ShapeDtypeStruct((m, 2 * MLA_WIDTH), BF16),
        ],
        scratch_shapes=[pltpu.VMEM((DIL_GROUPS, tm, k), BF16),
                        pltpu.VMEM((DIL_GROUPS - 1, k // LANES, tm, LANES), F32)],
        compiler_params=pltpu.CompilerParams(
            dimension_semantics=("arbitrary", "arbitrary"), vmem_limit_bytes=VMEM_LIMIT_IN_PROJ),
        name="in_proj",
    )(x2d, g.reshape(1, k), w_all, tabs)


def _mla_prep_kernel(lat_ref, qg_ref, kvg_ref, wqt_ref, wk_ref, wvt_ref, tqt_ref, tk_ref,
                     qt_out, k_out, vt_out):
    lat = lat_ref[0]
    ts = lat.shape[0]
    cqn = _rms_f32(lat[:, :Q_LORA], qg_ref[...]).astype(BF16)
    ckvn = _rms_f32(lat[:, Q_LORA:Q_LORA + KV_LORA], kvg_ref[...]).astype(BF16)
    kr = lat[:, Q_LORA + KV_LORA:]
    nt = (((1,), (1,)), ((), ()))
    half = MLA_ROPE // 2

    qt = lax.dot_general(wqt_ref[...], cqn, nt, preferred_element_type=F32)
    qc, qa, qb = tqt_ref[0], tqt_ref[1], tqt_ref[2]
    for h in range(MLA_HEADS):
        blk = qt[h * HEAD_PAD:(h + 1) * HEAD_PAD, :]
        up = jnp.concatenate([blk[half:], blk[:half]], axis=0)
        down = jnp.concatenate([blk[-half:], blk[:-half]], axis=0)
        qt_out[0, h] = (blk * qc + up * qa + down * qb).astype(BF16)

    krp = _rotate3(kr, tk_ref[0], tk_ref[1], tk_ref[2], half)
    kk = jnp.dot(ckvn, wk_ref[...], preferred_element_type=F32)
    for h in range(MLA_HEADS):
        k_out[0, h] = (kk[:, h * HEAD_PAD:(h + 1) * HEAD_PAD] + krp).astype(BF16)

    vt = lax.dot_general(wvt_ref[...], ckvn, nt, preferred_element_type=F32)
    row = lax.broadcasted_iota(jnp.int32, (HEAD_PAD, ts), 0)
    ones_row = (row == MLA_V).astype(F32)
    for h in range(MLA_HEADS):
        vt_out[0, h] = (vt[h * HEAD_PAD:(h + 1) * HEAD_PAD, :] + ones_row).astype(BF16)


def _mla_prep(lat, qg, kvg, wqt_all, wk_all, wvt_all, layer, tabs_qt, tabs_k, *, ts):
    b, s, _ = lat.shape
    hp = MLA_HEADS * HEAD_PAD
    const = lambda *shape: pl.BlockSpec(shape, lambda bi, i: (0,) * len(shape))
    of_layer = lambda *shape: pl.BlockSpec((None,) + shape, lambda bi, i: (layer,) + (0,) * len(shape))
    rows = pl.BlockSpec((1, MLA_HEADS, ts, HEAD_PAD), lambda bi, i: (bi, 0, i, 0))
    cols = pl.BlockSpec((1, MLA_HEADS, HEAD_PAD, ts), lambda bi, i: (bi, 0, 0, i))
    rows_shape = jax.ShapeDtypeStruct((b, MLA_HEADS, s, HEAD_PAD), BF16)
    cols_shape = jax.ShapeDtypeStruct((b, MLA_HEADS, HEAD_PAD, s), BF16)
    return pl.pallas_call(
        _mla_prep_kernel,
        grid=(b, s // ts),
        in_specs=[
            pl.BlockSpec((1, ts, LAT_WIDTH), lambda bi, i: (bi, i, 0)),
            const(1, Q_LORA), const(1, KV_LORA),
            of_layer(hp, Q_LORA), of_layer(KV_LORA, hp), of_layer(hp, KV_LORA),
            pl.BlockSpec((3, LANES, ts), lambda bi, i: (0, 0, i)),
            pl.BlockSpec((3, ts, LANES), lambda bi, i: (0, i, 0)),
        ],
        out_specs=[cols, rows, cols],
        out_shape=[cols_shape, rows_shape, cols_shape],
        compiler_params=pltpu.CompilerParams(
            dimension_semantics=("arbitrary", "arbitrary"), vmem_limit_bytes=VMEM_LIMIT),
        name="mla_prep",
    )(lat, qg.reshape(1, -1), kvg.reshape(1, -1), wqt_all, wk_all, wvt_all, tabs_qt, tabs_k)


def _flash_kernel(qt_ref, k_ref, vt_ref, o_ref, *, tk, qw):
    tq = qt_ref.shape[3]
    s_len = k_ref.shape[2]
    nsub = tk // SUB_K
    nchunk = s_len // tk
    items = [(qb, hh, c) for qb in range(tq // qw) for hh in range(2) for c in range(nchunk)]

    def scores(item, r):
        qb, hh, c = item
        rows = slice(c * tk + r * SUB_K, c * tk + (r + 1) * SUB_K)
        qt = qt_ref[0, hh, :, qb * qw:(qb + 1) * qw]
        return jnp.dot(k_ref[0, hh, rows, :], qt, preferred_element_type=F32)

    def weighted(item, r, p):
        _, hh, c = item
        cols = slice(c * tk + r * SUB_K, c * tk + (r + 1) * SUB_K)
        return jnp.dot(vt_ref[0, hh, 0:VT_ROWS, cols], p, preferred_element_type=F32)

    def fold(state, item, alpha, pv):
        key = item[:2]
        m, acc = state[key]
        state[key] = (m, alpha * acc + pv)

    state = {}
    s_parts = [scores(items[0], r) for r in range(nsub)]
    prev = None
    for t, item in enumerate(items):
        key = item[:2]
        if key not in state:
            state[key] = (jnp.full((1, qw), NEG_BIG, F32), jnp.zeros((VT_ROWS, qw), F32))
        m, acc = state[key]
        m_new = m
        for s in s_parts:
            m_new = jnp.maximum(m_new, jnp.max(s, axis=0, keepdims=True))
        alpha = jnp.exp2(m - m_new)
        state[key] = (m_new, acc)
        nxt, p_parts, pv = [], [], None
        for r in range(nsub):
            shift = m_new
            if t + 1 < len(items):
                nxt.append(scores(items[t + 1], r))
                shift = m_new + nxt[r][0:1, :] * 0.0
            if prev is not None:
                part = weighted(prev[0], r, prev[1][r])
                pv = part if pv is None else pv + part
            p_parts.append(jnp.exp2(s_parts[r] - shift).astype(BF16))
        if prev is not None:
            fold(state, prev[0], prev[2], pv)
        s_parts, prev = nxt, (item, p_parts, alpha)
    pv = None
    for r in range(nsub):
        part = weighted(prev[0], r, prev[1][r])
        pv = part if pv is None else pv + part
    fold(state, prev[0], prev[2], pv)

    for qb in range(tq // qw):
        outs = []
        for hh in range(2):
            acc = state[(qb, hh)][1]
            outs.append(acc[:MLA_V] / acc[MLA_V:MLA_V + 1])
        o_ref[0, qb * qw:(qb + 1) * qw, :] = jnp.concatenate(outs, axis=0).T.astype(o_ref.dtype)


def _mla_flash(qt, k, vt, *, tq, tk, qw):
    b, h, s, _ = k.shape
    return pl.pallas_call(
        functools.partial(_flash_kernel, tk=tk, qw=qw),
        grid=(b, h // 2, s // tq),
        in_specs=[
            pl.BlockSpec((1, 2, HEAD_PAD, tq), lambda bi, hp, i: (bi, hp, 0, i)),
            pl.BlockSpec((1, 2, s, HEAD_PAD), lambda bi, hp, i: (bi, hp, 0, 0)),
            pl.BlockSpec((1, 2, HEAD_PAD, s), lambda bi, hp, i: (bi, hp, 0, 0)),
        ],
        out_specs=pl.BlockSpec((1, tq, 2 * MLA_V), lambda bi, hp, i: (bi, i, hp)),
        out_shape=jax.ShapeDtypeStruct((b, s, MLA_WIDTH), BF16),
        compiler_params=pltpu.CompilerParams(
            dimension_semantics=("arbitrary", "arbitrary", "arbitrary"),
            vmem_limit_bytes=VMEM_LIMIT),
        name="mla_flash",
    )(qt, k, vt)


def _band_kernel(q_ref, kp_ref, km_ref, kn_ref, vp_ref, vm_ref, vn_ref, o_ref, lse_ref,
                 qwin, kwin, vwin, *, half, sub):
    tq = qwin.shape[0]
    nchunk, chunk = q_ref.shape[1], q_ref.shape[3]
    i = pl.program_id(2)
    for c in range(nchunk):
        rows = slice(c * chunk, (c + 1) * chunk)
        qwin[rows] = q_ref[0, c, 0]
        kwin[half + c * chunk:half + (c + 1) * chunk] = km_ref[0, c, 0]
        vwin[half + c * chunk:half + (c + 1) * chunk] = vm_ref[0, c, 0]
    kwin[0:half] = kp_ref[0, 0, 0]
    kwin[half + tq:] = kn_ref[0, 0, 0]
    vwin[0:half] = vp_ref[0, 0, 0]
    vwin[half + tq:] = vn_ref[0, 0, 0]

    wk = sub + 2 * half
    lo = lax.broadcasted_iota(jnp.int32, (sub, LANES), 1) < DIL_HD
    ones = jnp.ones((wk, LANES), BF16)
    row = lax.broadcasted_iota(jnp.int32, (2 * sub, wk), 0) % sub
    col = lax.broadcasted_iota(jnp.int32, (2 * sub, wk), 1)
    rel = col - half - row
    in_band = (rel >= -half) & (rel <= half)
    bias_mid = jnp.where(in_band, 0.0, NEG_BIG)
    bias_first = jnp.where(in_band & (col >= half), 0.0, NEG_BIG)
    bias_last = jnp.where(in_band & (col < half + sub), 0.0, NEG_BIG)
    nstep = tq // sub
    assert nstep >= 2
    at_start = i == 0
    at_end = i == pl.num_programs(2) - 1

    for t in range(nstep):
        r0 = t * sub
        bias = bias_mid
        if t == 0:
            bias = jnp.where(at_start, bias_first, bias)
        if t == nstep - 1:
            bias = jnp.where(at_end, bias_last, bias)
        for c in range(DIL_WIDTH // LANES):
            cs = slice(c * LANES, (c + 1) * LANES)
            qt = qwin[r0:r0 + sub, cs]
            zero = jnp.zeros_like(qt)
            q2 = jnp.concatenate([jnp.where(lo, qt, zero), jnp.where(lo, zero, qt)], axis=0)
            kt = kwin[r0:r0 + wk, cs]
            vt = vwin[r0:r0 + wk, cs]
            s = lax.dot_general(q2, kt, (((1,), (1,)), ((), ())), preferred_element_type=F32) + bias
            m = jnp.max(s, axis=-1, keepdims=True)
            p = jnp.exp2(s - m).astype(BF16)
            ol = jnp.dot(p, jnp.concatenate([vt, ones], axis=1), preferred_element_type=F32)
            o = jnp.where(lo, ol[:sub, :LANES], ol[sub:, :LANES])
            l = jnp.where(lo, ol[:sub, LANES:], ol[sub:, LANES:])
            mm = jnp.where(lo, m[:sub], m[sub:])
            o_ref[0, 0, r0:r0 + sub, cs] = (o / l).astype(o_ref.dtype)
            lse_ref[0, 0, r0:r0 + sub, cs] = mm + jnp.log2(l)


def _band_attention(dil2d, g, *, tm, tq, sub, batch, seq):
    window, d = DIL_PAIRS[g]
    half = window // (2 * d)
    length = seq // d
    tq = min(tq, length)
    chunk = tm // d
    view = dil2d.reshape(batch, seq // tm, d, chunk, dil2d.shape[1])
    rows_blk = min(chunk, tq)
    nchunk = tq // rows_blk

    def main(c):
        col = 3 * g + c
        if chunk >= tq:
            per = chunk // tq
            imap = lambda bi, r, i: (bi, i // per, r, i % per, col)
        else:
            imap = lambda bi, r, i: (bi, i, r, 0, col)
        return pl.BlockSpec((1, nchunk, 1, rows_blk, DIL_WIDTH), imap)

    def edge(c, start_of):
        col = 3 * g + c

        def imap(bi, r, i):
            start = start_of(i)
            return (bi, start // chunk, r, (start % chunk) // half, col)
        return pl.BlockSpec((1, 1, 1, half, DIL_WIDTH), imap)

    before = lambda i: jnp.maximum(i * tq - half, 0)
    after = lambda i: jnp.minimum((i + 1) * tq, length - half)
    out_spec = pl.BlockSpec((1, 1, tq, DIL_WIDTH), lambda bi, r, i: (bi, r, i, 0))
    return pl.pallas_call(
        functools.partial(_band_kernel, half=half, sub=sub),
        grid=(batch, d, length // tq),
        in_specs=[main(0), edge(1, before), main(1), edge(1, after),
                  edge(2, before), main(2), edge(2, after)],
        out_specs=[out_spec, out_spec],
        out_shape=[jax.ShapeDtypeStruct((batch, d, length, DIL_WIDTH), BF16),
                   jax.ShapeDtypeStruct((batch, d, length, DIL_WIDTH), F32)],
        scratch_shapes=[pltpu.VMEM((tq, DIL_WIDTH), BF16),
                        pltpu.VMEM((tq + 2 * half, DIL_WIDTH), BF16),
                        pltpu.VMEM((tq + 2 * half, DIL_WIDTH), BF16)],
        compiler_params=pltpu.CompilerParams(
            dimension_semantics=("arbitrary", "arbitrary", "arbitrary"),
            vmem_limit_bytes=VMEM_LIMIT),
        name=f"band_attn_d{d}",
    )(view, view, view, view, view, view, view)


def _silu(g):
    return g / (1.0 + jnp.exp(-g))


def _out_kernel(x_ref, a_ref, gate_ref, o1, l1, o2, l2, o3, l3, w_ref, fg_ref, y_ref, nat_scr,
                *, final):
    tm = x_ref.shape[0]

    def natural(ref, slot):
        d = ref.shape[1]
        if d == 1:
            return ref[0, 0]
        ntile = ref.shape[3] // LANES
        for r in range(d):
            for c in range(ntile):
                nat_scr[slot * ntile + c, pl.ds(r, tm // d, stride=d), :] = (
                    ref[0, r, :, c * LANES:(c + 1) * LANES].astype(F32))
        return jnp.concatenate([nat_scr[slot * ntile + c] for c in range(ntile)], axis=1)

    la, lb, lc = l1[0, 0], natural(l2, 0), natural(l3, 1)
    mx = jnp.maximum(jnp.maximum(la, lb), lc)
    ea, eb, ec = jnp.exp2(la - mx), jnp.exp2(lb - mx), jnp.exp2(lc - mx)
    bmix = (ea * o1[0, 0] + eb * natural(o2, 2) + ec * natural(o3, 3)) / (ea + eb + ec)
    gates = gate_ref[...].astype(F32)
    mix = jnp.concatenate([a_ref[...] * _silu(gates[:, :MLA_WIDTH]),
                           bmix * _silu(gates[:, MLA_WIDTH:])], axis=-1).astype(BF16)
    y = x_ref[...] + jnp.dot(mix, w_ref[...], preferred_element_type=F32)
    if final:
        y = _rms_f32(y, fg_ref[...])
    y_ref[...] = y


def _out_proj(x2d, a2d, gates, groups, w_out_all, layer, final_g, *, tm, seq, final):
    m, dm = x2d.shape
    nblk = seq // tm
    row = lambda n: pl.BlockSpec((tm, n), lambda i: (i, 0))
    args = [x2d, a2d, gates]
    specs = [row(dm), row(MLA_WIDTH), row(2 * MLA_WIDTH)]
    for (_, d), (o, lse) in zip(DIL_PAIRS, groups):
        spec = pl.BlockSpec((1, d, tm // d, DIL_WIDTH), lambda i: (i // nblk, 0, i % nblk, 0))
        args += [o, lse]
        specs += [spec, spec]
    args += [w_out_all, final_g.reshape(1, dm)]
    specs += [pl.BlockSpec((None, dm, dm), lambda i: (layer, 0, 0)),
              pl.BlockSpec((1, dm), lambda i: (0, 0))]
    return pl.pallas_call(
        functools.partial(_out_kernel, final=final),
        grid=(m // tm,),
        in_specs=specs,
        out_specs=row(dm),
        out_shape=jax.ShapeDtypeStruct((m, dm), F32),
        scratch_shapes=[pltpu.VMEM((4 * DIL_WIDTH // LANES, tm, LANES), F32)],
        compiler_params=pltpu.CompilerParams(
            dimension_semantics=("arbitrary",), vmem_limit_bytes=VMEM_LIMIT),
        name="merge_out_proj",
    )(*args)


def _arrange_w_in(w):
    o_kr = Q_LORA + KV_LORA
    o_ga = o_kr + MLA_ROPE
    o_dil = o_ga + MLA_WIDTH
    o_gb = o_dil + DIL_GROUPS * GROUP_W
    total = o_gb + MLA_WIDTH
    plan = [((o_dil, o_gb), 0), ((0, o_kr), MLA_NOPE), ((o_kr, o_ga), HEAD_PAD - MLA_NOPE - MLA_ROPE),
            ((o_ga, o_dil), 0), ((o_gb, total), 0)]
    width = sum(hi - lo + gap for (lo, hi), gap in plan)
    out, pos = None, 0
    for (lo, hi), gap in plan:
        seg = jnp.pad(w[..., lo:hi], ((0, 0), (0, 0), (pos, width - pos - (hi - lo))))
        out = seg if out is None else out + seg
        pos += hi - lo + gap
    return out.astype(BF16)


def _pad_heads(w, per_head, keep):
    lead = w.shape[:-1]
    wh = w.reshape(*lead, MLA_HEADS, per_head)[..., keep]
    wh = jnp.pad(wh, [(0, 0)] * (wh.ndim - 1) + [(0, HEAD_PAD - wh.shape[-1])])
    return wh.reshape(*lead, MLA_HEADS * HEAD_PAD)


def kernel(x, norm_g, w_in, q_norm_g, kv_norm_g, w_uq, w_ukv, w_out, final_g):
    b, s, dm = x.shape
    m = b * s
    tm_in = 1024
    mla_scale = (MLA_NOPE + MLA_ROPE) ** -0.5 * LOG2E
    dil_scale = DIL_HD ** -0.5 * LOG2E
    tabs_mqt = _mla_tables(s, mla_scale).transpose(0, 2, 1)
    tabs_mk = _mla_tables(s, 1.0)
    tabs_d = _dil_tables(s, math.sqrt(dil_scale), tm_in)

    w_in_all = _arrange_w_in(w_in)
    wqt_all = _pad_heads(w_uq, MLA_NOPE + MLA_ROPE, slice(None)).astype(BF16).swapaxes(1, 2)
    wk_all = _pad_heads(w_ukv, MLA_NOPE + MLA_V, slice(0, MLA_NOPE)).astype(BF16)
    wvt_all = _pad_heads(w_ukv, MLA_NOPE + MLA_V, slice(MLA_NOPE, None)).astype(BF16).swapaxes(1, 2)
    w_out_all = w_out.astype(BF16)

    x2d = x.reshape(m, dm)
    for layer in range(DEPTH):
        dil, lat, gates = _in_proj(x2d, norm_g[layer], w_in_all, layer, tabs_d, tm=tm_in, seq=s)

        qt, kk, vt = _mla_prep(lat.reshape(b, s, -1), q_norm_g[layer], kv_norm_g[layer],
                               wqt_all, wk_all, wvt_all, layer, tabs_mqt, tabs_mk, ts=1024)
        a = _mla_flash(qt, kk, vt, tq=1024, tk=2048, qw=512)

        groups = [_band_attention(dil, gi, tm=tm_in, tq=1024, sub=128, batch=b, seq=s)
                  for gi in range(DIL_GROUPS)]

        x2d = _out_proj(x2d, a.reshape(m, MLA_WIDTH), gates, groups, w_out_all, layer,
                        final_g, tm=512, seq=s, final=(layer == DEPTH - 1))
    return x2d.reshape(b, s, dm)
```

```python
import functools
import math

import numpy as np
import jax
import jax.numpy as jnp
from jax import lax
from jax.experimental import pallas as pl
from jax.experimental.pallas import tpu as pltpu

F32 = jnp.float32
BF16 = jnp.bfloat16

D_MODEL = 1024
DEPTH = 4
MLA_HEADS = 8
MLA_NOPE = 64
MLA_ROPE = 32
MLA_V = 64
Q_LORA = 384
KV_LORA = 256
MLA_WIDTH = MLA_HEADS * MLA_V
DIL_PAIRS = ((128, 1), (512, 4), (2048, 16))
DIL_GROUPS = 3
DIL_HEADS = 8
DIL_HD = 64
DIL_WIDTH = DIL_HEADS * DIL_HD
ROT_DIM = DIL_HD // 4
ROPE_THETA = 500000.0
EPS = 1e-6

LANES = 128
BF16_ROWS = 16
HEAD_PAD = 128
LAT_WIDTH = Q_LORA + KV_LORA + HEAD_PAD
VT_ROWS = -(-(MLA_V + 1) // BF16_ROWS) * BF16_ROWS
GROUP_W = 3 * DIL_WIDTH
ROW_SPLIT = 4
SUB_K = 256
LOG2E = math.log2(math.e)
NEG_BIG = -1e30
VMEM_LIMIT = 48 * 1024 * 1024
VMEM_LIMIT_IN_PROJ = 58 * 1024 * 1024


def _rope_tables(seq, dim):
    inv = 1.0 / (ROPE_THETA ** (jnp.arange(0, dim, 2, dtype=F32) / dim))
    ang = jnp.arange(seq, dtype=F32)[:, None] * inv[None, :]
    return jnp.cos(ang), jnp.sin(ang)


def _mla_tables(seq, scale):
    cos, sin = _rope_tables(seq, MLA_ROPE)
    half = MLA_ROPE // 2
    ones = jnp.ones((seq, MLA_NOPE), F32)
    z = lambda n: jnp.zeros((seq, n), F32)
    c = jnp.concatenate([ones, cos, cos, z(HEAD_PAD - MLA_NOPE - MLA_ROPE)], axis=1)
    a = jnp.concatenate([z(MLA_NOPE), -sin, z(half), z(HEAD_PAD - MLA_NOPE - MLA_ROPE)], axis=1)
    b = jnp.concatenate([z(MLA_NOPE), z(half), sin, z(HEAD_PAD - MLA_NOPE - MLA_ROPE)], axis=1)
    return jnp.stack([c, a, b]) * scale


def _dil_tables(seq, scale, tm):
    half = ROT_DIM // 2
    rest = DIL_HD - ROT_DIM
    rep = LANES // DIL_HD
    inv = 1.0 / (ROPE_THETA ** (jnp.arange(0, ROT_DIM, 2, dtype=F32) / ROT_DIM))
    z = lambda n: jnp.zeros((seq, n), F32)
    out = []
    for _, d in DIL_PAIRS:
        pos = np.arange(seq).reshape(seq // tm, tm // d, d).transpose(0, 2, 1).reshape(seq)
        ang = jnp.asarray(pos, F32)[:, None] * inv[None, :]
        cos, sin = jnp.cos(ang) * scale, jnp.sin(ang) * scale
        c = jnp.concatenate([cos, cos, jnp.full((seq, rest), scale, F32)], axis=1)
        a = jnp.concatenate([-sin, z(half), z(rest)], axis=1)
        b = jnp.concatenate([z(half), sin, z(rest)], axis=1)
        out.append(jnp.stack([jnp.tile(c, (1, rep)), jnp.tile(a, (1, rep)), jnp.tile(b, (1, rep))]))
    return jnp.stack(out)


def _rotate3(blk, tab_c, tab_a, tab_b, shift):
    return (blk * tab_c
            + pltpu.roll(blk, LANES - shift, 1) * tab_a
            + pltpu.roll(blk, shift, 1) * tab_b)


def _rms_f32(x, g):
    ms = jnp.mean(x * x, axis=-1, keepdims=True)
    return x * lax.rsqrt(ms + EPS) * g


def _in_proj_kernel(x_ref, g_ref, w_ref, tab_ref, dil_ref, lat_ref, gate_ref, h_scr, xn_scr):
    tm, k = x_ref.shape
    j = pl.program_id(1)
    span = tm // ROW_SPLIT

    def group_step(gi):
        cols = slice(gi * GROUP_W, (gi + 1) * GROUP_W)
        for part in range(ROW_SPLIT):
            rs = slice(part * span, (part + 1) * span)
            y = jnp.dot(h_scr[gi, rs, :], w_ref[:, cols], preferred_element_type=F32)
            tc, ta, tb = tab_ref[0, 0, rs, :], tab_ref[0, 1, rs, :], tab_ref[0, 2, rs, :]
            for c in range(GROUP_W // LANES):
                sl = slice(c * LANES, (c + 1) * LANES)
                yc = y[:, sl]
                if c < 2 * DIL_WIDTH // LANES:
                    yc = _rotate3(yc, tc, ta, tb, ROT_DIM // 2)
                dil_ref[rs, sl] = yc.astype(BF16)

    def reorder(gi):
        d_prev, d = DIL_PAIRS[gi - 1][1], DIL_PAIRS[gi][1]
        ratio, rows = d // d_prev, tm // d
        last = gi == DIL_GROUPS - 1
        for r in range(d_prev):
            for a in range(ratio):
                dst = slice((a * d_prev + r) * rows, (a * d_prev + r + 1) * rows)
                for c in range(k // LANES):
                    y = xn_scr[gi - 1, c, pl.ds(r * (tm // d_prev) + a, rows, stride=ratio), :]
                    if not last:
                        xn_scr[gi, c, dst, :] = y
                    h_scr[gi, dst, c * LANES:(c + 1) * LANES] = y.astype(BF16)

    assert DIL_PAIRS[0][1] == 1 and all(DIL_PAIRS[g + 1][1] % DIL_PAIRS[g][1] == 0
                                        for g in range(DIL_GROUPS - 1))

    @pl.when(j == 0)
    def _():
        xn = _rms_f32(x_ref[...], g_ref[...])
        h_scr[0] = xn.astype(BF16)
        for c in range(k // LANES):
            xn_scr[0, c] = xn[:, c * LANES:(c + 1) * LANES]
        group_step(0)
        reorder(1)

    @pl.when(j == 1)
    def _():
        group_step(1)
        reorder(2)

    @pl.when(j == 2)
    def _():
        group_step(2)

    lat_start = DIL_GROUPS * GROUP_W
    for step, ref, start in ((DIL_GROUPS, lat_ref, lat_start),
                             (DIL_GROUPS + 1, gate_ref, lat_start + LAT_WIDTH)):
        @pl.when(j == step)
        def _(ref=ref, start=start):
            cols = slice(start, start + ref.shape[1])
            for part in range(ROW_SPLIT):
                rs = slice(part * span, (part + 1) * span)
                y = jnp.dot(h_scr[0, rs, :], w_ref[:, cols], preferred_element_type=F32)
                ref[rs, :] = y.astype(ref.dtype)


def _in_proj(x2d, g, w_all, layer, tabs, *, tm, seq):
    m, k = x2d.shape
    nblk = seq // tm
    return pl.pallas_call(
        _in_proj_kernel,
        grid=(m // tm, DIL_GROUPS + 2),
        in_specs=[
            pl.BlockSpec((tm, k), lambda i, j: (i, 0)),
            pl.BlockSpec((1, k), lambda i, j: (0, 0)),
            pl.BlockSpec((None, k, w_all.shape[2]), lambda i, j: (layer, 0, 0),
                         pipeline_mode=pl.Buffered(1)),
            pl.BlockSpec((1, 3, tm, LANES), lambda i, j: (jnp.minimum(j, DIL_GROUPS - 1), 0, i % nblk, 0)),
        ],
        out_specs=[
            pl.BlockSpec((tm, GROUP_W), lambda i, j: (i, jnp.minimum(j, DIL_GROUPS - 1))),
            pl.BlockSpec((tm, LAT_WIDTH), lambda i, j: (i, 0)),
            pl.BlockSpec((tm, 2 * MLA_WIDTH), lambda i, j: (i, 0)),
        ],
        out_shape=[
            jax.ShapeDtypeStruct((m, DIL_GROUPS * GROUP_W), BF16),
            jax.ShapeDtypeStruct((m, LAT_WIDTH), F32),
            jax.ShapeDtypeStruct((m, 2 * MLA_WIDTH), BF16),
        ],
        scratch_shapes=[pltpu.VMEM((DIL_GROUPS, tm, k), BF16),
                        pltpu.VMEM((DIL_GROUPS - 1, k // LANES, tm, LANES), F32)],
        compiler_params=pltpu.CompilerParams(
            dimension_semantics=("arbitrary", "arbitrary"), vmem_limit_bytes=VMEM_LIMIT_IN_PROJ),
        name="in_proj",
    )(x2d, g.reshape(1, k), w_all, tabs)


def _mla_prep_kernel(lat_ref, qg_ref, kvg_ref, wqt_ref, wk_ref, wvt_ref, tqt_ref, tk_ref,
                     qt_out, k_out, vt_out):
    lat = lat_ref[0]
    ts = lat.shape[0]
    cqn = _rms_f32(lat[:, :Q_LORA], qg_ref[...]).astype(BF16)
    ckvn = _rms_f32(lat[:, Q_LORA:Q_LORA + KV_LORA], kvg_ref[...]).astype(BF16)
    kr = lat[:, Q_LORA + KV_LORA:]
    nt = (((1,), (1,)), ((), ()))
    half = MLA_ROPE // 2

    qt = lax.dot_general(wqt_ref[...], cqn, nt, preferred_element_type=F32)
    qc, qa, qb = tqt_ref[0], tqt_ref[1], tqt_ref[2]
    for h in range(MLA_HEADS):
        blk = qt[h * HEAD_PAD:(h + 1) * HEAD_PAD, :]
        up = jnp.concatenate([blk[half:], blk[:half]], axis=0)
        down = jnp.concatenate([blk[-half:], blk[:-half]], axis=0)
        qt_out[0, h] = (blk * qc + up * qa + down * qb).astype(BF16)

    krp = _rotate3(kr, tk_ref[0], tk_ref[1], tk_ref[2], half)
    kk = jnp.dot(ckvn, wk_ref[...], preferred_element_type=F32)
    for h in range(MLA_HEADS):
        k_out[0, h] = (kk[:, h * HEAD_PAD:(h + 1) * HEAD_PAD] + krp).astype(BF16)

    vt = lax.dot_general(wvt_ref[...], ckvn, nt, preferred_element_type=F32)
    row = lax.broadcasted_iota(jnp.int32, (HEAD_PAD, ts), 0)
    ones_row = (row == MLA_V).astype(F32)
    for h in range(MLA_HEADS):
        vt_out[0, h] = (vt[h * HEAD_PAD:(h + 1) * HEAD_PAD, :] + ones_row).astype(BF16)


def _mla_prep(lat, qg, kvg, wqt_all, wk_all, wvt_all, layer, tabs_qt, tabs_k, *, ts):
    b, s, _ = lat.shape
    hp = MLA_HEADS * HEAD_PAD
    const = lambda *shape: pl.BlockSpec(shape, lambda bi, i: (0,) * len(shape))
    of_layer = lambda *shape: pl.BlockSpec((None,) + shape, lambda bi, i: (layer,) + (0,) * len(shape))
    rows = pl.BlockSpec((1, MLA_HEADS, ts, HEAD_PAD), lambda bi, i: (bi, 0, i, 0))
    cols = pl.BlockSpec((1, MLA_HEADS, HEAD_PAD, ts), lambda bi, i: (bi, 0, 0, i))
    rows_shape = jax.ShapeDtypeStruct((b, MLA_HEADS, s, HEAD_PAD), BF16)
    cols_shape = jax.ShapeDtypeStruct((b, MLA_HEADS, HEAD_PAD, s), BF16)
    return pl.pallas_call(
        _mla_prep_kernel,
        grid=(b, s // ts),
        in_specs=[
            pl.BlockSpec((1, ts, LAT_WIDTH), lambda bi, i: (bi, i, 0)),
            const(1, Q_LORA), const(1, KV_LORA),
            of_layer(hp, Q_LORA), of_layer(KV_LORA, hp), of_layer(hp, KV_LORA),
            pl.BlockSpec((3, LANES, ts), lambda bi, i: (0, 0, i)),
            pl.BlockSpec((3, ts, LANES), lambda bi, i: (0, i, 0)),
        ],
        out_specs=[cols, rows, cols],
        out_shape=[cols_shape, rows_shape, cols_shape],
        compiler_params=pltpu.CompilerParams(
            dimension_semantics=("arbitrary", "arbitrary"), vmem_limit_bytes=VMEM_LIMIT),
        name="mla_prep",
    )(lat, qg.reshape(1, -1), kvg.reshape(1, -1), wqt_all, wk_all, wvt_all, tabs_qt, tabs_k)


def _flash_kernel(qt_ref, k_ref, vt_ref, o_ref, *, tk, qw):
    tq = qt_ref.shape[3]
    s_len = k_ref.shape[2]
    nsub = tk // SUB_K
    nchunk = s_len // tk
    items = [(qb, hh, c) for qb in range(tq // qw) for hh in range(2) for c in range(nchunk)]

    def scores(item, r):
        qb, hh, c = item
        rows = slice(c * tk + r * SUB_K, c * tk + (r + 1) * SUB_K)
        qt = qt_ref[0, hh, :, qb * qw:(qb + 1) * qw]
        return jnp.dot(k_ref[0, hh, rows, :], qt, preferred_element_type=F32)

    def weighted(item, r, p):
        _, hh, c = item
        cols = slice(c * tk + r * SUB_K, c * tk + (r + 1) * SUB_K)
        return jnp.dot(vt_ref[0, hh, 0:VT_ROWS, cols], p, preferred_element_type=F32)

    def fold(state, item, alpha, pv):
        key = item[:2]
        m, acc = state[key]
        state[key] = (m, alpha * acc + pv)

    state = {}
    s_parts = [scores(items[0], r) for r in range(nsub)]
    prev = None
    for t, item in enumerate(items):
        key = item[:2]
        if key not in state:
            state[key] = (jnp.full((1, qw), NEG_BIG, F32), jnp.zeros((VT_ROWS, qw), F32))
        m, acc = state[key]
        m_new = m
        for s in s_parts:
            m_new = jnp.maximum(m_new, jnp.max(s, axis=0, keepdims=True))
        alpha = jnp.exp2(m - m_new)
        state[key] = (m_new, acc)
        nxt, p_parts, pv = [], [], None
        for r in range(nsub):
            shift = m_new
            if t + 1 < len(items):
                nxt.append(scores(items[t + 1], r))
                shift = m_new + nxt[r][0:1, :] * 0.0
            if prev is not None:
                part = weighted(prev[0], r, prev[1][r])
                pv = part if pv is None else pv + part
            p_parts.append(jnp.exp2(s_parts[r] - shift).astype(BF16))
        if prev is not None:
            fold(state, prev[0], prev[2], pv)
        s_parts, prev = nxt, (item, p_parts, alpha)
    pv = None
    for r in range(nsub):
        part = weighted(prev[0], r, prev[1][r])
        pv = part if pv is None else pv + part
    fold(state, prev[0], prev[2], pv)

    for qb in range(tq // qw):
        outs = []
        for hh in range(2):
            acc = state[(qb, hh)][1]
            outs.append(acc[:MLA_V] / acc[MLA_V:MLA_V + 1])
        o_ref[0, qb * qw:(qb + 1) * qw, :] = jnp.concatenate(outs, axis=0).T.astype(o_ref.dtype)


def _mla_flash(qt, k, vt, *, tq, tk, qw):
    b, h, s, _ = k.shape
    return pl.pallas_call(
        functools.partial(_flash_kernel, tk=tk, qw=qw),
        grid=(b, h // 2, s // tq),
        in_specs=[
            pl.BlockSpec((1, 2, HEAD_PAD, tq), lambda bi, hp, i: (bi, hp, 0, i)),
            pl.BlockSpec((1, 2, s, HEAD_PAD), lambda bi, hp, i: (bi, hp, 0, 0)),
            pl.BlockSpec((1, 2, HEAD_PAD, s), lambda bi, hp, i: (bi, hp, 0, 0)),
        ],
        out_specs=pl.BlockSpec((1, tq, 2 * MLA_V), lambda bi, hp, i: (bi, i, hp)),
        out_shape=jax.ShapeDtypeStruct((b, s, MLA_WIDTH), BF16),
        compiler_params=pltpu.CompilerParams(
            dimension_semantics=("arbitrary", "arbitrary", "arbitrary"),
            vmem_limit_bytes=VMEM_LIMIT),
        name="mla_flash",
    )(qt, k, vt)


def _band_kernel(q_ref, kp_ref, km_ref, kn_ref, vp_ref, vm_ref, vn_ref, o_ref, lse_ref,
                 qwin, kwin, vwin, *, half, sub):
    nres, tq = qwin.shape[0], qwin.shape[1]
    nchunk, chunk = q_ref.shape[1], q_ref.shape[3]
    i = pl.program_id(2)
    for rr in range(nres):
        for c in range(nchunk):
            rows = slice(c * chunk, (c + 1) * chunk)
            qwin[rr, rows] = q_ref[0, c, rr]
            kwin[rr, half + c * chunk:half + (c + 1) * chunk] = km_ref[0, c, rr]
            vwin[rr, half + c * chunk:half + (c + 1) * chunk] = vm_ref[0, c, rr]
        kwin[rr, 0:half] = kp_ref[0, 0, rr]
        kwin[rr, half + tq:] = kn_ref[0, 0, rr]
        vwin[rr, 0:half] = vp_ref[0, 0, rr]
        vwin[rr, half + tq:] = vn_ref[0, 0, rr]

    wk = sub + 2 * half
    lo = lax.broadcasted_iota(jnp.int32, (sub, LANES), 1) < DIL_HD
    ones = jnp.ones((wk, LANES), BF16)
    row = lax.broadcasted_iota(jnp.int32, (2 * sub, wk), 0) % sub
    col = lax.broadcasted_iota(jnp.int32, (2 * sub, wk), 1)
    rel = col - half - row
    in_band = (rel >= -half) & (rel <= half)
    bias_mid = jnp.where(in_band, 0.0, NEG_BIG)
    bias_first = jnp.where(in_band & (col >= half), 0.0, NEG_BIG)
    bias_last = jnp.where(in_band & (col < half + sub), 0.0, NEG_BIG)
    nstep = tq // sub
    assert nstep >= 2
    at_start = i == 0
    at_end = i == pl.num_programs(2) - 1

    for rr, t in [(rr, t) for rr in range(nres) for t in range(nstep)]:
        r0 = t * sub
        bias = bias_mid
        if t == 0:
            bias = jnp.where(at_start, bias_first, bias)
        if t == nstep - 1:
            bias = jnp.where(at_end, bias_last, bias)
        for c in range(DIL_WIDTH // LANES):
            cs = slice(c * LANES, (c + 1) * LANES)
            qt = qwin[rr, r0:r0 + sub, cs]
            zero = jnp.zeros_like(qt)
            q2 = jnp.concatenate([jnp.where(lo, qt, zero), jnp.where(lo, zero, qt)], axis=0)
            kt = kwin[rr, r0:r0 + wk, cs]
            vt = vwin[rr, r0:r0 + wk, cs]
            s = lax.dot_general(q2, kt, (((1,), (1,)), ((), ())), preferred_element_type=F32) + bias
            m = jnp.max(s, axis=-1, keepdims=True)
            p = jnp.exp2(s - m).astype(BF16)
            ol = jnp.dot(p, jnp.concatenate([vt, ones], axis=1), preferred_element_type=F32)
            o = jnp.where(lo, ol[:sub, :LANES], ol[sub:, :LANES])
            l = jnp.where(lo, ol[:sub, LANES:], ol[sub:, LANES:])
            mm = jnp.where(lo, m[:sub], m[sub:])
            o_ref[0, rr, r0:r0 + sub, cs] = (o / l).astype(o_ref.dtype)
            lse_ref[0, rr, r0:r0 + sub, cs] = mm + jnp.log2(l)


def _band_attention(dil2d, g, *, tm, rows_per_step, sub, batch, seq):
    window, d = DIL_PAIRS[g]
    half = window // (2 * d)
    length = seq // d
    tq = min(rows_per_step, length)
    chunk = tm // d
    view = dil2d.reshape(batch, seq // tm, d, chunk, dil2d.shape[1])
    rows_blk = min(chunk, tq)
    nchunk = tq // rows_blk
    nres = min(d, max(1, rows_per_step // tq))

    def main(c):
        col = 3 * g + c
        if chunk >= tq:
            per = chunk // tq
            imap = lambda bi, r, i: (bi, i // per, r, i % per, col)
        else:
            imap = lambda bi, r, i: (bi, i, r, 0, col)
        return pl.BlockSpec((1, nchunk, nres, rows_blk, DIL_WIDTH), imap)

    def edge(c, start_of):
        col = 3 * g + c

        def imap(bi, r, i):
            start = start_of(i)
            return (bi, start // chunk, r, (start % chunk) // half, col)
        return pl.BlockSpec((1, 1, nres, half, DIL_WIDTH), imap)

    before = lambda i: jnp.maximum(i * tq - half, 0)
    after = lambda i: jnp.minimum((i + 1) * tq, length - half)
    out_spec = pl.BlockSpec((1, nres, tq, DIL_WIDTH), lambda bi, r, i: (bi, r, i, 0))
    return pl.pallas_call(
        functools.partial(_band_kernel, half=half, sub=sub),
        grid=(batch, d // nres, length // tq),
        in_specs=[main(0), edge(1, before), main(1), edge(1, after),
                  edge(2, before), main(2), edge(2, after)],
        out_specs=[out_spec, out_spec],
        out_shape=[jax.ShapeDtypeStruct((batch, d, length, DIL_WIDTH), BF16),
                   jax.ShapeDtypeStruct((batch, d, length, DIL_WIDTH), F32)],
        scratch_shapes=[pltpu.VMEM((nres, tq, DIL_WIDTH), BF16),
                        pltpu.VMEM((nres, tq + 2 * half, DIL_WIDTH), BF16),
                        pltpu.VMEM((nres, tq + 2 * half, DIL_WIDTH), BF16)],
        compiler_params=pltpu.CompilerParams(
            dimension_semantics=("arbitrary", "arbitrary", "arbitrary"),
            vmem_limit_bytes=VMEM_LIMIT),
        name=f"band_attn_d{d}",
    )(view, view, view, view, view, view, view)


def _silu(g):
    return g / (1.0 + jnp.exp(-g))


def _out_kernel(x_ref, a_ref, gate_ref, o1, l1, o2, l2, o3, l3, w_ref, fg_ref, y_ref, nat_scr,
                *, final):
    tm = x_ref.shape[0]

    def natural(ref, slot):
        d = ref.shape[1]
        if d == 1:
            return ref[0, 0]
        ntile = ref.shape[3] // LANES
        for r in range(d):
            for c in range(ntile):
                nat_scr[slot * ntile + c, pl.ds(r, tm // d, stride=d), :] = (
                    ref[0, r, :, c * LANES:(c + 1) * LANES].astype(F32))
        return jnp.concatenate([nat_scr[slot * ntile + c] for c in range(ntile)], axis=1)

    la, lb, lc = l1[0, 0], natural(l2, 0), natural(l3, 1)
    mx = jnp.maximum(jnp.maximum(la, lb), lc)
    ea, eb, ec = jnp.exp2(la - mx), jnp.exp2(lb - mx), jnp.exp2(lc - mx)
    bmix = (ea * o1[0, 0] + eb * natural(o2, 2) + ec * natural(o3, 3)) / (ea + eb + ec)
    gates = gate_ref[...].astype(F32)
    mix = jnp.concatenate([a_ref[...] * _silu(gates[:, :MLA_WIDTH]),
                           bmix * _silu(gates[:, MLA_WIDTH:])], axis=-1).astype(BF16)
    y = x_ref[...] + jnp.dot(mix, w_ref[...], preferred_element_type=F32)
    if final:
        y = _rms_f32(y, fg_ref[...])
    y_ref[...] = y


def _out_proj(x2d, a2d, gates, groups, w_out_all, layer, final_g, *, tm, seq, final):
    m, dm = x2d.shape
    nblk = seq // tm
    row = lambda n: pl.BlockSpec((tm, n), lambda i: (i, 0))
    args = [x2d, a2d, gates]
    specs = [row(dm), row(MLA_WIDTH), row(2 * MLA_WIDTH)]
    for (_, d), (o, lse) in zip(DIL_PAIRS, groups):
        spec = pl.BlockSpec((1, d, tm // d, DIL_WIDTH), lambda i: (i // nblk, 0, i % nblk, 0))
        args += [o, lse]
        specs += [spec, spec]
    args += [w_out_all, final_g.reshape(1, dm)]
    specs += [pl.BlockSpec((None, dm, dm), lambda i: (layer, 0, 0)),
              pl.BlockSpec((1, dm), lambda i: (0, 0))]
    return pl.pallas_call(
        functools.partial(_out_kernel, final=final),
        grid=(m // tm,),
        in_specs=specs,
        out_specs=row(dm),
        out_shape=jax.ShapeDtypeStruct((m, dm), F32),
        scratch_shapes=[pltpu.VMEM((4 * DIL_WIDTH // LANES, tm, LANES), F32)],
        compiler_params=pltpu.CompilerParams(
            dimension_semantics=("arbitrary",), vmem_limit_bytes=VMEM_LIMIT),
        name="merge_out_proj",
    )(*args)


def _arrange_w_in(w):
    o_kr = Q_LORA + KV_LORA
    o_ga = o_kr + MLA_ROPE
    o_dil = o_ga + MLA_WIDTH
    o_gb = o_dil + DIL_GROUPS * GROUP_W
    total = o_gb + MLA_WIDTH
    plan = [((o_dil, o_gb), 0), ((0, o_kr), MLA_NOPE), ((o_kr, o_ga), HEAD_PAD - MLA_NOPE - MLA_ROPE),
            ((o_ga, o_dil), 0), ((o_gb, total), 0)]
    width = sum(hi - lo + gap for (lo, hi), gap in plan)
    out, pos = None, 0
    for (lo, hi), gap in plan:
        seg = jnp.pad(w[..., lo:hi], ((0, 0), (0, 0), (pos, width - pos - (hi - lo))))
        out = seg if out is None else out + seg
        pos += hi - lo + gap
    return out.astype(BF16)


def _pad_heads(w, per_head, keep):
    lead = w.shape[:-1]
    wh = w.reshape(*lead, MLA_HEADS, per_head)[..., keep]
    wh = jnp.pad(wh, [(0, 0)] * (wh.ndim - 1) + [(0, HEAD_PAD - wh.shape[-1])])
    return wh.reshape(*lead, MLA_HEADS * HEAD_PAD)


def kernel(x, norm_g, w_in, q_norm_g, kv_norm_g, w_uq, w_ukv, w_out, final_g):
    b, s, dm = x.shape
    m = b * s
    tm_in = 1024
    mla_scale = (MLA_NOPE + MLA_ROPE) ** -0.5 * LOG2E
    dil_scale = DIL_HD ** -0.5 * LOG2E
    tabs_mqt = _mla_tables(s, mla_scale).transpose(0, 2, 1)
    tabs_mk = _mla_tables(s, 1.0)
    tabs_d = _dil_tables(s, math.sqrt(dil_scale), tm_in)

    w_in_all = _arrange_w_in(w_in)
    wqt_all = _pad_heads(w_uq, MLA_NOPE + MLA_ROPE, slice(None)).astype(BF16).swapaxes(1, 2)
    wk_all = _pad_heads(w_ukv, MLA_NOPE + MLA_V, slice(0, MLA_NOPE)).astype(BF16)
    wvt_all = _pad_heads(w_ukv, MLA_NOPE + MLA_V, slice(MLA_NOPE, None)).astype(BF16).swapaxes(1, 2)
    w_out_all = w_out.astype(BF16)

    x2d = x.reshape(m, dm)
    for layer in range(DEPTH):
        dil, lat, gates = _in_proj(x2d, norm_g[layer], w_in_all, layer, tabs_d, tm=tm_in, seq=s)

        qt, kk, vt = _mla_prep(lat.reshape(b, s, -1), q_norm_g[layer], kv_norm_g[layer],
                               wqt_all, wk_all, wvt_all, layer, tabs_mqt, tabs_mk, ts=1024)
        a = _mla_flash(qt, kk, vt, tq=1024, tk=2048, qw=512)

        groups = [_band_attention(dil, gi, tm=tm_in, rows_per_step=1024, sub=128, batch=b, seq=s)
                  for gi in range(DIL_GROUPS)]

        x2d = _out_proj(x2d, a.reshape(m, MLA_WIDTH), gates, groups, w_out_all, layer,
                        final_g, tm=512, seq=s, final=(layer == DEPTH - 1))
    return x2d.reshape(b, s, dm)
```

```python
import functools
import math

import numpy as np
import jax
import jax.numpy as jnp
from jax import lax
from jax.experimental import pallas as pl
from jax.experimental.pallas import tpu as pltpu

F32 = jnp.float32
BF16 = jnp.bfloat16

D_MODEL = 1024
DEPTH = 4
MLA_HEADS = 8
MLA_NOPE = 64
MLA_ROPE = 32
MLA_V = 64
Q_LORA = 384
KV_LORA = 256
MLA_WIDTH = MLA_HEADS * MLA_V
DIL_PAIRS = ((128, 1), (512, 4), (2048, 16))
DIL_GROUPS = 3
DIL_HEADS = 8
DIL_HD = 64
DIL_WIDTH = DIL_HEADS * DIL_HD
ROT_DIM = DIL_HD // 4
ROPE_THETA = 500000.0
EPS = 1e-6

LANES = 128
BF16_ROWS = 16
HEAD_PAD = 128
LAT_WIDTH = Q_LORA + KV_LORA + HEAD_PAD
VT_ROWS = -(-(MLA_V + 1) // BF16_ROWS) * BF16_ROWS
GROUP_W = 3 * DIL_WIDTH
ROW_SPLIT = 4
SUB_K = 256
LOG2E = math.log2(math.e)
NEG_INF = float("-inf")
VMEM_LIMIT = 48 * 1024 * 1024
VMEM_LIMIT_IN_PROJ = 58 * 1024 * 1024


def _rope_tables(seq, dim):
    inv = 1.0 / (ROPE_THETA ** (jnp.arange(0, dim, 2, dtype=F32) / dim))
    ang = jnp.arange(seq, dtype=F32)[:, None] * inv[None, :]
    return jnp.cos(ang), jnp.sin(ang)


def _mla_tables(seq, scale):
    cos, sin = _rope_tables(seq, MLA_ROPE)
    half = MLA_ROPE // 2
    ones = jnp.ones((seq, MLA_NOPE), F32)
    z = lambda n: jnp.zeros((seq, n), F32)
    c = jnp.concatenate([ones, cos, cos, z(HEAD_PAD - MLA_NOPE - MLA_ROPE)], axis=1)
    a = jnp.concatenate([z(MLA_NOPE), -sin, z(half), z(HEAD_PAD - MLA_NOPE - MLA_ROPE)], axis=1)
    b = jnp.concatenate([z(MLA_NOPE), z(half), sin, z(HEAD_PAD - MLA_NOPE - MLA_ROPE)], axis=1)
    return jnp.stack([c, a, b]) * scale


def _dil_tables(seq, scale, tm):
    half = ROT_DIM // 2
    rest = DIL_HD - ROT_DIM
    rep = LANES // DIL_HD
    inv = 1.0 / (ROPE_THETA ** (jnp.arange(0, ROT_DIM, 2, dtype=F32) / ROT_DIM))
    z = lambda n: jnp.zeros((seq, n), F32)
    out = []
    for _, d in DIL_PAIRS:
        pos = np.arange(seq).reshape(seq // tm, tm // d, d).transpose(0, 2, 1).reshape(seq)
        ang = jnp.asarray(pos, F32)[:, None] * inv[None, :]
        cos, sin = jnp.cos(ang) * scale, jnp.sin(ang) * scale
        c = jnp.concatenate([cos, cos, jnp.full((seq, rest), scale, F32)], axis=1)
        a = jnp.concatenate([-sin, z(half), z(rest)], axis=1)
        b = jnp.concatenate([z(half), sin, z(rest)], axis=1)
        out.append(jnp.stack([jnp.tile(c, (1, rep)), jnp.tile(a, (1, rep)), jnp.tile(b, (1, rep))]))
    return jnp.stack(out)


def _rotate3(blk, tab_c, tab_a, tab_b, shift):
    return (blk * tab_c
            + pltpu.roll(blk, LANES - shift, 1) * tab_a
            + pltpu.roll(blk, shift, 1) * tab_b)


def _rms_f32(x, g):
    ms = jnp.mean(x * x, axis=-1, keepdims=True)
    return x * lax.rsqrt(ms + EPS) * g


def _in_proj_kernel(x_ref, g_ref, w_ref, tab_ref, dil_ref, lat_ref, gate_ref, h_scr, xn_scr):
    tm, k = x_ref.shape
    j = pl.program_id(1)
    span = tm // ROW_SPLIT

    def group_step(gi):
        cols = slice(gi * GROUP_W, (gi + 1) * GROUP_W)
        for part in range(ROW_SPLIT):
            rs = slice(part * span, (part + 1) * span)
            y = jnp.dot(h_scr[gi, rs, :], w_ref[:, cols], preferred_element_type=F32)
            tc, ta, tb = tab_ref[0, 0, rs, :], tab_ref[0, 1, rs, :], tab_ref[0, 2, rs, :]
            for c in range(GROUP_W // LANES):
                sl = slice(c * LANES, (c + 1) * LANES)
                yc = y[:, sl]
                if c < 2 * DIL_WIDTH // LANES:
                    yc = _rotate3(yc, tc, ta, tb, ROT_DIM // 2)
                dil_ref[rs, sl] = yc.astype(BF16)

    def reorder(gi):
        d_prev, d = DIL_PAIRS[gi - 1][1], DIL_PAIRS[gi][1]
        ratio, rows = d // d_prev, tm // d
        last = gi == DIL_GROUPS - 1
        for r in range(d_prev):
            for a in range(ratio):
                dst = slice((a * d_prev + r) * rows, (a * d_prev + r + 1) * rows)
                for c in range(k // LANES):
                    y = xn_scr[gi - 1, c, pl.ds(r * (tm // d_prev) + a, rows, stride=ratio), :]
                    if not last:
                        xn_scr[gi, c, dst, :] = y
                    h_scr[gi, dst, c * LANES:(c + 1) * LANES] = y.astype(BF16)

    assert DIL_PAIRS[0][1] == 1 and all(DIL_PAIRS[g + 1][1] % DIL_PAIRS[g][1] == 0
                                        for g in range(DIL_GROUPS - 1))

    @pl.when(j == 0)
    def _():
        xn = _rms_f32(x_ref[...], g_ref[...])
        h_scr[0] = xn.astype(BF16)
        for c in range(k // LANES):
            xn_scr[0, c] = xn[:, c * LANES:(c + 1) * LANES]
        group_step(0)
        reorder(1)

    @pl.when(j == 1)
    def _():
        group_step(1)
        reorder(2)

    @pl.when(j == 2)
    def _():
        group_step(2)

    lat_start = DIL_GROUPS * GROUP_W
    for step, ref, start in ((DIL_GROUPS, lat_ref, lat_start),
                             (DIL_GROUPS + 1, gate_ref, lat_start + LAT_WIDTH)):
        @pl.when(j == step)
        def _(ref=ref, start=start):
            cols = slice(start, start + ref.shape[1])
            for part in range(ROW_SPLIT):
                rs = slice(part * span, (part + 1) * span)
                y = jnp.dot(h_scr[0, rs, :], w_ref[:, cols], preferred_element_type=F32)
                ref[rs, :] = y.astype(ref.dtype)


def _in_proj(x2d, g, w_all, layer, tabs, *, tm, seq):
    m, k = x2d.shape
    nblk = seq // tm
    return pl.pallas_call(
        _in_proj_kernel,
        grid=(m // tm, DIL_GROUPS + 2),
        in_specs=[
            pl.BlockSpec((tm, k), lambda i, j: (i, 0)),
            pl.BlockSpec((1, k), lambda i, j: (0, 0)),
            pl.BlockSpec((None, k, w_all.shape[2]), lambda i, j: (layer, 0, 0),
                         pipeline_mode=pl.Buffered(1)),
            pl.BlockSpec((1, 3, tm, LANES), lambda i, j: (jnp.minimum(j, DIL_GROUPS - 1), 0, i % nblk, 0)),
        ],
        out_specs=[
            pl.BlockSpec((tm, GROUP_W), lambda i, j: (i, jnp.minimum(j, DIL_GROUPS - 1))),
            pl.BlockSpec((tm, LAT_WIDTH), lambda i, j: (i, 0)),
            pl.BlockSpec((tm, 2 * MLA_WIDTH), lambda i, j: (i, 0)),
        ],
        out_shape=[
            jax.ShapeDtypeStruct((m, DIL_GROUPS * GROUP_W), BF16),
            jax.ShapeDtypeStruct((m, LAT_WIDTH), F32),
            jax.ShapeDtypeStruct((m, 2 * MLA_WIDTH), BF16),
        ],
        scratch_shapes=[pltpu.VMEM((DIL_GROUPS, tm, k), BF16),
                        pltpu.VMEM((DIL_GROUPS - 1, k // LANES, tm, LANES), F32)],
        compiler_params=pltpu.CompilerParams(
            dimension_semantics=("arbitrary", "arbitrary"), vmem_limit_bytes=VMEM_LIMIT_IN_PROJ),
        name="in_proj",
    )(x2d, g.reshape(1, k), w_all, tabs)


def _mla_prep_kernel(lat_ref, qg_ref, kvg_ref, wqt_ref, wk_ref, wvt_ref, tqt_ref, tk_ref,
                     qt_out, k_out, vt_out):
    lat = lat_ref[0]
    ts = lat.shape[0]
    cqn = _rms_f32(lat[:, :Q_LORA], qg_ref[...]).astype(BF16)
    ckvn = _rms_f32(lat[:, Q_LORA:Q_LORA + KV_LORA], kvg_ref[...]).astype(BF16)
    kr = lat[:, Q_LORA + KV_LORA:]
    nt = (((1,), (1,)), ((), ()))
    half = MLA_ROPE // 2

    qt = lax.dot_general(wqt_ref[...], cqn, nt, preferred_element_type=F32)
    qc, qa, qb = tqt_ref[0], tqt_ref[1], tqt_ref[2]
    for h in range(MLA_HEADS):
        blk = qt[h * HEAD_PAD:(h + 1) * HEAD_PAD, :]
        up = jnp.concatenate([blk[half:], blk[:half]], axis=0)
        down = jnp.concatenate([blk[-half:], blk[:-half]], axis=0)
        qt_out[0, h] = (blk * qc + up * qa + down * qb).astype(BF16)

    krp = _rotate3(kr, tk_ref[0], tk_ref[1], tk_ref[2], half)
    kk = jnp.dot(ckvn, wk_ref[...], preferred_element_type=F32)
    for h in range(MLA_HEADS):
        k_out[0, h] = (kk[:, h * HEAD_PAD:(h + 1) * HEAD_PAD] + krp).astype(BF16)

    vt = lax.dot_general(wvt_ref[...], ckvn, nt, preferred_element_type=F32)
    row = lax.broadcasted_iota(jnp.int32, (HEAD_PAD, ts), 0)
    ones_row = (row == MLA_V).astype(F32)
    for h in range(MLA_HEADS):
        vt_out[0, h] = (vt[h * HEAD_PAD:(h + 1) * HEAD_PAD, :] + ones_row).astype(BF16)


def _mla_prep(lat, qg, kvg, wqt_all, wk_all, wvt_all, layer, tabs_qt, tabs_k, *, ts):
    b, s, _ = lat.shape
    hp = MLA_HEADS * HEAD_PAD
    const = lambda *shape: pl.BlockSpec(shape, lambda bi, i: (0,) * len(shape))
    of_layer = lambda *shape: pl.BlockSpec((None,) + shape, lambda bi, i: (layer,) + (0,) * len(shape))
    rows = pl.BlockSpec((1, MLA_HEADS, ts, HEAD_PAD), lambda bi, i: (bi, 0, i, 0))
    cols = pl.BlockSpec((1, MLA_HEADS, HEAD_PAD, ts), lambda bi, i: (bi, 0, 0, i))
    rows_shape = jax.ShapeDtypeStruct((b, MLA_HEADS, s, HEAD_PAD), BF16)
    cols_shape = jax.ShapeDtypeStruct((b, MLA_HEADS, HEAD_PAD, s), BF16)
    return pl.pallas_call(
        _mla_prep_kernel,
        grid=(b, s // ts),
        in_specs=[
            pl.BlockSpec((1, ts, LAT_WIDTH), lambda bi, i: (bi, i, 0)),
            const(1, Q_LORA), const(1, KV_LORA),
            of_layer(hp, Q_LORA), of_layer(KV_LORA, hp), of_layer(hp, KV_LORA),
            pl.BlockSpec((3, LANES, ts), lambda bi, i: (0, 0, i)),
            pl.BlockSpec((3, ts, LANES), lambda bi, i: (0, i, 0)),
        ],
        out_specs=[cols, rows, cols],
        out_shape=[cols_shape, rows_shape, cols_shape],
        compiler_params=pltpu.CompilerParams(
            dimension_semantics=("arbitrary", "arbitrary"), vmem_limit_bytes=VMEM_LIMIT),
        name="mla_prep",
    )(lat, qg.reshape(1, -1), kvg.reshape(1, -1), wqt_all, wk_all, wvt_all, tabs_qt, tabs_k)


def _flash_kernel(qt_ref, k_ref, vt_ref, o_ref, *, tk, qw):
    tq = qt_ref.shape[3]
    s_len = k_ref.shape[2]
    nsub = tk // SUB_K
    nchunk = s_len // tk
    items = [(qb, hh, c) for qb in range(tq // qw) for hh in range(2) for c in range(nchunk)]

    def scores(item, r):
        qb, hh, c = item
        rows = slice(c * tk + r * SUB_K, c * tk + (r + 1) * SUB_K)
        qt = qt_ref[0, hh, :, qb * qw:(qb + 1) * qw]
        return jnp.dot(k_ref[0, hh, rows, :], qt, preferred_element_type=F32)

    def weighted(item, r, p):
        _, hh, c = item
        cols = slice(c * tk + r * SUB_K, c * tk + (r + 1) * SUB_K)
        return jnp.dot(vt_ref[0, hh, 0:VT_ROWS, cols], p, preferred_element_type=F32)

    def fold(state, item, alpha, pv):
        key = item[:2]
        m, acc = state[key]
        state[key] = (m, alpha * acc + pv)

    state = {}
    s_parts = [scores(items[0], r) for r in range(nsub)]
    prev = None
    for t, item in enumerate(items):
        key = item[:2]
        if key not in state:
            state[key] = (jnp.full((1, qw), NEG_INF, F32), jnp.zeros((VT_ROWS, qw), F32))
        m, acc = state[key]
        m_new = m
        for s in s_parts:
            m_new = jnp.maximum(m_new, jnp.max(s, axis=0, keepdims=True))
        alpha = jnp.exp2(m - m_new)
        state[key] = (m_new, acc)
        nxt, p_parts, pv = [], [], None
        for r in range(nsub):
            shift = m_new
            if t + 1 < len(items):
                nxt.append(scores(items[t + 1], r))
                shift = m_new + nxt[r][0:1, :] * 0.0
            if prev is not None:
                part = weighted(prev[0], r, prev[1][r])
                pv = part if pv is None else pv + part
            p_parts.append(jnp.exp2(s_parts[r] - shift).astype(BF16))
        if prev is not None:
            fold(state, prev[0], prev[2], pv)
        s_parts, prev = nxt, (item, p_parts, alpha)
    pv = None
    for r in range(nsub):
        part = weighted(prev[0], r, prev[1][r])
        pv = part if pv is None else pv + part
    fold(state, prev[0], prev[2], pv)

    for qb in range(tq // qw):
        outs = []
        for hh in range(2):
            acc = state[(qb, hh)][1]
            outs.append(acc[:MLA_V] / acc[MLA_V:MLA_V + 1])
        o_ref[0, qb * qw:(qb + 1) * qw, :] = jnp.concatenate(outs, axis=0).T.astype(o_ref.dtype)


def _mla_flash(qt, k, vt, *, tq, tk, qw):
    b, h, s, _ = k.shape
    return pl.pallas_call(
        functools.partial(_flash_kernel, tk=tk, qw=qw),
        grid=(b, h // 2, s // tq),
        in_specs=[
            pl.BlockSpec((1, 2, HEAD_PAD, tq), lambda bi, hp, i: (bi, hp, 0, i)),
            pl.BlockSpec((1, 2, s, HEAD_PAD), lambda bi, hp, i: (bi, hp, 0, 0)),
            pl.BlockSpec((1, 2, HEAD_PAD, s), lambda bi, hp, i: (bi, hp, 0, 0)),
        ],
        out_specs=pl.BlockSpec((1, tq, 2 * MLA_V), lambda bi, hp, i: (bi, i, hp)),
        out_shape=jax.ShapeDtypeStruct((b, s, MLA_WIDTH), BF16),
        compiler_params=pltpu.CompilerParams(
            dimension_semantics=("arbitrary", "arbitrary", "arbitrary"),
            vmem_limit_bytes=VMEM_LIMIT),
        name="mla_flash",
    )(qt, k, vt)


def _band_kernel(q_ref, kp_ref, km_ref, kn_ref, vp_ref, vm_ref, vn_ref, o_ref, lse_ref,
                 qwin, kwin, vwin, *, half, sub):
    nres, tq = qwin.shape[0], qwin.shape[1]
    nchunk, chunk = q_ref.shape[1], q_ref.shape[3]
    i = pl.program_id(2)
    for rr in range(nres):
        for c in range(nchunk):
            rows = slice(c * chunk, (c + 1) * chunk)
            qwin[rr, rows] = q_ref[0, c, rr]
            kwin[rr, half + c * chunk:half + (c + 1) * chunk] = km_ref[0, c, rr]
            vwin[rr, half + c * chunk:half + (c + 1) * chunk] = vm_ref[0, c, rr]
        kwin[rr, 0:half] = kp_ref[0, 0, rr]
        kwin[rr, half + tq:] = kn_ref[0, 0, rr]
        vwin[rr, 0:half] = vp_ref[0, 0, rr]
        vwin[rr, half + tq:] = vn_ref[0, 0, rr]

    wk = sub + 2 * half
    lo = lax.broadcasted_iota(jnp.int32, (sub, LANES), 1) < DIL_HD
    ones = jnp.ones((wk, LANES), BF16)
    row = lax.broadcasted_iota(jnp.int32, (2 * sub, wk), 0) % sub
    col = lax.broadcasted_iota(jnp.int32, (2 * sub, wk), 1)
    rel = col - half - row
    in_band = (rel >= -half) & (rel <= half)
    bias_mid = jnp.where(in_band, 0.0, NEG_INF)
    bias_first = jnp.where(in_band & (col >= half), 0.0, NEG_INF)
    bias_last = jnp.where(in_band & (col < half + sub), 0.0, NEG_INF)
    nstep = tq // sub
    assert nstep >= 2
    at_start = i == 0
    at_end = i == pl.num_programs(2) - 1

    for rr, t in [(rr, t) for rr in range(nres) for t in range(nstep)]:
        r0 = t * sub
        bias = bias_mid
        if t == 0:
            bias = jnp.where(at_start, bias_first, bias)
        if t == nstep - 1:
            bias = jnp.where(at_end, bias_last, bias)
        for c in range(DIL_WIDTH // LANES):
            cs = slice(c * LANES, (c + 1) * LANES)
            qt = qwin[rr, r0:r0 + sub, cs]
            zero = jnp.zeros_like(qt)
            q2 = jnp.concatenate([jnp.where(lo, qt, zero), jnp.where(lo, zero, qt)], axis=0)
            kt = kwin[rr, r0:r0 + wk, cs]
            vt = vwin[rr, r0:r0 + wk, cs]
            s = lax.dot_general(q2, kt, (((1,), (1,)), ((), ())), preferred_element_type=F32) + bias
            m = jnp.max(s, axis=-1, keepdims=True)
            p = jnp.exp2(s - m).astype(BF16)
            ol = jnp.dot(p, jnp.concatenate([vt, ones], axis=1), preferred_element_type=F32)
            o = jnp.where(lo, ol[:sub, :LANES], ol[sub:, :LANES])
            l = jnp.where(lo, ol[:sub, LANES:], ol[sub:, LANES:])
            mm = jnp.where(lo, m[:sub], m[sub:])
            o_ref[0, rr, r0:r0 + sub, cs] = (o / l).astype(o_ref.dtype)
            lse_ref[0, rr, r0:r0 + sub, cs] = mm + jnp.log2(l)


def _band_attention(dil2d, g, *, tm, rows_per_step, sub, batch, seq):
    window, d = DIL_PAIRS[g]
    half = window // (2 * d)
    length = seq // d
    tq = min(rows_per_step, length)
    chunk = tm // d
    view = dil2d.reshape(batch, seq // tm, d, chunk, dil2d.shape[1])
    rows_blk = min(chunk, tq)
    nchunk = tq // rows_blk
    nres = min(d, max(1, rows_per_step // tq))

    def main(c):
        col = 3 * g + c
        if chunk >= tq:
            per = chunk // tq
            imap = lambda bi, r, i: (bi, i // per, r, i % per, col)
        else:
            imap = lambda bi, r, i: (bi, i, r, 0, col)
        return pl.BlockSpec((1, nchunk, nres, rows_blk, DIL_WIDTH), imap)

    def edge(c, start_of):
        col = 3 * g + c

        def imap(bi, r, i):
            start = start_of(i)
            return (bi, start // chunk, r, (start % chunk) // half, col)
        return pl.BlockSpec((1, 1, nres, half, DIL_WIDTH), imap)

    before = lambda i: jnp.maximum(i * tq - half, 0)
    after = lambda i: jnp.minimum((i + 1) * tq, length - half)
    out_spec = pl.BlockSpec((1, nres, tq, DIL_WIDTH), lambda bi, r, i: (bi, r, i, 0))
    return pl.pallas_call(
        functools.partial(_band_kernel, half=half, sub=sub),
        grid=(batch, d // nres, length // tq),
        in_specs=[main(0), edge(1, before), main(1), edge(1, after),
                  edge(2, before), main(2), edge(2, after)],
        out_specs=[out_spec, out_spec],
        out_shape=[jax.ShapeDtypeStruct((batch, d, length, DIL_WIDTH), BF16),
                   jax.ShapeDtypeStruct((batch, d, length, DIL_WIDTH), F32)],
        scratch_shapes=[pltpu.VMEM((nres, tq, DIL_WIDTH), BF16),
                        pltpu.VMEM((nres, tq + 2 * half, DIL_WIDTH), BF16),
                        pltpu.VMEM((nres, tq + 2 * half, DIL_WIDTH), BF16)],
        compiler_params=pltpu.CompilerParams(
            dimension_semantics=("arbitrary", "arbitrary", "arbitrary"),
            vmem_limit_bytes=VMEM_LIMIT),
        name=f"band_attn_d{d}",
    )(view, view, view, view, view, view, view)


def _silu(g):
    return g / (1.0 + jnp.exp(-g))


def _out_kernel(x_ref, a_ref, gate_ref, o1, l1, o2, l2, o3, l3, w_ref, fg_ref, y_ref, nat_scr,
                *, final):
    tm = x_ref.shape[0]

    def natural(ref, slot):
        d = ref.shape[1]
        if d == 1:
            return ref[0, 0]
        ntile = ref.shape[3] // LANES
        for r in range(d):
            for c in range(ntile):
                nat_scr[slot * ntile + c, pl.ds(r, tm // d, stride=d), :] = (
                    ref[0, r, :, c * LANES:(c + 1) * LANES].astype(F32))
        return jnp.concatenate([nat_scr[slot * ntile + c] for c in range(ntile)], axis=1)

    la, lb, lc = l1[0, 0], natural(l2, 0), natural(l3, 1)
    mx = jnp.maximum(jnp.maximum(la, lb), lc)
    ea, eb, ec = jnp.exp2(la - mx), jnp.exp2(lb - mx), jnp.exp2(lc - mx)
    bmix = (ea * o1[0, 0] + eb * natural(o2, 2) + ec * natural(o3, 3)) / (ea + eb + ec)
    gates = gate_ref[...].astype(F32)
    mix = jnp.concatenate([a_ref[...] * _silu(gates[:, :MLA_WIDTH]),
                           bmix * _silu(gates[:, MLA_WIDTH:])], axis=-1).astype(BF16)
    y = x_ref[...] + jnp.dot(mix, w_ref[...], preferred_element_type=F32)
    if final:
        y = _rms_f32(y, fg_ref[...])
    y_ref[...] = y


def _out_proj(x2d, a2d, gates, groups, w_out_all, layer, final_g, *, tm, seq, final):
    m, dm = x2d.shape
    nblk = seq // tm
    row = lambda n: pl.BlockSpec((tm, n), lambda i: (i, 0))
    args = [x2d, a2d, gates]
    specs = [row(dm), row(MLA_WIDTH), row(2 * MLA_WIDTH)]
    for (_, d), (o, lse) in zip(DIL_PAIRS, groups):
        spec = pl.BlockSpec((1, d, tm // d, DIL_WIDTH), lambda i: (i // nblk, 0, i % nblk, 0))
        args += [o, lse]
        specs += [spec, spec]
    args += [w_out_all, final_g.reshape(1, dm)]
    specs += [pl.BlockSpec((None, dm, dm), lambda i: (layer, 0, 0)),
              pl.BlockSpec((1, dm), lambda i: (0, 0))]
    return pl.pallas_call(
        functools.partial(_out_kernel, final=final),
        grid=(m // tm,),
        in_specs=specs,
        out_specs=row(dm),
        out_shape=jax.ShapeDtypeStruct((m, dm), F32),
        scratch_shapes=[pltpu.VMEM((4 * DIL_WIDTH // LANES, tm, LANES), F32)],
        compiler_params=pltpu.CompilerParams(
            dimension_semantics=("arbitrary",), vmem_limit_bytes=VMEM_LIMIT),
        name="merge_out_proj",
    )(*args)


def _arrange_w_in(w):
    o_kr = Q_LORA + KV_LORA
    o_ga = o_kr + MLA_ROPE
    o_dil = o_ga + MLA_WIDTH
    o_gb = o_dil + DIL_GROUPS * GROUP_W
    total = o_gb + MLA_WIDTH
    plan = [((o_dil, o_gb), 0), ((0, o_kr), MLA_NOPE), ((o_kr, o_ga), HEAD_PAD - MLA_NOPE - MLA_ROPE),
            ((o_ga, o_dil), 0), ((o_gb, total), 0)]
    width = sum(hi - lo + gap for (lo, hi), gap in plan)
    out, pos = None, 0
    for (lo, hi), gap in plan:
        seg = jnp.pad(w[..., lo:hi], ((0, 0), (0, 0), (pos, width - pos - (hi - lo))))
        out = seg if out is None else out + seg
        pos += hi - lo + gap
    return out.astype(BF16)


def _pad_heads(w, per_head, keep):
    lead = w.shape[:-1]
    wh = w.reshape(*lead, MLA_HEADS, per_head)[..., keep]
    wh = jnp.pad(wh, [(0, 0)] * (wh.ndim - 1) + [(0, HEAD_PAD - wh.shape[-1])])
    return wh.reshape(*lead, MLA_HEADS * HEAD_PAD)


def kernel(x, norm_g, w_in, q_norm_g, kv_norm_g, w_uq, w_ukv, w_out, final_g):
    b, s, dm = x.shape
    m = b * s
    tm_in = 1024
    mla_scale = (MLA_NOPE + MLA_ROPE) ** -0.5 * LOG2E
    dil_scale = DIL_HD ** -0.5 * LOG2E
    tabs_mqt = _mla_tables(s, mla_scale).transpose(0, 2, 1)
    tabs_mk = _mla_tables(s, 1.0)
    tabs_d = _dil_tables(s, math.sqrt(dil_scale), tm_in)

    w_in_all = _arrange_w_in(w_in)
    wqt_all = _pad_heads(w_uq, MLA_NOPE + MLA_ROPE, slice(None)).astype(BF16).swapaxes(1, 2)
    wk_all = _pad_heads(w_ukv, MLA_NOPE + MLA_V, slice(0, MLA_NOPE)).astype(BF16)
    wvt_all = _pad_heads(w_ukv, MLA_NOPE + MLA_V, slice(MLA_NOPE, None)).astype(BF16).swapaxes(1, 2)
    w_out_all = w_out.astype(BF16)

    x2d = x.reshape(m, dm)
    for layer in range(DEPTH):
        dil, lat, gates = _in_proj(x2d, norm_g[layer], w_in_all, layer, tabs_d, tm=tm_in, seq=s)

        qt, kk, vt = _mla_prep(lat.reshape(b, s, -1), q_norm_g[layer], kv_norm_g[layer],
                               wqt_all, wk_all, wvt_all, layer, tabs_mqt, tabs_mk, ts=1024)
        a = _mla_flash(qt, kk, vt, tq=1024, tk=2048, qw=512)

        groups = [_band_attention(dil, gi, tm=tm_in, rows_per_step=1024, sub=128, batch=b, seq=s)
                  for gi in range(DIL_GROUPS)]

        x2d = _out_proj(x2d, a.reshape(m, MLA_WIDTH), gates, groups, w_out_all, layer,
                        final_g, tm=512, seq=s, final=(layer == DEPTH - 1))
    return x2d.reshape(b, s, dm)
```

```python
import functools
import math

import numpy as np
import jax
import jax.numpy as jnp
from jax import lax
from jax.experimental import pallas as pl
from jax.experimental.pallas import tpu as pltpu

F32 = jnp.float32
BF16 = jnp.bfloat16

D_MODEL = 1024
DEPTH = 4
MLA_HEADS = 8
MLA_NOPE = 64
MLA_ROPE = 32
MLA_V = 64
Q_LORA = 384
KV_LORA = 256
MLA_WIDTH = MLA_HEADS * MLA_V
DIL_PAIRS = ((128, 1), (512, 4), (2048, 16))
DIL_GROUPS = 3
DIL_HEADS = 8
DIL_HD = 64
DIL_WIDTH = DIL_HEADS * DIL_HD
ROT_DIM = DIL_HD // 4
ROPE_THETA = 500000.0
EPS = 1e-6

LANES = 128
BF16_ROWS = 16
HEAD_PAD = 128
LAT_WIDTH = Q_LORA + KV_LORA + HEAD_PAD
VT_ROWS = -(-(MLA_V + 1) // BF16_ROWS) * BF16_ROWS
GROUP_W = 3 * DIL_WIDTH
ROW_SPLIT = 4
SUB_K = 256
LOG2E = math.log2(math.e)
NEG_INF = float("-inf")
VMEM_LIMIT = 48 * 1024 * 1024
VMEM_LIMIT_IN_PROJ = 58 * 1024 * 1024


def _rope_tables(seq, dim):
    inv = 1.0 / (ROPE_THETA ** (jnp.arange(0, dim, 2, dtype=F32) / dim))
    ang = jnp.arange(seq, dtype=F32)[:, None] * inv[None, :]
    return jnp.cos(ang), jnp.sin(ang)


def _mla_tables(seq, scale):
    cos, sin = _rope_tables(seq, MLA_ROPE)
    half = MLA_ROPE // 2
    ones = jnp.ones((seq, MLA_NOPE), F32)
    z = lambda n: jnp.zeros((seq, n), F32)
    c = jnp.concatenate([ones, cos, cos, z(HEAD_PAD - MLA_NOPE - MLA_ROPE)], axis=1)
    a = jnp.concatenate([z(MLA_NOPE), -sin, z(half), z(HEAD_PAD - MLA_NOPE - MLA_ROPE)], axis=1)
    b = jnp.concatenate([z(MLA_NOPE), z(half), sin, z(HEAD_PAD - MLA_NOPE - MLA_ROPE)], axis=1)
    return jnp.stack([c, a, b]) * scale


def _dil_tables(seq, scale, tm):
    half = ROT_DIM // 2
    rest = DIL_HD - ROT_DIM
    rep = LANES // DIL_HD
    inv = 1.0 / (ROPE_THETA ** (jnp.arange(0, ROT_DIM, 2, dtype=F32) / ROT_DIM))
    z = lambda n: jnp.zeros((seq, n), F32)
    out = []
    for _, d in DIL_PAIRS:
        pos = np.arange(seq).reshape(seq // tm, tm // d, d).transpose(0, 2, 1).reshape(seq)
        ang = jnp.asarray(pos, F32)[:, None] * inv[None, :]
        cos, sin = jnp.cos(ang) * scale, jnp.sin(ang) * scale
        c = jnp.concatenate([cos, cos, jnp.full((seq, rest), scale, F32)], axis=1)
        a = jnp.concatenate([-sin, z(half), z(rest)], axis=1)
        b = jnp.concatenate([z(half), sin, z(rest)], axis=1)
        out.append(jnp.stack([jnp.tile(c, (1, rep)), jnp.tile(a, (1, rep)), jnp.tile(b, (1, rep))]))
    return jnp.stack(out)


def _rotate3(blk, tab_c, tab_a, tab_b, shift):
    return (blk * tab_c
            + pltpu.roll(blk, LANES - shift, 1) * tab_a
            + pltpu.roll(blk, shift, 1) * tab_b)


def _rms_f32(x, g):
    ms = jnp.mean(x * x, axis=-1, keepdims=True)
    return x * lax.rsqrt(ms + EPS) * g


def _in_proj_kernel(x_ref, g_ref, w_ref, tab_ref, dil_ref, lat_ref, gate_ref, h_scr, xn_scr):
    tm, k = x_ref.shape
    j = pl.program_id(1)
    span = tm // ROW_SPLIT

    def group_step(gi):
        cols = slice(gi * GROUP_W, (gi + 1) * GROUP_W)
        for part in range(ROW_SPLIT):
            rs = slice(part * span, (part + 1) * span)
            y = jnp.dot(h_scr[gi, rs, :], w_ref[:, cols], preferred_element_type=F32)
            tc, ta, tb = tab_ref[0, 0, rs, :], tab_ref[0, 1, rs, :], tab_ref[0, 2, rs, :]
            for c in range(GROUP_W // LANES):
                sl = slice(c * LANES, (c + 1) * LANES)
                yc = y[:, sl]
                if c < 2 * DIL_WIDTH // LANES:
                    yc = _rotate3(yc, tc, ta, tb, ROT_DIM // 2)
                dil_ref[rs, sl] = yc.astype(BF16)

    def reorder(gi):
        d_prev, d = DIL_PAIRS[gi - 1][1], DIL_PAIRS[gi][1]
        ratio, rows = d // d_prev, tm // d
        last = gi == DIL_GROUPS - 1
        for r in range(d_prev):
            for a in range(ratio):
                dst = slice((a * d_prev + r) * rows, (a * d_prev + r + 1) * rows)
                for c in range(k // LANES):
                    y = xn_scr[gi - 1, c, pl.ds(r * (tm // d_prev) + a, rows, stride=ratio), :]
                    if not last:
                        xn_scr[gi, c, dst, :] = y
                    h_scr[gi, dst, c * LANES:(c + 1) * LANES] = y.astype(BF16)

    assert DIL_PAIRS[0][1] == 1 and all(DIL_PAIRS[g + 1][1] % DIL_PAIRS[g][1] == 0
                                        for g in range(DIL_GROUPS - 1))

    @pl.when(j == 0)
    def _():
        xn = _rms_f32(x_ref[...], g_ref[...])
        h_scr[0] = xn.astype(BF16)
        for c in range(k // LANES):
            xn_scr[0, c] = xn[:, c * LANES:(c + 1) * LANES]
        group_step(0)
        reorder(1)

    @pl.when(j == 1)
    def _():
        group_step(1)
        reorder(2)

    @pl.when(j == 2)
    def _():
        group_step(2)

    lat_start = DIL_GROUPS * GROUP_W
    @pl.when(j == DIL_GROUPS)
    def _():
        for ref, start in ((lat_ref, lat_start), (gate_ref, lat_start + LAT_WIDTH)):
            cols = slice(start, start + ref.shape[1])
            for part in range(ROW_SPLIT):
                rs = slice(part * span, (part + 1) * span)
                y = jnp.dot(h_scr[0, rs, :], w_ref[:, cols], preferred_element_type=F32)
                ref[rs, :] = y.astype(ref.dtype)


def _in_proj(x2d, g, w_all, layer, tabs, *, tm, seq):
    m, k = x2d.shape
    nblk = seq // tm
    return pl.pallas_call(
        _in_proj_kernel,
        grid=(m // tm, DIL_GROUPS + 1),
        in_specs=[
            pl.BlockSpec((tm, k), lambda i, j: (i, 0)),
            pl.BlockSpec((1, k), lambda i, j: (0, 0)),
            pl.BlockSpec((None, k, w_all.shape[2]), lambda i, j: (layer, 0, 0),
                         pipeline_mode=pl.Buffered(1)),
            pl.BlockSpec((1, 3, tm, LANES), lambda i, j: (jnp.minimum(j, DIL_GROUPS - 1), 0, i % nblk, 0)),
        ],
        out_specs=[
            pl.BlockSpec((tm, GROUP_W), lambda i, j: (i, jnp.minimum(j, DIL_GROUPS - 1))),
            pl.BlockSpec((tm, LAT_WIDTH), lambda i, j: (i, 0)),
            pl.BlockSpec((tm, 2 * MLA_WIDTH), lambda i, j: (i, 0)),
        ],
        out_shape=[
            jax.ShapeDtypeStruct((m, DIL_GROUPS * GROUP_W), BF16),
            jax.ShapeDtypeStruct((m, LAT_WIDTH), F32),
            jax.ShapeDtypeStruct((m, 2 * MLA_WIDTH), BF16),
        ],
        scratch_shapes=[pltpu.VMEM((DIL_GROUPS, tm, k), BF16),
                        pltpu.VMEM((DIL_GROUPS - 1, k // LANES, tm, LANES), F32)],
        compiler_params=pltpu.CompilerParams(
            dimension_semantics=("arbitrary", "arbitrary"), vmem_limit_bytes=VMEM_LIMIT_IN_PROJ),
        name="in_proj",
    )(x2d, g.reshape(1, k), w_all, tabs)


def _mla_prep_kernel(lat_ref, qg_ref, kvg_ref, wqt_ref, wk_ref, wvt_ref, tqt_ref, tk_ref,
                     qt_out, k_out, vt_out):
    lat = lat_ref[0]
    ts = lat.shape[0]
    cqn = _rms_f32(lat[:, :Q_LORA], qg_ref[...]).astype(BF16)
    ckvn = _rms_f32(lat[:, Q_LORA:Q_LORA + KV_LORA], kvg_ref[...]).astype(BF16)
    kr = lat[:, Q_LORA + KV_LORA:]
    nt = (((1,), (1,)), ((), ()))
    half = MLA_ROPE // 2

    qt = lax.dot_general(wqt_ref[...], cqn, nt, preferred_element_type=F32)
    qc, qa, qb = tqt_ref[0], tqt_ref[1], tqt_ref[2]
    for h in range(MLA_HEADS):
        blk = qt[h * HEAD_PAD:(h + 1) * HEAD_PAD, :]
        up = jnp.concatenate([blk[half:], blk[:half]], axis=0)
        down = jnp.concatenate([blk[-half:], blk[:-half]], axis=0)
        qt_out[0, h] = (blk * qc + up * qa + down * qb).astype(BF16)

    krp = _rotate3(kr, tk_ref[0], tk_ref[1], tk_ref[2], half)
    kk = jnp.dot(ckvn, wk_ref[...], preferred_element_type=F32)
    for h in range(MLA_HEADS):
        k_out[0, h] = (kk[:, h * HEAD_PAD:(h + 1) * HEAD_PAD] + krp).astype(BF16)

    vt = lax.dot_general(wvt_ref[...], ckvn, nt, preferred_element_type=F32)
    row = lax.broadcasted_iota(jnp.int32, (HEAD_PAD, ts), 0)
    ones_row = (row == MLA_V).astype(F32)
    for h in range(MLA_HEADS):
        vt_out[0, h] = (vt[h * HEAD_PAD:(h + 1) * HEAD_PAD, :] + ones_row).astype(BF16)


def _mla_prep(lat, qg, kvg, wqt_all, wk_all, wvt_all, layer, tabs_qt, tabs_k, *, ts):
    b, s, _ = lat.shape
    hp = MLA_HEADS * HEAD_PAD
    const = lambda *shape: pl.BlockSpec(shape, lambda bi, i: (0,) * len(shape))
    of_layer = lambda *shape: pl.BlockSpec((None,) + shape, lambda bi, i: (layer,) + (0,) * len(shape))
    rows = pl.BlockSpec((1, MLA_HEADS, ts, HEAD_PAD), lambda bi, i: (bi, 0, i, 0))
    cols = pl.BlockSpec((1, MLA_HEADS, HEAD_PAD, ts), lambda bi, i: (bi, 0, 0, i))
    rows_shape = jax.ShapeDtypeStruct((b, MLA_HEADS, s, HEAD_PAD), BF16)
    cols_shape = jax.ShapeDtypeStruct((b, MLA_HEADS, HEAD_PAD, s), BF16)
    return pl.pallas_call(
        _mla_prep_kernel,
        grid=(b, s // ts),
        in_specs=[
            pl.BlockSpec((1, ts, LAT_WIDTH), lambda bi, i: (bi, i, 0)),
            const(1, Q_LORA), const(1, KV_LORA),
            of_layer(hp, Q_LORA), of_layer(KV_LORA, hp), of_layer(hp, KV_LORA),
            pl.BlockSpec((3, LANES, ts), lambda bi, i: (0, 0, i)),
            pl.BlockSpec((3, ts, LANES), lambda bi, i: (0, i, 0)),
        ],
        out_specs=[cols, rows, cols],
        out_shape=[cols_shape, rows_shape, cols_shape],
        compiler_params=pltpu.CompilerParams(
            dimension_semantics=("arbitrary", "arbitrary"), vmem_limit_bytes=VMEM_LIMIT),
        name="mla_prep",
    )(lat, qg.reshape(1, -1), kvg.reshape(1, -1), wqt_all, wk_all, wvt_all, tabs_qt, tabs_k)


def _flash_kernel(qt_ref, k_ref, vt_ref, o_ref, *, tk, qw):
    tq = qt_ref.shape[3]
    s_len = k_ref.shape[2]
    nsub = tk // SUB_K
    nchunk = s_len // tk
    items = [(qb, hh, c) for qb in range(tq // qw) for hh in range(2) for c in range(nchunk)]

    def scores(item, r):
        qb, hh, c = item
        rows = slice(c * tk + r * SUB_K, c * tk + (r + 1) * SUB_K)
        qt = qt_ref[0, hh, :, qb * qw:(qb + 1) * qw]
        return jnp.dot(k_ref[0, hh, rows, :], qt, preferred_element_type=F32)

    def weighted(item, r, p):
        _, hh, c = item
        cols = slice(c * tk + r * SUB_K, c * tk + (r + 1) * SUB_K)
        return jnp.dot(vt_ref[0, hh, 0:VT_ROWS, cols], p, preferred_element_type=F32)

    def fold(state, item, alpha, pv):
        key = item[:2]
        m, acc = state[key]
        state[key] = (m, alpha * acc + pv)

    state = {}
    s_parts = [scores(items[0], r) for r in range(nsub)]
    prev = None
    for t, item in enumerate(items):
        key = item[:2]
        if key not in state:
            state[key] = (jnp.full((1, qw), NEG_INF, F32), jnp.zeros((VT_ROWS, qw), F32))
        m, acc = state[key]
        m_new = m
        for s in s_parts:
            m_new = jnp.maximum(m_new, jnp.max(s, axis=0, keepdims=True))
        alpha = jnp.exp2(m - m_new)
        state[key] = (m_new, acc)
        nxt, p_parts, pv = [], [], None
        for r in range(nsub):
            shift = m_new
            if t + 1 < len(items):
                nxt.append(scores(items[t + 1], r))
                shift = m_new + nxt[r][0:1, :] * 0.0
            if prev is not None:
                part = weighted(prev[0], r, prev[1][r])
                pv = part if pv is None else pv + part
            p_parts.append(jnp.exp2(s_parts[r] - shift).astype(BF16))
        if prev is not None:
            fold(state, prev[0], prev[2], pv)
        s_parts, prev = nxt, (item, p_parts, alpha)
    pv = None
    for r in range(nsub):
        part = weighted(prev[0], r, prev[1][r])
        pv = part if pv is None else pv + part
    fold(state, prev[0], prev[2], pv)

    for qb in range(tq // qw):
        outs = []
        for hh in range(2):
            acc = state[(qb, hh)][1]
            outs.append(acc[:MLA_V] / acc[MLA_V:MLA_V + 1])
        o_ref[0, qb * qw:(qb + 1) * qw, :] = jnp.concatenate(outs, axis=0).T.astype(o_ref.dtype)


def _mla_flash(qt, k, vt, *, tq, tk, qw):
    b, h, s, _ = k.shape
    return pl.pallas_call(
        functools.partial(_flash_kernel, tk=tk, qw=qw),
        grid=(b, h // 2, s // tq),
        in_specs=[
            pl.BlockSpec((1, 2, HEAD_PAD, tq), lambda bi, hp, i: (bi, hp, 0, i)),
            pl.BlockSpec((1, 2, s, HEAD_PAD), lambda bi, hp, i: (bi, hp, 0, 0)),
            pl.BlockSpec((1, 2, HEAD_PAD, s), lambda bi, hp, i: (bi, hp, 0, 0)),
        ],
        out_specs=pl.BlockSpec((1, tq, 2 * MLA_V), lambda bi, hp, i: (bi, i, hp)),
        out_shape=jax.ShapeDtypeStruct((b, s, MLA_WIDTH), BF16),
        compiler_params=pltpu.CompilerParams(
            dimension_semantics=("arbitrary", "arbitrary", "arbitrary"),
            vmem_limit_bytes=VMEM_LIMIT),
        name="mla_flash",
    )(qt, k, vt)


def _band_kernel(q_ref, kp_ref, km_ref, kn_ref, vp_ref, vm_ref, vn_ref, o_ref, lse_ref,
                 qwin, kwin, vwin, *, half, sub):
    nres, tq = qwin.shape[0], qwin.shape[1]
    nchunk, chunk = q_ref.shape[1], q_ref.shape[3]
    i = pl.program_id(2)
    for rr in range(nres):
        for c in range(nchunk):
            rows = slice(c * chunk, (c + 1) * chunk)
            qwin[rr, rows] = q_ref[0, c, rr]
            kwin[rr, half + c * chunk:half + (c + 1) * chunk] = km_ref[0, c, rr]
            vwin[rr, half + c * chunk:half + (c + 1) * chunk] = vm_ref[0, c, rr]
        kwin[rr, 0:half] = kp_ref[0, 0, rr]
        kwin[rr, half + tq:] = kn_ref[0, 0, rr]
        vwin[rr, 0:half] = vp_ref[0, 0, rr]
        vwin[rr, half + tq:] = vn_ref[0, 0, rr]

    wk = sub + 2 * half
    lo = lax.broadcasted_iota(jnp.int32, (sub, LANES), 1) < DIL_HD
    ones = jnp.ones((wk, LANES), BF16)
    row = lax.broadcasted_iota(jnp.int32, (2 * sub, wk), 0) % sub
    col = lax.broadcasted_iota(jnp.int32, (2 * sub, wk), 1)
    rel = col - half - row
    in_band = (rel >= -half) & (rel <= half)
    bias_mid = jnp.where(in_band, 0.0, NEG_INF)
    bias_first = jnp.where(in_band & (col >= half), 0.0, NEG_INF)
    bias_last = jnp.where(in_band & (col < half + sub), 0.0, NEG_INF)
    nstep = tq // sub
    assert nstep >= 2
    at_start = i == 0
    at_end = i == pl.num_programs(2) - 1

    for rr, t in [(rr, t) for rr in range(nres) for t in range(nstep)]:
        r0 = t * sub
        bias = bias_mid
        if t == 0:
            bias = jnp.where(at_start, bias_first, bias)
        if t == nstep - 1:
            bias = jnp.where(at_end, bias_last, bias)
        for c in range(DIL_WIDTH // LANES):
            cs = slice(c * LANES, (c + 1) * LANES)
            qt = qwin[rr, r0:r0 + sub, cs]
            zero = jnp.zeros_like(qt)
            q2 = jnp.concatenate([jnp.where(lo, qt, zero), jnp.where(lo, zero, qt)], axis=0)
            kt = kwin[rr, r0:r0 + wk, cs]
            vt = vwin[rr, r0:r0 + wk, cs]
            s = lax.dot_general(q2, kt, (((1,), (1,)), ((), ())), preferred_element_type=F32) + bias
            m = jnp.max(s, axis=-1, keepdims=True)
            p = jnp.exp2(s - m).astype(BF16)
            ol = jnp.dot(p, jnp.concatenate([vt, ones], axis=1), preferred_element_type=F32)
            o = jnp.where(lo, ol[:sub, :LANES], ol[sub:, :LANES])
            l = jnp.where(lo, ol[:sub, LANES:], ol[sub:, LANES:])
            mm = jnp.where(lo, m[:sub], m[sub:])
            o_ref[0, rr, r0:r0 + sub, cs] = (o / l).astype(o_ref.dtype)
            lse_ref[0, rr, r0:r0 + sub, cs] = mm + jnp.log2(l)


def _band_attention(dil2d, g, *, tm, rows_per_step, sub, batch, seq):
    window, d = DIL_PAIRS[g]
    half = window // (2 * d)
    length = seq // d
    tq = min(rows_per_step, length)
    chunk = tm // d
    view = dil2d.reshape(batch, seq // tm, d, chunk, dil2d.shape[1])
    rows_blk = min(chunk, tq)
    nchunk = tq // rows_blk
    nres = min(d, max(1, rows_per_step // tq))

    def main(c):
        col = 3 * g + c
        if chunk >= tq:
            per = chunk // tq
            imap = lambda bi, r, i: (bi, i // per, r, i % per, col)
        else:
            imap = lambda bi, r, i: (bi, i, r, 0, col)
        return pl.BlockSpec((1, nchunk, nres, rows_blk, DIL_WIDTH), imap)

    def edge(c, start_of):
        col = 3 * g + c

        def imap(bi, r, i):
            start = start_of(i)
            return (bi, start // chunk, r, (start % chunk) // half, col)
        return pl.BlockSpec((1, 1, nres, half, DIL_WIDTH), imap)

    before = lambda i: jnp.maximum(i * tq - half, 0)
    after = lambda i: jnp.minimum((i + 1) * tq, length - half)
    out_spec = pl.BlockSpec((1, nres, tq, DIL_WIDTH), lambda bi, r, i: (bi, r, i, 0))
    return pl.pallas_call(
        functools.partial(_band_kernel, half=half, sub=sub),
        grid=(batch, d // nres, length // tq),
        in_specs=[main(0), edge(1, before), main(1), edge(1, after),
                  edge(2, before), main(2), edge(2, after)],
        out_specs=[out_spec, out_spec],
        out_shape=[jax.ShapeDtypeStruct((batch, d, length, DIL_WIDTH), BF16),
                   jax.ShapeDtypeStruct((batch, d, length, DIL_WIDTH), F32)],
        scratch_shapes=[pltpu.VMEM((nres, tq, DIL_WIDTH), BF16),
                        pltpu.VMEM((nres, tq + 2 * half, DIL_WIDTH), BF16),
                        pltpu.VMEM((nres, tq + 2 * half, DIL_WIDTH), BF16)],
        compiler_params=pltpu.CompilerParams(
            dimension_semantics=("arbitrary", "arbitrary", "arbitrary"),
            vmem_limit_bytes=VMEM_LIMIT),
        name=f"band_attn_d{d}",
    )(view, view, view, view, view, view, view)


def _silu(g):
    return g / (1.0 + jnp.exp(-g))


def _out_kernel(x_ref, a_ref, gate_ref, o1, l1, o2, l2, o3, l3, w_ref, fg_ref, y_ref, nat_scr,
                *, final):
    tm = x_ref.shape[0]

    def natural(ref, slot):
        d = ref.shape[1]
        if d == 1:
            return ref[0, 0]
        ntile = ref.shape[3] // LANES
        for r in range(d):
            for c in range(ntile):
                nat_scr[slot * ntile + c, pl.ds(r, tm // d, stride=d), :] = (
                    ref[0, r, :, c * LANES:(c + 1) * LANES].astype(F32))
        return jnp.concatenate([nat_scr[slot * ntile + c] for c in range(ntile)], axis=1)

    la, lb, lc = l1[0, 0], natural(l2, 0), natural(l3, 1)
    mx = jnp.maximum(jnp.maximum(la, lb), lc)
    ea, eb, ec = jnp.exp2(la - mx), jnp.exp2(lb - mx), jnp.exp2(lc - mx)
    bmix = (ea * o1[0, 0] + eb * natural(o2, 2) + ec * natural(o3, 3)) / (ea + eb + ec)
    gates = gate_ref[...].astype(F32)
    mix = jnp.concatenate([a_ref[...] * _silu(gates[:, :MLA_WIDTH]),
                           bmix * _silu(gates[:, MLA_WIDTH:])], axis=-1).astype(BF16)
    y = x_ref[...] + jnp.dot(mix, w_ref[...], preferred_element_type=F32)
    if final:
        y = _rms_f32(y, fg_ref[...])
    y_ref[...] = y


def _out_proj(x2d, a2d, gates, groups, w_out_all, layer, final_g, *, tm, seq, final):
    m, dm = x2d.shape
    nblk = seq // tm
    row = lambda n: pl.BlockSpec((tm, n), lambda i: (i, 0))
    args = [x2d, a2d, gates]
    specs = [row(dm), row(MLA_WIDTH), row(2 * MLA_WIDTH)]
    for (_, d), (o, lse) in zip(DIL_PAIRS, groups):
        spec = pl.BlockSpec((1, d, tm // d, DIL_WIDTH), lambda i: (i // nblk, 0, i % nblk, 0))
        args += [o, lse]
        specs += [spec, spec]
    args += [w_out_all, final_g.reshape(1, dm)]
    specs += [pl.BlockSpec((None, dm, dm), lambda i: (layer, 0, 0)),
              pl.BlockSpec((1, dm), lambda i: (0, 0))]
    return pl.pallas_call(
        functools.partial(_out_kernel, final=final),
        grid=(m // tm,),
        in_specs=specs,
        out_specs=row(dm),
        out_shape=jax.ShapeDtypeStruct((m, dm), F32),
        scratch_shapes=[pltpu.VMEM((4 * DIL_WIDTH // LANES, tm, LANES), F32)],
        compiler_params=pltpu.CompilerParams(
            dimension_semantics=("arbitrary",), vmem_limit_bytes=VMEM_LIMIT),
        name="merge_out_proj",
    )(*args)


def _arrange_w_in(w):
    o_kr = Q_LORA + KV_LORA
    o_ga = o_kr + MLA_ROPE
    o_dil = o_ga + MLA_WIDTH
    o_gb = o_dil + DIL_GROUPS * GROUP_W
    total = o_gb + MLA_WIDTH
    plan = [((o_dil, o_gb), 0), ((0, o_kr), MLA_NOPE), ((o_kr, o_ga), HEAD_PAD - MLA_NOPE - MLA_ROPE),
            ((o_ga, o_dil), 0), ((o_gb, total), 0)]
    width = sum(hi - lo + gap for (lo, hi), gap in plan)
    out, pos = None, 0
    for (lo, hi), gap in plan:
        seg = jnp.pad(w[..., lo:hi], ((0, 0), (0, 0), (pos, width - pos - (hi - lo))))
        out = seg if out is None else out + seg
        pos += hi - lo + gap
    return out.astype(BF16)


def _pad_heads(w, per_head, keep):
    lead = w.shape[:-1]
    wh = w.reshape(*lead, MLA_HEADS, per_head)[..., keep]
    wh = jnp.pad(wh, [(0, 0)] * (wh.ndim - 1) + [(0, HEAD_PAD - wh.shape[-1])])
    return wh.reshape(*lead, MLA_HEADS * HEAD_PAD)


def kernel(x, norm_g, w_in, q_norm_g, kv_norm_g, w_uq, w_ukv, w_out, final_g):
    b, s, dm = x.shape
    m = b * s
    tm_in = 1024
    mla_scale = (MLA_NOPE + MLA_ROPE) ** -0.5 * LOG2E
    dil_scale = DIL_HD ** -0.5 * LOG2E
    tabs_mqt = _mla_tables(s, mla_scale).transpose(0, 2, 1)
    tabs_mk = _mla_tables(s, 1.0)
    tabs_d = _dil_tables(s, math.sqrt(dil_scale), tm_in)

    w_in_all = _arrange_w_in(w_in)
    wqt_all = _pad_heads(w_uq, MLA_NOPE + MLA_ROPE, slice(None)).astype(BF16).swapaxes(1, 2)
    wk_all = _pad_heads(w_ukv, MLA_NOPE + MLA_V, slice(0, MLA_NOPE)).astype(BF16)
    wvt_all = _pad_heads(w_ukv, MLA_NOPE + MLA_V, slice(MLA_NOPE, None)).astype(BF16).swapaxes(1, 2)
    w_out_all = w_out.astype(BF16)

    x2d = x.reshape(m, dm)
    for layer in range(DEPTH):
        dil, lat, gates = _in_proj(x2d, norm_g[layer], w_in_all, layer, tabs_d, tm=tm_in, seq=s)

        qt, kk, vt = _mla_prep(lat.reshape(b, s, -1), q_norm_g[layer], kv_norm_g[layer],
                               wqt_all, wk_all, wvt_all, layer, tabs_mqt, tabs_mk, ts=1024)
        a = _mla_flash(qt, kk, vt, tq=1024, tk=2048, qw=512)

        groups = [_band_attention(dil, gi, tm=tm_in, rows_per_step=1024, sub=128, batch=b, seq=s)
                  for gi in range(DIL_GROUPS)]

        x2d = _out_proj(x2d, a.reshape(m, MLA_WIDTH), gates, groups, w_out_all, layer,
                        final_g, tm=512, seq=s, final=(layer == DEPTH - 1))
    return x2d.reshape(b, s, dm)
```
